```python
import math
import jax
import jax.numpy as jnp
from jax import lax
import numpy as np

D_MODEL = 2048
BATCH = 4
SEQ = 4096
DEPTH = 1

HEAD_DIM = 128
ROPE_DIM = HEAD_DIM // 4
NOPE_DIM = HEAD_DIM - ROPE_DIM
ROPE_THETA = 500000.0
EPS = 1e-6
NEG = -1e30

N_HEADS_A = 8
WIDTH_A = N_HEADS_A * HEAD_DIM
Q_RANK = 512
KV_RANK = 256
IDX_HEADS = 16
IDX_DIM = 64
IDX_ROPE = IDX_DIM // 4
TOPK_MAX = 256
QUERY_BLOCK = 128

N_HEADS_B = 8
WIDTH_B = N_HEADS_B * HEAD_DIM
DILATED_CONFIGS = ((128, 1), (512, 4), (2048, 16))
BAND_BLOCK = 128

MIX_WIDTH = WIDTH_A + WIDTH_B
IN_SPLITS = (Q_RANK, KV_RANK, ROPE_DIM, IDX_DIM, IDX_HEADS, WIDTH_A,
             WIDTH_B, WIDTH_B, WIDTH_B, WIDTH_B)
IN_WIDTH = sum(IN_SPLITS)

kernel_name = 'hybrid_dsa_dilated_parallel_heads'


def rms_norm(x, g):
    xf = x.astype(jnp.float32)
    y = xf * lax.rsqrt(jnp.mean(xf * xf, axis=-1, keepdims=True) + EPS)
    return (y * g.astype(jnp.float32)).astype(x.dtype)


def partial_rope(x, n_rot):
    L = x.shape[1]
    inv = ROPE_THETA ** (-jnp.arange(0, n_rot, 2, dtype=jnp.float32) / n_rot)
    ang = jnp.arange(L, dtype=jnp.float32)[:, None] * inv[None, :]
    cos = jnp.cos(ang).astype(x.dtype)[None, :, None, :]
    sin = jnp.sin(ang).astype(x.dtype)[None, :, None, :]
    half = n_rot // 2
    x1 = x[..., :half]
    x2 = x[..., half:n_rot]
    return jnp.concatenate([x1 * cos - x2 * sin, x2 * cos + x1 * sin, x[..., n_rot:]], axis=-1)


def dsa_branch(cq, ckv, krope, kidx, widx, g_q, g_kv, w_uq, w_uq_idx, w_uk, w_uv):
    Bsz, L, _ = cq.shape
    k_sel = min(TOPK_MAX, L // 4)
    nb = L // QUERY_BLOCK
    scale = HEAD_DIM ** -0.5
    cq = rms_norm(cq, g_q)
    q = partial_rope((cq @ w_uq).reshape(Bsz, L, N_HEADS_A, HEAD_DIM), ROPE_DIM)
    q_rope, q_nope = q[..., :ROPE_DIM], q[..., ROPE_DIM:]
    q_lat = jnp.einsum('blhn,hnr->blhr', q_nope, w_uk)
    q_idx = partial_rope((cq @ w_uq_idx).reshape(Bsz, L, IDX_HEADS, IDX_DIM), IDX_ROPE)
    w_idx = widx * (IDX_HEADS ** -0.5 * IDX_DIM ** -0.5)
    c_kv = rms_norm(ckv, g_kv)
    k_rope = partial_rope(krope[:, :, None, :], ROPE_DIM)[:, :, 0]
    k_idx = partial_rope(kidx[:, :, None, :], IDX_ROPE)[:, :, 0]
    key_pos = jnp.arange(L)
    gather = jax.vmap(lambda a, i: a[i])

    def attend_block(blk):
        ql, qr, qi, wi, qpos = blk
        logits = jnp.einsum('bqhd,bsd->bqhs', qi, k_idx)
        isc = jnp.einsum('bqh,bqhs->bqs', wi, jax.nn.relu(logits)).astype(jnp.float32)
        causal = key_pos[None, :] <= qpos[:, None]
        isc = jnp.where(causal[None], isc, NEG)
        _, idx = lax.top_k(isc, k_sel)
        valid = idx <= qpos[None, :, None]
        ckv_sel = gather(c_kv, idx)
        kr_sel = gather(k_rope, idx)
        s = (jnp.einsum('bqhr,bqkr->bqhk', ql, ckv_sel)
             + jnp.einsum('bqhe,bqke->bqhk', qr, kr_sel)).astype(jnp.float32) * scale
        s = jnp.where(valid[:, :, None, :], s, NEG)
        p = jax.nn.softmax(s, axis=-1).astype(ckv_sel.dtype)
        return jnp.einsum('bqhk,bqkr->bqhr', p, ckv_sel)

    def to_blocks(t):
        return jnp.moveaxis(t.reshape(Bsz, nb, QUERY_BLOCK, *t.shape[2:]), 1, 0)

    qpos = key_pos.reshape(nb, QUERY_BLOCK)
    o_lat = lax.map(attend_block, (to_blocks(q_lat), to_blocks(q_rope), to_blocks(q_idx),
                                   to_blocks(w_idx), qpos))
    o_lat = jnp.moveaxis(o_lat, 0, 1).reshape(Bsz, L, N_HEADS_A, KV_RANK)
    return jnp.einsum('blhr,hrv->blhv', o_lat, w_uv).reshape(Bsz, L, WIDTH_A)


def dilated_window_attention(q, k, v, dil, n_back):
    Bsz, L, H, Dh = q.shape
    M = L // dil
    N = Bsz * dil
    P = BAND_BLOCK
    nb = -(-M // P)
    Mp = nb * P
    scale = Dh ** -0.5

    def to_res(t):
        return jnp.swapaxes(t.reshape(Bsz, M, dil, H, Dh), 1, 2).reshape(N, M, H, Dh)

    def key_blocks(t):
        tp = jnp.pad(t, ((0, 0), (P, Mp - M), (0, 0), (0, 0))).reshape(N, nb + 1, P, H, Dh)
        return jnp.concatenate([tp[:, :-1], tp[:, 1:]], axis=2)

    qb = jnp.pad(to_res(q), ((0, 0), (0, Mp - M), (0, 0), (0, 0))).reshape(N, nb, P, H, Dh)
    kb = key_blocks(to_res(k))
    vb = key_blocks(to_res(v))
    qpos = jnp.arange(Mp).reshape(nb, P)
    kpos = jnp.arange(nb)[:, None] * P - P + jnp.arange(2 * P)[None, :]
    rel = qpos[:, :, None] - kpos[:, None, :]
    allowed = (rel >= 0) & (rel <= n_back) & (kpos[:, None, :] >= 0)
    s = jnp.einsum('nbqhd,nbkhd->nbhqk', qb, kb).astype(jnp.float32) * scale
    s = jnp.where(allowed[None, :, None], s, NEG)
    lse = jax.nn.logsumexp(s, axis=-1)
    p = jnp.exp(s - lse[..., None]).astype(v.dtype)
    o = jnp.einsum('nbhqk,nbkhd->nbqhd', p, vb).reshape(N, Mp, H, Dh)[:, :M]
    lse = jnp.swapaxes(lse, 2, 3).reshape(N, Mp, H)[:, :M]
    o = jnp.swapaxes(o.reshape(Bsz, dil, M, H, Dh), 1, 2).reshape(Bsz, L, H, Dh)
    lse = jnp.swapaxes(lse.reshape(Bsz, dil, M, H), 1, 2).reshape(Bsz, L, H)
    return o, lse


def dilated_branch(qb, kb, vb):
    Bsz, L, _ = qb.shape
    q = partial_rope(qb.reshape(Bsz, L, N_HEADS_B, HEAD_DIM), ROPE_DIM)
    k = partial_rope(kb.reshape(Bsz, L, N_HEADS_B, HEAD_DIM), ROPE_DIM)
    v = vb.reshape(Bsz, L, N_HEADS_B, HEAD_DIM)
    outs, lses = [], []
    for window, dil in DILATED_CONFIGS:
        o, lse = dilated_window_attention(q, k, v, dil, window // dil)
        outs.append(o)
        lses.append(lse)
    alpha = jax.nn.softmax(jnp.stack(lses), axis=0)
    o = jnp.sum(alpha[..., None] * jnp.stack(outs).astype(jnp.float32), axis=0)
    return o.astype(q.dtype).reshape(Bsz, L, WIDTH_B)


def hybrid_layer(x, c, w_ada, b_ada, g_pre, g_post, w_in, g_q, g_kv, w_uq, w_uq_idx,
                 w_uk, w_uv, w_out):
    mod = jax.nn.silu(c) @ w_ada + b_ada
    shift, scale, gate = jnp.split(mod, 3, axis=-1)
    h = rms_norm(x, g_pre) * (1 + scale[:, None, :]) + shift[:, None, :]
    proj = h @ w_in
    split_points = [int(v) for v in np.cumsum(IN_SPLITS)[:-1]]
    (cq, ckv, krope, kidx, widx, gate_a, qb, kb, vb, gate_b) = jnp.split(proj, split_points, axis=-1)
    o_a = dsa_branch(cq, ckv, krope, kidx, widx, g_q, g_kv, w_uq, w_uq_idx, w_uk, w_uv) * jax.nn.silu(gate_a)
    o_b = dilated_branch(qb, kb, vb) * jax.nn.silu(gate_b)
    y = jnp.concatenate([o_a, o_b], axis=-1) @ w_out
    return x + gate[:, None, :] * rms_norm(y, g_post)


def setup_inputs(seed: int = 0) -> dict:
    key = jax.random.key(seed)
    ks = jax.random.split(key, 14)

    def nrm(k, shape, s):
        return jax.random.normal(k, shape, jnp.float32) * s

    return {
        'x': nrm(ks[0], (BATCH, SEQ, D_MODEL), 1.0),
        'c': nrm(ks[1], (BATCH, D_MODEL), 1.0),
        'w_ada': nrm(ks[2], (DEPTH, D_MODEL, 3 * D_MODEL), D_MODEL ** -0.5),
        'b_ada': nrm(ks[3], (DEPTH, 3 * D_MODEL), 0.01),
        'g_pre': 1.0 + nrm(ks[4], (DEPTH, D_MODEL), 0.01),
        'g_post': 1.0 + nrm(ks[5], (DEPTH, D_MODEL), 0.01),
        'w_in': nrm(ks[6], (DEPTH, D_MODEL, IN_WIDTH), D_MODEL ** -0.5),
        'g_q': 1.0 + nrm(ks[7], (DEPTH, Q_RANK), 0.01),
        'g_kv': 1.0 + nrm(ks[8], (DEPTH, KV_RANK), 0.01),
        'w_uq': nrm(ks[9], (DEPTH, Q_RANK, WIDTH_A), Q_RANK ** -0.5),
        'w_uq_idx': nrm(ks[10], (DEPTH, Q_RANK, IDX_HEADS * IDX_DIM), Q_RANK ** -0.5),
        'w_uk': nrm(ks[11], (DEPTH, N_HEADS_A, NOPE_DIM, KV_RANK), KV_RANK ** -0.5),
        'w_uv': nrm(ks[12], (DEPTH, N_HEADS_A, KV_RANK, HEAD_DIM), KV_RANK ** -0.5),
        'w_out': nrm(ks[13], (DEPTH, MIX_WIDTH, D_MODEL), MIX_WIDTH ** -0.5),
    }


def reference(x, c, w_ada, b_ada, g_pre, g_post, w_in, g_q, g_kv, w_uq, w_uq_idx,
              w_uk, w_uv, w_out):
    for layer in range(DEPTH):
        x = hybrid_layer(x, c, w_ada[layer], b_ada[layer], g_pre[layer], g_post[layer],
                         w_in[layer], g_q[layer], g_kv[layer], w_uq[layer], w_uq_idx[layer],
                         w_uk[layer], w_uv[layer], w_out[layer])
    return x
```

```python
import functools
import math

import numpy as np
import jax
import jax.numpy as jnp
from jax import lax
from jax.experimental import pallas as pl
from jax.experimental.pallas import tpu as pltpu

F32 = jnp.float32
BF16 = jnp.bfloat16
I32 = jnp.int32

HEAD_DIM = 128
ROPE_DIM = HEAD_DIM // 4
ROPE_THETA = 500000.0
EPS = 1e-6
NEG = -1e30
N_HEADS_A = 8
WIDTH_A = N_HEADS_A * HEAD_DIM
Q_RANK = 512
KV_RANK = 256
IDX_HEADS = 16
IDX_DIM = 64
IDX_ROPE = IDX_DIM // 4
TOPK_MAX = 256
QUERY_BLOCK = 128
N_HEADS_B = 8
WIDTH_B = N_HEADS_B * HEAD_DIM
DILATED_CONFIGS = ((128, 1), (512, 4), (2048, 16))
N_BACK = 128
SMALL_W = 1024
IN_PAD = SMALL_W + 5 * 1024
KCAT = 384
KIDX_PAD = 128
INT_MIN = -(2 ** 31)
LOG2E = 1.4426950408889634

LANE = 128
VMEM_LIMIT = 56 * 1024 * 1024
PROJ_TM = 1024
PROJ_TN = 512
PREP_TQ = 512
KEY_CHUNK = 512
OUT_TM = 512
ROW_CHUNK = 512
NORM_ROWS = 128


def _cparams(sem):
    return pltpu.CompilerParams(dimension_semantics=sem, vmem_limit_bytes=VMEM_LIMIT)


def _silu(g):
    return g * jax.nn.sigmoid(g)


def _mod_kernel(c_ref, w_ref, b_ref, o_ref):
    c = c_ref[...]
    o_ref[...] = jnp.dot(_silu(c), w_ref[...], preferred_element_type=F32,
                         precision=lax.Precision.HIGHEST) + b_ref[...]


def _mod_call(c_pad, w_ada, b_ada):
    rows, d = c_pad.shape
    n = w_ada.shape[1]
    tn = 1024
    return pl.pallas_call(
        _mod_kernel,
        grid=(n // tn,),
        in_specs=[pl.BlockSpec((rows, d), lambda j: (0, 0)),
                  pl.BlockSpec((d, tn), lambda j: (0, j)),
                  pl.BlockSpec((1, tn), lambda j: (0, j))],
        out_specs=pl.BlockSpec((rows, tn), lambda j: (0, j)),
        out_shape=jax.ShapeDtypeStruct((rows, n), F32),
        compiler_params=_cparams(("arbitrary",)),
        name="mod",
    )(c_pad, w_ada, b_ada)


def _proj_kernel(x_ref, mod_ref, g_ref, w_ref, small_ref, qkv_ref, gates_ref, h_ref, *,
                 n_small, n_qkv):
    j = pl.program_id(1)

    @pl.when(j == 0)
    def _():
        shift = mod_ref[0, 0:1, :]
        scale1 = 1.0 + mod_ref[0, 1:2, :]
        g = g_ref[...]

        def body(r, carry):
            rows = pl.ds(pl.multiple_of(r * NORM_ROWS, NORM_ROWS), NORM_ROWS)
            x = x_ref[rows, :]
            ms = jnp.mean(x * x, axis=-1, keepdims=True)
            y = x * lax.rsqrt(ms + EPS) * g
            h_ref[rows, :] = (y * scale1 + shift).astype(BF16)
            return carry

        lax.fori_loop(0, x_ref.shape[0] // NORM_ROWS, body, 0)

    acc = jnp.dot(h_ref[...], w_ref[...], preferred_element_type=F32)

    @pl.when(j < n_small)
    def _():
        small_ref[...] = acc

    @pl.when((j >= n_small) & (j < n_small + n_qkv))
    def _():
        qkv_ref[...] = acc

    @pl.when(j >= n_small + n_qkv)
    def _():
        gates_ref[...] = acc.astype(BF16)


def _proj_call(x2, mod3, g_pre, w_all, seq):
    rows, d = x2.shape
    tm = min(PROJ_TM, seq)
    tn = PROJ_TN
    n_small = SMALL_W // tn
    n_qkv = 3 * WIDTH_B // tn
    n_gate = (WIDTH_A + WIDTH_B) // tn
    tiles_per_batch = seq // tm
    kern = functools.partial(_proj_kernel, n_small=n_small, n_qkv=n_qkv)
    return pl.pallas_call(
        kern,
        grid=(rows // tm, n_small + n_qkv + n_gate),
        in_specs=[pl.BlockSpec((tm, d), lambda i, j: (i, 0)),
                  pl.BlockSpec((1, 3, d), lambda i, j: (i // tiles_per_batch, 0, 0)),
                  pl.BlockSpec((1, d), lambda i, j: (0, 0)),
                  pl.BlockSpec((d, tn), lambda i, j: (0, j))],
        out_specs=[pl.BlockSpec((tm, tn), lambda i, j: (i, jnp.minimum(j, n_small - 1))),
                   pl.BlockSpec((tm, tn), lambda i, j: (i, jnp.clip(j - n_small, 0, n_qkv - 1))),
                   pl.BlockSpec((tm, tn),
                                lambda i, j: (i, jnp.clip(j - n_small - n_qkv, 0, n_gate - 1)))],
        out_shape=[jax.ShapeDtypeStruct((rows, SMALL_W), F32),
                   jax.ShapeDtypeStruct((rows, 3 * WIDTH_B), F32),
                   jax.ShapeDtypeStruct((rows, WIDTH_A + WIDTH_B), BF16)],
        scratch_shapes=[pltpu.VMEM((tm, d), BF16)],
        compiler_params=_cparams(("parallel", "arbitrary")),
        name="in_proj",
    )(x2, mod3, g_pre, w_all)


def _prep_kernel(small_ref, gq_ref, gkv_ref, wuq_ref, wuqi_ref, wuk_ref,
                 cosq_ref, sinq_ref, cosi_ref, sini_ref, ck_ref, sk_ref, ci_ref, si_ref,
                 qcat_ref, qidx_ref, wt_ref, kcat_ref, kidx_ref, ckvt_ref, *, qscale):
    tq = small_ref.shape[0]
    nblk = tq // QUERY_BLOCK
    cq = small_ref[:, 0:Q_RANK]
    cqn = (cq * lax.rsqrt(jnp.mean(cq * cq, axis=-1, keepdims=True) + EPS)
           * gq_ref[...]).astype(BF16)
    nt = (((1,), (1,)), ((), ()))

    q_t = lax.dot_general(wuq_ref[...], cqn, nt, preferred_element_type=F32)
    cosq = cosq_ref[...]
    sinq = sinq_ref[...]
    half = ROPE_DIM // 2
    zpad = jnp.zeros((KCAT - KV_RANK - ROPE_DIM, tq), F32)
    for h in range(N_HEADS_A):
        base = h * HEAD_DIM
        x1 = q_t[base:base + half]
        x2 = q_t[base + half:base + ROPE_DIM]
        q_rope = jnp.concatenate([x1 * cosq - x2 * sinq, x2 * cosq + x1 * sinq, zpad], axis=0)
        q_lat = jnp.dot(wuk_ref[h], q_t[base:base + HEAD_DIM].astype(BF16),
                        preferred_element_type=F32)
        q_lat = (q_lat * qscale).astype(BF16)
        q_rope = (q_rope * qscale).astype(BF16)
        for blk in range(nblk):
            cols = slice(blk * QUERY_BLOCK, (blk + 1) * QUERY_BLOCK)
            lanes = slice(h * QUERY_BLOCK, (h + 1) * QUERY_BLOCK)
            qcat_ref[blk, 0:KV_RANK, lanes] = q_lat[:, cols]
            qcat_ref[blk, KV_RANK:KCAT, lanes] = q_rope[:, cols]

    qi_t = lax.dot_general(wuqi_ref[...], cqn, nt, preferred_element_type=F32)
    cosi = cosi_ref[...]
    sini = sini_ref[...]
    ihalf = IDX_ROPE // 2
    ipad = jnp.zeros((KIDX_PAD - IDX_DIM, tq), F32)
    for h in range(IDX_HEADS):
        base = h * IDX_DIM
        x1 = qi_t[base:base + ihalf]
        x2 = qi_t[base + ihalf:base + IDX_ROPE]
        qi = jnp.concatenate([x1 * cosi - x2 * sini, x2 * cosi + x1 * sini,
                              qi_t[base + IDX_ROPE:base + IDX_DIM], ipad], axis=0).astype(BF16)
        for blk in range(nblk):
            cols = slice(blk * QUERY_BLOCK, (blk + 1) * QUERY_BLOCK)
            qidx_ref[blk, :, h * QUERY_BLOCK:(h + 1) * QUERY_BLOCK] = qi[:, cols]

    lane = lax.broadcasted_iota(I32, (tq, LANE), 1)

    slab_b = small_ref[:, 896:1024]
    swap_b = jnp.where(lane < ihalf, pltpu.roll(slab_b, LANE - ihalf, 1), pltpu.roll(slab_b, ihalf, 1))
    kidx_ref[...] = (slab_b * ci_ref[...] + swap_b * si_ref[...]).astype(BF16)
    w_t = slab_b.T[IDX_DIM:IDX_DIM + IDX_HEADS] * (IDX_HEADS ** -0.5 * IDX_DIM ** -0.5)
    for blk in range(nblk):
        wt_ref[blk] = w_t[:, blk * QUERY_BLOCK:(blk + 1) * QUERY_BLOCK]

    ckv = small_ref[:, Q_RANK:Q_RANK + KV_RANK]
    cn = ckv * lax.rsqrt(jnp.mean(ckv * ckv, axis=-1, keepdims=True) + EPS) * gkv_ref[...]
    kcat_ref[:, 0:KV_RANK] = cn.astype(BF16)
    ckvt_ref[0] = cn.T.astype(BF16)
    slab_a = small_ref[:, 768:896]
    swap_a = jnp.where(lane < half, pltpu.roll(slab_a, LANE - half, 1), pltpu.roll(slab_a, half, 1))
    kcat_ref[:, KV_RANK:KCAT] = (slab_a * ck_ref[...] + swap_a * sk_ref[...]).astype(BF16)


def _prep_call(small, g_q, g_kv, wuq_t, wuqi_t, wuk_t, tabs, seq):
    rows = small.shape[0]
    tq = PREP_TQ
    nblk = tq // QUERY_BLOCK
    tpb = seq // tq
    nqb = rows // QUERY_BLOCK
    qscale = HEAD_DIM ** -0.5 * LOG2E
    kern = functools.partial(_prep_kernel, qscale=qscale)
    const = lambda t: (0, 0)
    tcol = lambda t: (0, t % tpb)
    trow = lambda t: (t % tpb, 0)
    return pl.pallas_call(
        kern,
        grid=(rows // tq,),
        in_specs=[pl.BlockSpec((tq, SMALL_W), lambda t: (t, 0)),
                  pl.BlockSpec((1, Q_RANK), const),
                  pl.BlockSpec((1, KV_RANK), const),
                  pl.BlockSpec((WIDTH_A, Q_RANK), const),
                  pl.BlockSpec((IDX_HEADS * IDX_DIM, Q_RANK), const),
                  pl.BlockSpec((N_HEADS_A, KV_RANK, HEAD_DIM), lambda t: (0, 0, 0)),
                  pl.BlockSpec((ROPE_DIM // 2, tq), tcol),
                  pl.BlockSpec((ROPE_DIM // 2, tq), tcol),
                  pl.BlockSpec((IDX_ROPE // 2, tq), tcol),
                  pl.BlockSpec((IDX_ROPE // 2, tq), tcol),
                  pl.BlockSpec((tq, LANE), trow),
                  pl.BlockSpec((tq, LANE), trow),
                  pl.BlockSpec((tq, LANE), trow),
                  pl.BlockSpec((tq, LANE), trow)],
        out_specs=[pl.BlockSpec((nblk, KCAT, N_HEADS_A * QUERY_BLOCK), lambda t: (t, 0, 0)),
                   pl.BlockSpec((nblk, KIDX_PAD, IDX_HEADS * QUERY_BLOCK), lambda t: (t, 0, 0)),
                   pl.BlockSpec((nblk, IDX_HEADS, QUERY_BLOCK), lambda t: (t, 0, 0)),
                   pl.BlockSpec((tq, KCAT), lambda t: (t, 0)),
                   pl.BlockSpec((tq, KIDX_PAD), lambda t: (t, 0)),
                   pl.BlockSpec((1, KV_RANK, tq), lambda t: (t, 0, 0))],
        out_shape=[jax.ShapeDtypeStruct((nqb, KCAT, N_HEADS_A * QUERY_BLOCK), BF16),
                   jax.ShapeDtypeStruct((nqb, KIDX_PAD, IDX_HEADS * QUERY_BLOCK), BF16),
                   jax.ShapeDtypeStruct((nqb, IDX_HEADS, QUERY_BLOCK), F32),
                   jax.ShapeDtypeStruct((rows, KCAT), BF16),
                   jax.ShapeDtypeStruct((rows, KIDX_PAD), BF16),
                   jax.ShapeDtypeStruct((rows // tq, KV_RANK, tq), BF16)],
        compiler_params=_cparams(("parallel",)),
        name="dsa_prep",
    )(small, g_q, g_kv, wuq_t, wuqi_t, wuk_t, *tabs)


def _ordered_key(x):
    bits = lax.bitcast_convert_type(x, I32)
    return bits ^ (lax.shift_right_arithmetic(bits, 31) & 0x7FFFFFFF)


def _ordered_key_const(value):
    bits = int(np.float32(value).view(np.int32))
    return bits ^ ((bits >> 31) & 0x7FFFFFFF)


_NEG_KEY = _ordered_key_const(NEG)


def _dsa_kernel(kidx_ref, kcat_ref, ckvt_ref, qcat_ref, qidx_ref, wt_ref, gate_ref, wuv_ref,
                o_ref, keys_ref, m_ref, l_ref, acc_ref, *, k_sel):
    i = pl.program_id(1)
    kc = KEY_CHUNK
    qb = QUERY_BLOCK
    per = kc // qb
    nch = (i + per) // per
    qpos = i * qb + lax.broadcasted_iota(I32, (kc, qb), 1)
    krow = lax.broadcasted_iota(I32, (kc, qb), 0)
    neg_key = _NEG_KEY

    def idx_body(c, carry):
        k0 = pl.multiple_of(c * kc, kc)
        kblk = kidx_ref[pl.ds(k0, kc), :]
        acc = jnp.zeros((kc, qb), F32)
        group = 4
        for g in range(IDX_HEADS // group):
            lg = jnp.dot(kblk, qidx_ref[0, :, g * group * qb:(g + 1) * group * qb],
                         preferred_element_type=F32)
            for hh in range(group):
                h = g * group + hh
                acc = acc + jnp.maximum(lg[:, hh * qb:(hh + 1) * qb], 0.0) * wt_ref[0, h:h + 1, :]
        causal = (k0 + krow) <= qpos
        keys_ref[pl.ds(k0, kc), :] = jnp.where(causal, _ordered_key(acc), neg_key)
        return carry

    lax.fori_loop(0, nch, idx_body, 0)

    def bis_body(s, tu):
        bit = lax.shift_left(jnp.int32(1), 31 - s)
        cand_u = tu | bit
        cand_s = cand_u ^ INT_MIN

        def cnt_body(c, cnt8):
            k = keys_ref[pl.ds(pl.multiple_of(c * kc, kc), kc), :]
            sel = jnp.where(k >= cand_s, 1, 0).astype(I32)
            return cnt8 + jnp.sum(sel.reshape(kc // 8, 8, qb), axis=0)

        cnt8 = lax.fori_loop(0, nch, cnt_body, jnp.zeros((8, qb), I32))
        cnt = jnp.sum(cnt8, axis=0, keepdims=True)
        return jnp.where(cnt >= k_sel, cand_u, tu)

    tu = lax.fori_loop(0, 32, bis_body, jnp.zeros((1, qb), I32))
    thr = jnp.maximum(tu ^ INT_MIN, INT_MIN + 1)

    m_ref[...] = jnp.full(m_ref.shape, NEG, F32)
    l_ref[...] = jnp.zeros(l_ref.shape, F32)
    acc_ref[...] = jnp.zeros(acc_ref.shape, F32)
    pair = 2 * qb

    def att_body(c, carry):
        k0 = pl.multiple_of(c * kc, kc)
        kblk = kcat_ref[pl.ds(k0, kc), :]
        key = keys_ref[pl.ds(k0, kc), :]
        causal = (k0 + krow) <= qpos
        sel = jnp.where(causal, key, INT_MIN) >= thr
        bias = jnp.where(sel, 0.0, NEG).astype(F32)
        bias2 = jnp.concatenate([bias, bias], axis=1)
        v_t = ckvt_ref[c]
        for hp in range(N_HEADS_A // 2):
            cols = slice(hp * pair, (hp + 1) * pair)
            s = jnp.dot(kblk, qcat_ref[0, :, cols], preferred_element_type=F32) + bias2
            m_old = m_ref[:, cols]
            m_new = jnp.maximum(m_old, jnp.max(s, axis=0, keepdims=True))
            alpha = jnp.exp2(m_old - m_new)
            p = jnp.exp2(s - m_new)
            l_ref[:, cols] = alpha * l_ref[:, cols] + jnp.sum(p, axis=0, keepdims=True)
            acc_ref[:, cols] = alpha * acc_ref[:, cols] + jnp.dot(
                v_t, p.astype(BF16), preferred_element_type=F32)
            m_ref[:, cols] = m_new
        return carry

    lax.fori_loop(0, nch, att_body, 0)

    inv_l = 1.0 / l_ref[...]
    outs = []
    for h in range(N_HEADS_A):
        cols = slice(h * qb, (h + 1) * qb)
        o_t = acc_ref[:, cols] * inv_l[:, cols]
        outs.append(jnp.dot(o_t.T.astype(BF16), wuv_ref[h], preferred_element_type=F32))
    o = jnp.concatenate(outs, axis=1)
    g = gate_ref[...].astype(F32)
    o_ref[...] = (o * _silu(g)).astype(BF16)


def _dsa_call(kidx, kcat, ckvt, qcat, qidx, w_t, gates, wuv, batch, seq):
    rows = kidx.shape[0]
    nb = seq // QUERY_BLOCK
    nkc = seq // KEY_CHUNK
    k_sel = min(TOPK_MAX, seq // 4)
    kern = functools.partial(_dsa_kernel, k_sel=k_sel)
    return pl.pallas_call(
        kern,
        grid=(batch, nb),
        in_specs=[pl.BlockSpec((seq, KIDX_PAD), lambda b, i: (b, 0)),
                  pl.BlockSpec((seq, KCAT), lambda b, i: (b, 0)),
                  pl.BlockSpec((nkc, KV_RANK, KEY_CHUNK), lambda b, i: (b, 0, 0)),
                  pl.BlockSpec((1, KCAT, N_HEADS_A * QUERY_BLOCK), lambda b, i: (b * nb + i, 0, 0)),
                  pl.BlockSpec((1, KIDX_PAD, IDX_HEADS * QUERY_BLOCK), lambda b, i: (b * nb + i, 0, 0)),
                  pl.BlockSpec((1, IDX_HEADS, QUERY_BLOCK), lambda b, i: (b * nb + i, 0, 0)),
                  pl.BlockSpec((QUERY_BLOCK, WIDTH_A), lambda b, i: (b * nb + i, 0)),
                  pl.BlockSpec((N_HEADS_A, KV_RANK, HEAD_DIM), lambda b, i: (0, 0, 0))],
        out_specs=pl.BlockSpec((QUERY_BLOCK, WIDTH_A), lambda b, i: (b * nb + i, 0)),
        out_shape=jax.ShapeDtypeStruct((rows, WIDTH_A), BF16),
        scratch_shapes=[pltpu.VMEM((seq, QUERY_BLOCK), I32),
                        pltpu.VMEM((1, N_HEADS_A * QUERY_BLOCK), F32),
                        pltpu.VMEM((1, N_HEADS_A * QUERY_BLOCK), F32),
                        pltpu.VMEM((KV_RANK, N_HEADS_A * QUERY_BLOCK), F32)],
        compiler_params=_cparams(("parallel", "arbitrary")),
        name="dsa_attn",
    )(kidx, kcat, ckvt, qcat, qidx, w_t, gates, wuv)


def _dil_kernel(q_ref, k_ref, v_ref, gate_ref, c_ref, s_ref, o_ref,
                qr_ref, kr_ref, oc0, oc1, oc2, ls0, ls1, ls2, *, qscale, unroll):
    seq = q_ref.shape[0]
    half = ROPE_DIM // 2
    blk = N_BACK
    rc = ROW_CHUNK

    def rope_body(c, carry):
        rows = pl.ds(pl.multiple_of(c * rc, rc), rc)
        lane = lax.broadcasted_iota(I32, (rc, LANE), 1)
        cs = c_ref[rows, :]
        sn = s_ref[rows, :]
        for src, dst, sc in ((q_ref, qr_ref, qscale), (k_ref, kr_ref, 1.0)):
            x = src[rows, :]
            sw = jnp.where(lane < half, pltpu.roll(x, LANE - half, 1), pltpu.roll(x, half, 1))
            dst[rows, :] = (x * cs + sw * sn) * sc
        return carry

    lax.fori_loop(0, seq // rc, rope_body, 0)

    qi = lax.broadcasted_iota(I32, (blk, 2 * blk), 0) + blk
    kj = lax.broadcasted_iota(I32, (blk, 2 * blk), 1)
    rel = qi - kj
    band = (rel >= 0) & (rel <= N_BACK)
    nt = (((1,), (1,)), ((), ()))

    for (window, dil), oc, ls in zip(DILATED_CONFIGS, (oc0, oc1, oc2), (ls0, ls1, ls2)):
        lg = dil.bit_length() - 1

        def rows_at(start, dil=dil):
            if dil == 1:
                return pl.ds(start, blk)
            return pl.ds(start, blk, stride=dil)

        def unit(u, dil=dil, lg=lg, oc=oc, ls=ls, rows_at=rows_at):
            r = u & (dil - 1)
            mb = lax.shift_right_logical(u, lg)
            q0 = r + dil * blk * mb
            kp0 = jnp.maximum(q0 - dil * blk, r)
            qb = qr_ref[rows_at(q0), :].astype(BF16)
            kb = jnp.concatenate([kr_ref[rows_at(kp0), :], kr_ref[rows_at(q0), :]], axis=0).astype(BF16)
            vb = jnp.concatenate([v_ref[rows_at(kp0), :], v_ref[rows_at(q0), :]], axis=0).astype(BF16)
            s = lax.dot_general(qb, kb, nt, preferred_element_type=F32)
            kmin = jnp.where(mb > 0, 0, blk)
            s = jnp.where(band & (kj >= kmin), s, NEG)
            m = jnp.max(s, axis=-1, keepdims=True)
            p = jnp.exp2(s - m)
            l = jnp.sum(p, axis=-1, keepdims=True)
            o = jnp.dot(p.astype(BF16), vb, preferred_element_type=F32) / l
            oc[rows_at(q0), :] = o
            ls[rows_at(q0), :] = jnp.broadcast_to(m + jnp.log2(l), (blk, LANE))

        def loop_body(g, carry, unit=unit):
            for t in range(unroll):
                unit(g * unroll + t)
            return carry

        lax.fori_loop(0, seq // blk // unroll, loop_body, 0)

    def merge_body(c, carry):
        rows = pl.ds(pl.multiple_of(c * rc, rc), rc)
        l0 = ls0[rows, :]
        l1 = ls1[rows, :]
        l2 = ls2[rows, :]
        mx = jnp.maximum(jnp.maximum(l0, l1), l2)
        e0 = jnp.exp2(l0 - mx)
        e1 = jnp.exp2(l1 - mx)
        e2 = jnp.exp2(l2 - mx)
        o = (e0 * oc0[rows, :] + e1 * oc1[rows, :] + e2 * oc2[rows, :]) / (e0 + e1 + e2)
        g = gate_ref[rows, :].astype(F32)
        o_ref[rows, :] = (o * _silu(g)).astype(BF16)
        return carry

    lax.fori_loop(0, seq // rc, merge_body, 0)


def _dil_call(qkv, gates, ctab, stab, batch, seq):
    rows = qkv.shape[0]
    nh = N_HEADS_B
    qscale = HEAD_DIM ** -0.5 * LOG2E
    kern = functools.partial(_dil_kernel, qscale=qscale, unroll=4)
    blk = (seq, HEAD_DIM)
    return pl.pallas_call(
        kern,
        grid=(batch, nh),
        in_specs=[pl.BlockSpec(blk, lambda b, h: (b, h)),
                  pl.BlockSpec(blk, lambda b, h: (b, nh + h)),
                  pl.BlockSpec(blk, lambda b, h: (b, 2 * nh + h)),
                  pl.BlockSpec(blk, lambda b, h: (b, nh + h)),
                  pl.BlockSpec(blk, lambda b, h: (0, 0)),
                  pl.BlockSpec(blk, lambda b, h: (0, 0))],
        out_specs=pl.BlockSpec(blk, lambda b, h: (b, h)),
        out_shape=jax.ShapeDtypeStruct((rows, WIDTH_B), BF16),
        scratch_shapes=[pltpu.VMEM(blk, F32) for _ in range(8)],
        compiler_params=_cparams(("parallel", "arbitrary")),
        name="dilated",
    )(qkv, qkv, qkv, gates, ctab, stab)


def _out_kernel(oa_ref, ob_ref, w_ref, x_ref, mod_ref, g_ref, o_ref):
    y = jnp.dot(oa_ref[...], w_ref[0:WIDTH_A, :], preferred_element_type=F32)
    y = y + jnp.dot(ob_ref[...], w_ref[WIDTH_A:WIDTH_A + WIDTH_B, :], preferred_element_type=F32)
    yn = y * lax.rsqrt(jnp.mean(y * y, axis=-1, keepdims=True) + EPS) * g_ref[...]
    o_ref[...] = x_ref[...] + mod_ref[0, 2:3, :] * yn


def _out_call(o_a, o_b, w_out, x2, mod3, g_post, seq):
    rows, d = x2.shape
    tm = OUT_TM
    tpb = seq // tm
    return pl.pallas_call(
        _out_kernel,
        grid=(rows // tm,),
        in_specs=[pl.BlockSpec((tm, WIDTH_A), lambda i: (i, 0)),
                  pl.BlockSpec((tm, WIDTH_B), lambda i: (i, 0)),
                  pl.BlockSpec((WIDTH_A + WIDTH_B, d), lambda i: (0, 0)),
                  pl.BlockSpec((tm, d), lambda i: (i, 0)),
                  pl.BlockSpec((1, 3, d), lambda i: (i // tpb, 0, 0)),
                  pl.BlockSpec((1, d), lambda i: (0, 0))],
        out_specs=pl.BlockSpec((tm, d), lambda i: (i, 0)),
        out_shape=jax.ShapeDtypeStruct((rows, d), F32),
        compiler_params=_cparams(("parallel",)),
        name="out_proj",
    )(o_a, o_b, w_out, x2, mod3, g_post)


def _rope_tables(seq):
    pos = jnp.arange(seq, dtype=F32)

    def cs(n_rot):
        inv = ROPE_THETA ** (-jnp.arange(0, n_rot, 2, dtype=F32) / n_rot)
        ang = pos[:, None] * inv[None, :]
        return jnp.cos(ang), jnp.sin(ang)

    c32, s32 = cs(ROPE_DIM)
    c16, s16 = cs(IDX_ROPE)
    ck = jnp.concatenate([c32, c32, jnp.ones((seq, LANE - ROPE_DIM), F32)], axis=1)
    sk = jnp.concatenate([-s32, s32, jnp.zeros((seq, LANE - ROPE_DIM), F32)], axis=1)
    ci = jnp.concatenate([c16, c16, jnp.ones((seq, LANE - IDX_ROPE), F32)], axis=1)
    si = jnp.concatenate([-s16, s16, jnp.zeros((seq, LANE - IDX_ROPE), F32)], axis=1)
    return (c32.T, s32.T, c16.T, s16.T, ck, sk, ci, si)


def _layer(x, c, w_ada, b_ada, g_pre, g_post, w_in, g_q, g_kv, w_uq, w_uq_idx, w_uk, w_uv, w_out):
    batch, seq, d = x.shape
    assert seq % PROJ_TM == 0 or seq < PROJ_TM
    assert seq % KEY_CHUNK == 0 and seq // 16 >= N_BACK
    assert all(w // dl == N_BACK for w, dl in DILATED_CONFIGS)
    rows = batch * seq
    x2 = x.reshape(rows, d)

    o_cq, o_ckv, o_kr, o_ki, o_wi = 0, 512, 768, 800, 864
    o_ga, o_q, o_k, o_v, o_gb = 880, 1904, 2928, 3952, 4976
    zeros = lambda n: jnp.zeros((d, n), w_in.dtype)
    w_all = jnp.concatenate([
        w_in[:, o_cq:o_ki], zeros(896 - 800), w_in[:, o_ki:o_ga], zeros(SMALL_W - 976),
        w_in[:, o_q:o_gb], w_in[:, o_ga:o_q], w_in[:, o_gb:o_gb + WIDTH_B]], axis=1).astype(BF16)
    assert w_all.shape[1] == IN_PAD

    c_pad = jnp.zeros((8, d), F32).at[:batch].set(c)
    mod = _mod_call(c_pad, w_ada, b_ada.reshape(1, -1))[:batch]
    mod3 = mod.reshape(batch, 3, d)

    small, qkv, gates = _proj_call(x2, mod3, g_pre.reshape(1, d), w_all, seq)

    tabs = _rope_tables(seq)
    wuk_t = jnp.transpose(jnp.pad(w_uk, ((0, 0), (ROPE_DIM, 0), (0, 0))), (0, 2, 1)).astype(BF16)
    qcat, qidx, w_t, kcat, kidx, ckvt = _prep_call(
        small, g_q.reshape(1, -1), g_kv.reshape(1, -1), w_uq.T.astype(BF16),
        w_uq_idx.T.astype(BF16), wuk_t, tabs, seq)

    o_a = _dsa_call(kidx, kcat, ckvt, qcat, qidx, w_t, gates, w_uv.astype(BF16), batch, seq)
    o_b = _dil_call(qkv, gates, tabs[4], tabs[5], batch, seq)
    out = _out_call(o_a, o_b, w_out.astype(BF16), x2, mod3, g_post.reshape(1, d), seq)
    return out.reshape(batch, seq, d)


def kernel(x, c, w_ada, b_ada, g_pre, g_post, w_in, g_q, g_kv, w_uq, w_uq_idx, w_uk, w_uv, w_out):
    for layer in range(w_ada.shape[0]):
        x = _layer(x, c, w_ada[layer], b_ada[layer], g_pre[layer], g_post[layer], w_in[layer],
                   g_q[layer], g_kv[layer], w_uq[layer], w_uq_idx[layer], w_uk[layer],
                   w_uv[layer], w_out[layer])
    return x
```

```python
import functools
import math

import numpy as np
import jax
import jax.numpy as jnp
from jax import lax
from jax.experimental import pallas as pl
from jax.experimental.pallas import tpu as pltpu

F32 = jnp.float32
BF16 = jnp.bfloat16
I32 = jnp.int32

HEAD_DIM = 128
ROPE_DIM = HEAD_DIM // 4
ROPE_THETA = 500000.0
EPS = 1e-6
NEG = -1e30
N_HEADS_A = 8
WIDTH_A = N_HEADS_A * HEAD_DIM
Q_RANK = 512
KV_RANK = 256
IDX_HEADS = 16
IDX_DIM = 64
IDX_ROPE = IDX_DIM // 4
TOPK_MAX = 256
QUERY_BLOCK = 128
N_HEADS_B = 8
WIDTH_B = N_HEADS_B * HEAD_DIM
DILATED_CONFIGS = ((128, 1), (512, 4), (2048, 16))
N_BACK = 128
SMALL_W = 1024
IN_PAD = SMALL_W + 5 * 1024
KCAT = 384
KIDX_PAD = 128
INT_MIN = -(2 ** 31)
LOG2E = 1.4426950408889634

LANE = 128
VMEM_LIMIT = 56 * 1024 * 1024
PROJ_TM = 1024
PROJ_TN = 512
PREP_TQ = 512
KEY_CHUNK = 512
OUT_TM = 512
ROW_CHUNK = 512
NORM_ROWS = 128


def _cparams(sem):
    return pltpu.CompilerParams(dimension_semantics=sem, vmem_limit_bytes=VMEM_LIMIT)


def _silu(g):
    return g * jax.nn.sigmoid(g)


def _mod_kernel(c_ref, w_ref, b_ref, o_ref):
    c = c_ref[...]
    o_ref[...] = jnp.dot(_silu(c), w_ref[...], preferred_element_type=F32,
                         precision=lax.Precision.HIGHEST) + b_ref[...]


def _mod_call(c_pad, w_ada, b_ada):
    rows, d = c_pad.shape
    n = w_ada.shape[1]
    tn = 1024
    return pl.pallas_call(
        _mod_kernel,
        grid=(n // tn,),
        in_specs=[pl.BlockSpec((rows, d), lambda j: (0, 0)),
                  pl.BlockSpec((d, tn), lambda j: (0, j)),
                  pl.BlockSpec((1, tn), lambda j: (0, j))],
        out_specs=pl.BlockSpec((rows, tn), lambda j: (0, j)),
        out_shape=jax.ShapeDtypeStruct((rows, n), F32),
        compiler_params=_cparams(("arbitrary",)),
        name="mod",
    )(c_pad, w_ada, b_ada)


def _proj_kernel(x_ref, mod_ref, g_ref, w_ref, small_ref, qkv_ref, gates_ref, h_ref, *,
                 n_small, n_qkv):
    j = pl.program_id(1)

    @pl.when(j == 0)
    def _():
        shift = mod_ref[0, 0:1, :]
        scale1 = 1.0 + mod_ref[0, 1:2, :]
        g = g_ref[...]

        def body(r, carry):
            rows = pl.ds(pl.multiple_of(r * NORM_ROWS, NORM_ROWS), NORM_ROWS)
            x = x_ref[rows, :]
            ms = jnp.mean(x * x, axis=-1, keepdims=True)
            y = x * lax.rsqrt(ms + EPS) * g
            h_ref[rows, :] = (y * scale1 + shift).astype(BF16)
            return carry

        lax.fori_loop(0, x_ref.shape[0] // NORM_ROWS, body, 0)

    acc = jnp.dot(h_ref[...], w_ref[...], preferred_element_type=F32)

    @pl.when(j < n_small)
    def _():
        small_ref[...] = acc

    @pl.when((j >= n_small) & (j < n_small + n_qkv))
    def _():
        qkv_ref[...] = acc

    @pl.when(j >= n_small + n_qkv)
    def _():
        gates_ref[...] = acc.astype(BF16)


def _proj_call(x2, mod3, g_pre, w_all, seq):
    rows, d = x2.shape
    tm = min(PROJ_TM, seq)
    tn = PROJ_TN
    n_small = SMALL_W // tn
    n_qkv = 3 * WIDTH_B // tn
    n_gate = (WIDTH_A + WIDTH_B) // tn
    tiles_per_batch = seq // tm
    kern = functools.partial(_proj_kernel, n_small=n_small, n_qkv=n_qkv)
    return pl.pallas_call(
        kern,
        grid=(rows // tm, n_small + n_qkv + n_gate),
        in_specs=[pl.BlockSpec((tm, d), lambda i, j: (i, 0)),
                  pl.BlockSpec((1, 3, d), lambda i, j: (i // tiles_per_batch, 0, 0)),
                  pl.BlockSpec((1, d), lambda i, j: (0, 0)),
                  pl.BlockSpec((d, tn), lambda i, j: (0, j))],
        out_specs=[pl.BlockSpec((tm, tn), lambda i, j: (i, jnp.minimum(j, n_small - 1))),
                   pl.BlockSpec((tm, tn), lambda i, j: (i, jnp.clip(j - n_small, 0, n_qkv - 1))),
                   pl.BlockSpec((tm, tn),
                                lambda i, j: (i, jnp.clip(j - n_small - n_qkv, 0, n_gate - 1)))],
        out_shape=[jax.ShapeDtypeStruct((rows, SMALL_W), F32),
                   jax.ShapeDtypeStruct((rows, 3 * WIDTH_B), F32),
                   jax.ShapeDtypeStruct((rows, WIDTH_A + WIDTH_B), BF16)],
        scratch_shapes=[pltpu.VMEM((tm, d), BF16)],
        compiler_params=_cparams(("parallel", "arbitrary")),
        name="in_proj",
    )(x2, mod3, g_pre, w_all)


def _prep_kernel(small_ref, gq_ref, gkv_ref, wuq_ref, wuqi_ref, wuk_ref,
                 cosq_ref, sinq_ref, cosi_ref, sini_ref, ck_ref, sk_ref, ci_ref, si_ref,
                 qcat_ref, qidx_ref, wt_ref, kcat_ref, kidx_ref, ckvt_ref, *, qscale):
    tq = small_ref.shape[0]
    nblk = tq // QUERY_BLOCK
    cq = small_ref[:, 0:Q_RANK]
    cqn = (cq * lax.rsqrt(jnp.mean(cq * cq, axis=-1, keepdims=True) + EPS)
           * gq_ref[...]).astype(BF16)
    nt = (((1,), (1,)), ((), ()))

    q_t = lax.dot_general(wuq_ref[...], cqn, nt, preferred_element_type=F32)
    cosq = cosq_ref[...]
    sinq = sinq_ref[...]
    half = ROPE_DIM // 2
    zpad = jnp.zeros((KCAT - KV_RANK - ROPE_DIM, tq), F32)
    for h in range(N_HEADS_A):
        base = h * HEAD_DIM
        x1 = q_t[base:base + half]
        x2 = q_t[base + half:base + ROPE_DIM]
        q_rope = jnp.concatenate([x1 * cosq - x2 * sinq, x2 * cosq + x1 * sinq, zpad], axis=0)
        q_lat = jnp.dot(wuk_ref[h], q_t[base:base + HEAD_DIM].astype(BF16),
                        preferred_element_type=F32)
        q_lat = (q_lat * qscale).astype(BF16)
        q_rope = (q_rope * qscale).astype(BF16)
        for blk in range(nblk):
            cols = slice(blk * QUERY_BLOCK, (blk + 1) * QUERY_BLOCK)
            lanes = slice(h * QUERY_BLOCK, (h + 1) * QUERY_BLOCK)
            qcat_ref[blk, 0:KV_RANK, lanes] = q_lat[:, cols]
            qcat_ref[blk, KV_RANK:KCAT, lanes] = q_rope[:, cols]

    qi_t = lax.dot_general(wuqi_ref[...], cqn, nt, preferred_element_type=F32)
    cosi = cosi_ref[...]
    sini = sini_ref[...]
    ihalf = IDX_ROPE // 2
    ipad = jnp.zeros((KIDX_PAD - IDX_DIM, tq), F32)
    for h in range(IDX_HEADS):
        base = h * IDX_DIM
        x1 = qi_t[base:base + ihalf]
        x2 = qi_t[base + ihalf:base + IDX_ROPE]
        qi = jnp.concatenate([x1 * cosi - x2 * sini, x2 * cosi + x1 * sini,
                              qi_t[base + IDX_ROPE:base + IDX_DIM], ipad], axis=0).astype(BF16)
        for blk in range(nblk):
            cols = slice(blk * QUERY_BLOCK, (blk + 1) * QUERY_BLOCK)
            qidx_ref[blk, :, h * QUERY_BLOCK:(h + 1) * QUERY_BLOCK] = qi[:, cols]

    lane = lax.broadcasted_iota(I32, (tq, LANE), 1)

    slab_b = small_ref[:, 896:1024]
    swap_b = jnp.where(lane < ihalf, pltpu.roll(slab_b, LANE - ihalf, 1), pltpu.roll(slab_b, ihalf, 1))
    kidx_ref[...] = (slab_b * ci_ref[...] + swap_b * si_ref[...]).astype(BF16)
    w_t = slab_b.T[IDX_DIM:IDX_DIM + IDX_HEADS] * (IDX_HEADS ** -0.5 * IDX_DIM ** -0.5)
    for blk in range(nblk):
        wt_ref[blk] = w_t[:, blk * QUERY_BLOCK:(blk + 1) * QUERY_BLOCK]

    ckv = small_ref[:, Q_RANK:Q_RANK + KV_RANK]
    cn = ckv * lax.rsqrt(jnp.mean(ckv * ckv, axis=-1, keepdims=True) + EPS) * gkv_ref[...]
    kcat_ref[:, 0:KV_RANK] = cn.astype(BF16)
    ckvt_ref[0] = cn.T.astype(BF16)
    slab_a = small_ref[:, 768:896]
    swap_a = jnp.where(lane < half, pltpu.roll(slab_a, LANE - half, 1), pltpu.roll(slab_a, half, 1))
    kcat_ref[:, KV_RANK:KCAT] = (slab_a * ck_ref[...] + swap_a * sk_ref[...]).astype(BF16)


def _prep_call(small, g_q, g_kv, wuq_t, wuqi_t, wuk_t, tabs, seq):
    rows = small.shape[0]
    tq = PREP_TQ
    nblk = tq // QUERY_BLOCK
    tpb = seq // tq
    nqb = rows // QUERY_BLOCK
    qscale = HEAD_DIM ** -0.5 * LOG2E
    kern = functools.partial(_prep_kernel, qscale=qscale)
    const = lambda t: (0, 0)
    tcol = lambda t: (0, t % tpb)
    trow = lambda t: (t % tpb, 0)
    return pl.pallas_call(
        kern,
        grid=(rows // tq,),
        in_specs=[pl.BlockSpec((tq, SMALL_W), lambda t: (t, 0)),
                  pl.BlockSpec((1, Q_RANK), const),
                  pl.BlockSpec((1, KV_RANK), const),
                  pl.BlockSpec((WIDTH_A, Q_RANK), const),
                  pl.BlockSpec((IDX_HEADS * IDX_DIM, Q_RANK), const),
                  pl.BlockSpec((N_HEADS_A, KV_RANK, HEAD_DIM), lambda t: (0, 0, 0)),
                  pl.BlockSpec((ROPE_DIM // 2, tq), tcol),
                  pl.BlockSpec((ROPE_DIM // 2, tq), tcol),
                  pl.BlockSpec((IDX_ROPE // 2, tq), tcol),
                  pl.BlockSpec((IDX_ROPE // 2, tq), tcol),
                  pl.BlockSpec((tq, LANE), trow),
                  pl.BlockSpec((tq, LANE), trow),
                  pl.BlockSpec((tq, LANE), trow),
                  pl.BlockSpec((tq, LANE), trow)],
        out_specs=[pl.BlockSpec((nblk, KCAT, N_HEADS_A * QUERY_BLOCK), lambda t: (t, 0, 0)),
                   pl.BlockSpec((nblk, KIDX_PAD, IDX_HEADS * QUERY_BLOCK), lambda t: (t, 0, 0)),
                   pl.BlockSpec((nblk, IDX_HEADS, QUERY_BLOCK), lambda t: (t, 0, 0)),
                   pl.BlockSpec((tq, KCAT), lambda t: (t, 0)),
                   pl.BlockSpec((tq, KIDX_PAD), lambda t: (t, 0)),
                   pl.BlockSpec((1, KV_RANK, tq), lambda t: (t, 0, 0))],
        out_shape=[jax.ShapeDtypeStruct((nqb, KCAT, N_HEADS_A * QUERY_BLOCK), BF16),
                   jax.ShapeDtypeStruct((nqb, KIDX_PAD, IDX_HEADS * QUERY_BLOCK), BF16),
                   jax.ShapeDtypeStruct((nqb, IDX_HEADS, QUERY_BLOCK), F32),
                   jax.ShapeDtypeStruct((rows, KCAT), BF16),
                   jax.ShapeDtypeStruct((rows, KIDX_PAD), BF16),
                   jax.ShapeDtypeStruct((rows // tq, KV_RANK, tq), BF16)],
        compiler_params=_cparams(("parallel",)),
        name="dsa_prep",
    )(small, g_q, g_kv, wuq_t, wuqi_t, wuk_t, *tabs)


def _ordered_key(x):
    bits = lax.bitcast_convert_type(x, I32)
    return bits ^ (lax.shift_right_arithmetic(bits, 31) & 0x7FFFFFFF)


def _ordered_key_const(value):
    bits = int(np.float32(value).view(np.int32))
    return bits ^ ((bits >> 31) & 0x7FFFFFFF)


_NEG_KEY = _ordered_key_const(NEG)


def _dsa_kernel(kidx_ref, kcat_ref, ckvt_ref, qcat_ref, qidx_ref, wt_ref, gate_ref, wuv_ref,
                o_ref, keys_ref, m_ref, l_ref, acc_ref, *, k_sel):
    i = pl.program_id(1)
    kc = KEY_CHUNK
    qb = QUERY_BLOCK
    per = kc // qb
    nch = (i + per) // per
    qpos = i * qb + lax.broadcasted_iota(I32, (kc, qb), 1)
    krow = lax.broadcasted_iota(I32, (kc, qb), 0)
    neg_key = _NEG_KEY

    def idx_body(c, carry):
        k0 = pl.multiple_of(c * kc, kc)
        kblk = kidx_ref[pl.ds(k0, kc), :]
        acc = jnp.zeros((kc, qb), F32)
        group = 4
        for g in range(IDX_HEADS // group):
            lg = jnp.dot(kblk, qidx_ref[0, :, g * group * qb:(g + 1) * group * qb],
                         preferred_element_type=F32)
            for hh in range(group):
                h = g * group + hh
                acc = acc + jnp.maximum(lg[:, hh * qb:(hh + 1) * qb], 0.0) * wt_ref[0, h:h + 1, :]
        causal = (k0 + krow) <= qpos
        keys_ref[pl.ds(k0, kc), :] = jnp.where(causal, _ordered_key(acc), neg_key)
        return carry

    lax.fori_loop(0, nch, idx_body, 0)

    def bis_body(s, tu):
        bit = lax.shift_left(jnp.int32(1), 31 - s)
        cand_u = tu | bit
        cand_s = cand_u ^ INT_MIN

        def cnt_body(c, cnt8):
            k = keys_ref[pl.ds(pl.multiple_of(c * kc, kc), kc), :]
            sel = jnp.where(k >= cand_s, 1, 0).astype(I32)
            return cnt8 + jnp.sum(sel.reshape(kc // 8, 8, qb), axis=0)

        cnt8 = lax.fori_loop(0, nch, cnt_body, jnp.zeros((8, qb), I32))
        cnt = jnp.sum(cnt8, axis=0, keepdims=True)
        return jnp.where(cnt >= k_sel, cand_u, tu)

    tu = lax.fori_loop(0, 32, bis_body, jnp.zeros((1, qb), I32))
    thr = jnp.maximum(tu ^ INT_MIN, INT_MIN + 1)

    m_ref[...] = jnp.full(m_ref.shape, NEG, F32)
    l_ref[...] = jnp.zeros(l_ref.shape, F32)
    acc_ref[...] = jnp.zeros(acc_ref.shape, F32)
    pair = 2 * qb

    def att_step(c):
        k0 = pl.multiple_of(c * kc, kc)
        kblk = kcat_ref[pl.ds(k0, kc), :]
        key = keys_ref[pl.ds(k0, kc), :]
        causal = (k0 + krow) <= qpos
        sel = jnp.where(causal, key, INT_MIN) >= thr
        bias = jnp.where(sel, 0.0, NEG).astype(F32)
        bias_all = jnp.concatenate([bias] * N_HEADS_A, axis=1)
        v_t = ckvt_ref[c]
        s = jnp.dot(kblk, qcat_ref[0], preferred_element_type=F32) + bias_all
        m_old = m_ref[...]
        m_new = jnp.maximum(m_old, jnp.max(s, axis=0, keepdims=True))
        alpha = jnp.exp2(m_old - m_new)
        p = jnp.exp2(s - m_new)
        l_ref[...] = alpha * l_ref[...] + jnp.sum(p, axis=0, keepdims=True)
        acc_ref[...] = alpha * acc_ref[...] + jnp.dot(v_t, p.astype(BF16), preferred_element_type=F32)
        m_ref[...] = m_new

    def att_body(c2, carry):
        att_step(2 * c2)
        att_step(2 * c2 + 1)
        return carry

    lax.fori_loop(0, nch // 2, att_body, 0)

    @pl.when(nch % 2 == 1)
    def _():
        att_step(nch - 1)

    inv_l = 1.0 / l_ref[...]
    outs = []
    for h in range(N_HEADS_A):
        cols = slice(h * qb, (h + 1) * qb)
        o_t = acc_ref[:, cols] * inv_l[:, cols]
        outs.append(jnp.dot(o_t.T.astype(BF16), wuv_ref[h], preferred_element_type=F32))
    o = jnp.concatenate(outs, axis=1)
    g = gate_ref[...].astype(F32)
    o_ref[...] = (o * _silu(g)).astype(BF16)


def _dsa_call(kidx, kcat, ckvt, qcat, qidx, w_t, gates, wuv, batch, seq):
    rows = kidx.shape[0]
    nb = seq // QUERY_BLOCK
    nkc = seq // KEY_CHUNK
    k_sel = min(TOPK_MAX, seq // 4)
    kern = functools.partial(_dsa_kernel, k_sel=k_sel)
    return pl.pallas_call(
        kern,
        grid=(batch, nb),
        in_specs=[pl.BlockSpec((seq, KIDX_PAD), lambda b, i: (b, 0)),
                  pl.BlockSpec((seq, KCAT), lambda b, i: (b, 0)),
                  pl.BlockSpec((nkc, KV_RANK, KEY_CHUNK), lambda b, i: (b, 0, 0)),
                  pl.BlockSpec((1, KCAT, N_HEADS_A * QUERY_BLOCK), lambda b, i: (b * nb + i, 0, 0)),
                  pl.BlockSpec((1, KIDX_PAD, IDX_HEADS * QUERY_BLOCK), lambda b, i: (b * nb + i, 0, 0)),
                  pl.BlockSpec((1, IDX_HEADS, QUERY_BLOCK), lambda b, i: (b * nb + i, 0, 0)),
                  pl.BlockSpec((QUERY_BLOCK, WIDTH_A), lambda b, i: (b * nb + i, 0)),
                  pl.BlockSpec((N_HEADS_A, KV_RANK, HEAD_DIM), lambda b, i: (0, 0, 0))],
        out_specs=pl.BlockSpec((QUERY_BLOCK, WIDTH_A), lambda b, i: (b * nb + i, 0)),
        out_shape=jax.ShapeDtypeStruct((rows, WIDTH_A), BF16),
        scratch_shapes=[pltpu.VMEM((seq, QUERY_BLOCK), I32),
                        pltpu.VMEM((1, N_HEADS_A * QUERY_BLOCK), F32),
                        pltpu.VMEM((1, N_HEADS_A * QUERY_BLOCK), F32),
                        pltpu.VMEM((KV_RANK, N_HEADS_A * QUERY_BLOCK), F32)],
        compiler_params=_cparams(("parallel", "arbitrary")),
        name="dsa_attn",
    )(kidx, kcat, ckvt, qcat, qidx, w_t, gates, wuv)


def _dil_kernel(q_ref, k_ref, v_ref, gate_ref, c_ref, s_ref, o_ref,
                qr_ref, kr_ref, oc0, oc1, oc2, ls0, ls1, ls2, *, qscale, unroll):
    seq = q_ref.shape[0]
    half = ROPE_DIM // 2
    blk = N_BACK
    rc = ROW_CHUNK

    def rope_body(c, carry):
        rows = pl.ds(pl.multiple_of(c * rc, rc), rc)
        lane = lax.broadcasted_iota(I32, (rc, LANE), 1)
        cs = c_ref[rows, :]
        sn = s_ref[rows, :]
        for src, dst, sc in ((q_ref, qr_ref, qscale), (k_ref, kr_ref, 1.0)):
            x = src[rows, :]
            sw = jnp.where(lane < half, pltpu.roll(x, LANE - half, 1), pltpu.roll(x, half, 1))
            dst[rows, :] = (x * cs + sw * sn) * sc
        return carry

    lax.fori_loop(0, seq // rc, rope_body, 0)

    qi = lax.broadcasted_iota(I32, (blk, 2 * blk), 0) + blk
    kj = lax.broadcasted_iota(I32, (blk, 2 * blk), 1)
    rel = qi - kj
    band = (rel >= 0) & (rel <= N_BACK)
    nt = (((1,), (1,)), ((), ()))

    for (window, dil), oc, ls in zip(DILATED_CONFIGS, (oc0, oc1, oc2), (ls0, ls1, ls2)):
        lg = dil.bit_length() - 1

        def rows_at(start, dil=dil):
            if dil == 1:
                return pl.ds(start, blk)
            return pl.ds(start, blk, stride=dil)

        def unit(u, dil=dil, lg=lg, oc=oc, ls=ls, rows_at=rows_at):
            r = u & (dil - 1)
            mb = lax.shift_right_logical(u, lg)
            q0 = r + dil * blk * mb
            kp0 = jnp.maximum(q0 - dil * blk, r)
            qb = qr_ref[rows_at(q0), :].astype(BF16)
            kb = jnp.concatenate([kr_ref[rows_at(kp0), :], kr_ref[rows_at(q0), :]], axis=0).astype(BF16)
            vb = jnp.concatenate([v_ref[rows_at(kp0), :], v_ref[rows_at(q0), :]], axis=0).astype(BF16)
            s = lax.dot_general(qb, kb, nt, preferred_element_type=F32)
            kmin = jnp.where(mb > 0, 0, blk)
            s = jnp.where(band & (kj >= kmin), s, NEG)
            m = jnp.max(s, axis=-1, keepdims=True)
            p = jnp.exp2(s - m)
            l = jnp.sum(p, axis=-1, keepdims=True)
            o = jnp.dot(p.astype(BF16), vb, preferred_element_type=F32) / l
            oc[rows_at(q0), :] = o
            ls[rows_at(q0), :] = jnp.broadcast_to(m + jnp.log2(l), (blk, LANE))

        def loop_body(g, carry, unit=unit):
            for t in range(unroll):
                unit(g * unroll + t)
            return carry

        lax.fori_loop(0, seq // blk // unroll, loop_body, 0)

    def merge_body(c, carry):
        rows = pl.ds(pl.multiple_of(c * rc, rc), rc)
        l0 = ls0[rows, :]
        l1 = ls1[rows, :]
        l2 = ls2[rows, :]
        mx = jnp.maximum(jnp.maximum(l0, l1), l2)
        e0 = jnp.exp2(l0 - mx)
        e1 = jnp.exp2(l1 - mx)
        e2 = jnp.exp2(l2 - mx)
        o = (e0 * oc0[rows, :] + e1 * oc1[rows, :] + e2 * oc2[rows, :]) / (e0 + e1 + e2)
        g = gate_ref[rows, :].astype(F32)
        o_ref[rows, :] = (o * _silu(g)).astype(BF16)
        return carry

    lax.fori_loop(0, seq // rc, merge_body, 0)


def _dil_call(qkv, gates, ctab, stab, batch, seq):
    rows = qkv.shape[0]
    nh = N_HEADS_B
    qscale = HEAD_DIM ** -0.5 * LOG2E
    kern = functools.partial(_dil_kernel, qscale=qscale, unroll=16)
    blk = (seq, HEAD_DIM)
    return pl.pallas_call(
        kern,
        grid=(batch, nh),
        in_specs=[pl.BlockSpec(blk, lambda b, h: (b, h)),
                  pl.BlockSpec(blk, lambda b, h: (b, nh + h)),
                  pl.BlockSpec(blk, lambda b, h: (b, 2 * nh + h)),
                  pl.BlockSpec(blk, lambda b, h: (b, nh + h)),
                  pl.BlockSpec(blk, lambda b, h: (0, 0)),
                  pl.BlockSpec(blk, lambda b, h: (0, 0))],
        out_specs=pl.BlockSpec(blk, lambda b, h: (b, h)),
        out_shape=jax.ShapeDtypeStruct((rows, WIDTH_B), BF16),
        scratch_shapes=[pltpu.VMEM(blk, F32) for _ in range(8)],
        compiler_params=_cparams(("parallel", "arbitrary")),
        name="dilated",
    )(qkv, qkv, qkv, gates, ctab, stab)


def _out_kernel(oa_ref, ob_ref, w_ref, x_ref, mod_ref, g_ref, o_ref):
    y = jnp.dot(oa_ref[...], w_ref[0:WIDTH_A, :], preferred_element_type=F32)
    y = y + jnp.dot(ob_ref[...], w_ref[WIDTH_A:WIDTH_A + WIDTH_B, :], preferred_element_type=F32)
    yn = y * lax.rsqrt(jnp.mean(y * y, axis=-1, keepdims=True) + EPS) * g_ref[...]
    o_ref[...] = x_ref[...] + mod_ref[0, 2:3, :] * yn


def _out_call(o_a, o_b, w_out, x2, mod3, g_post, seq):
    rows, d = x2.shape
    tm = OUT_TM
    tpb = seq // tm
    return pl.pallas_call(
        _out_kernel,
        grid=(rows // tm,),
        in_specs=[pl.BlockSpec((tm, WIDTH_A), lambda i: (i, 0)),
                  pl.BlockSpec((tm, WIDTH_B), lambda i: (i, 0)),
                  pl.BlockSpec((WIDTH_A + WIDTH_B, d), lambda i: (0, 0)),
                  pl.BlockSpec((tm, d), lambda i: (i, 0)),
                  pl.BlockSpec((1, 3, d), lambda i: (i // tpb, 0, 0)),
                  pl.BlockSpec((1, d), lambda i: (0, 0))],
        out_specs=pl.BlockSpec((tm, d), lambda i: (i, 0)),
        out_shape=jax.ShapeDtypeStruct((rows, d), F32),
        compiler_params=_cparams(("parallel",)),
        name="out_proj",
    )(o_a, o_b, w_out, x2, mod3, g_post)


def _rope_tables(seq):
    pos = jnp.arange(seq, dtype=F32)

    def cs(n_rot):
        inv = ROPE_THETA ** (-jnp.arange(0, n_rot, 2, dtype=F32) / n_rot)
        ang = pos[:, None] * inv[None, :]
        return jnp.cos(ang), jnp.sin(ang)

    c32, s32 = cs(ROPE_DIM)
    c16, s16 = cs(IDX_ROPE)
    ck = jnp.concatenate([c32, c32, jnp.ones((seq, LANE - ROPE_DIM), F32)], axis=1)
    sk = jnp.concatenate([-s32, s32, jnp.zeros((seq, LANE - ROPE_DIM), F32)], axis=1)
    ci = jnp.concatenate([c16, c16, jnp.ones((seq, LANE - IDX_ROPE), F32)], axis=1)
    si = jnp.concatenate([-s16, s16, jnp.zeros((seq, LANE - IDX_ROPE), F32)], axis=1)
    return (c32.T, s32.T, c16.T, s16.T, ck, sk, ci, si)


def _layer(x, c, w_ada, b_ada, g_pre, g_post, w_in, g_q, g_kv, w_uq, w_uq_idx, w_uk, w_uv, w_out):
    batch, seq, d = x.shape
    assert seq % PROJ_TM == 0 or seq < PROJ_TM
    assert seq % KEY_CHUNK == 0 and seq // 16 >= N_BACK
    assert all(w // dl == N_BACK for w, dl in DILATED_CONFIGS)
    rows = batch * seq
    x2 = x.reshape(rows, d)

    o_cq, o_ckv, o_kr, o_ki, o_wi = 0, 512, 768, 800, 864
    o_ga, o_q, o_k, o_v, o_gb = 880, 1904, 2928, 3952, 4976
    zeros = lambda n: jnp.zeros((d, n), w_in.dtype)
    w_all = jnp.concatenate([
        w_in[:, o_cq:o_ki], zeros(896 - 800), w_in[:, o_ki:o_ga], zeros(SMALL_W - 976),
        w_in[:, o_q:o_gb], w_in[:, o_ga:o_q], w_in[:, o_gb:o_gb + WIDTH_B]], axis=1).astype(BF16)
    assert w_all.shape[1] == IN_PAD

    c_pad = jnp.zeros((8, d), F32).at[:batch].set(c)
    mod = _mod_call(c_pad, w_ada, b_ada.reshape(1, -1))[:batch]
    mod3 = mod.reshape(batch, 3, d)

    small, qkv, gates = _proj_call(x2, mod3, g_pre.reshape(1, d), w_all, seq)

    tabs = _rope_tables(seq)
    wuk_t = jnp.transpose(jnp.pad(w_uk, ((0, 0), (ROPE_DIM, 0), (0, 0))), (0, 2, 1)).astype(BF16)
    qcat, qidx, w_t, kcat, kidx, ckvt = _prep_call(
        small, g_q.reshape(1, -1), g_kv.reshape(1, -1), w_uq.T.astype(BF16),
        w_uq_idx.T.astype(BF16), wuk_t, tabs, seq)

    o_a = _dsa_call(kidx, kcat, ckvt, qcat, qidx, w_t, gates, w_uv.astype(BF16), batch, seq)
    o_b = _dil_call(qkv, gates, tabs[4], tabs[5], batch, seq)
    out = _out_call(o_a, o_b, w_out.astype(BF16), x2, mod3, g_post.reshape(1, d), seq)
    return out.reshape(batch, seq, d)


def kernel(x, c, w_ada, b_ada, g_pre, g_post, w_in, g_q, g_kv, w_uq, w_uq_idx, w_uk, w_uv, w_out):
    for layer in range(w_ada.shape[0]):
        x = _layer(x, c, w_ada[layer], b_ada[layer], g_pre[layer], g_post[layer], w_in[layer],
                   g_q[layer], g_kv[layer], w_uq[layer], w_uq_idx[layer], w_uk[layer],
                   w_uv[layer], w_out[layer])
    return x
```

```python
import functools
import math

import numpy as np
import jax
import jax.numpy as jnp
from jax import lax
from jax.experimental import pallas as pl
from jax.experimental.pallas import tpu as pltpu

F32 = jnp.float32
BF16 = jnp.bfloat16
I32 = jnp.int32
I16 = jnp.int16

HEAD_DIM = 128
ROPE_DIM = HEAD_DIM // 4
ROPE_THETA = 500000.0
EPS = 1e-6
NEG = -1e30
N_HEADS_A = 8
WIDTH_A = N_HEADS_A * HEAD_DIM
Q_RANK = 512
KV_RANK = 256
IDX_HEADS = 16
IDX_DIM = 64
IDX_ROPE = IDX_DIM // 4
TOPK_MAX = 256
QUERY_BLOCK = 128
N_HEADS_B = 8
WIDTH_B = N_HEADS_B * HEAD_DIM
DILATED_CONFIGS = ((128, 1), (512, 4), (2048, 16))
N_BACK = 128
SMALL_W = 1024
IN_PAD = SMALL_W + 5 * 1024
KCAT = 384
KIDX_PAD = 128
INT_MIN = -(2 ** 31)
LOG2E = 1.4426950408889634

LANE = 128
PACK16 = 16
VMEM_LIMIT = 56 * 1024 * 1024
PROJ_TM = 1024
PROJ_TN = 512
PREP_TQ = 512
KEY_CHUNK = 512
OUT_TM = 512
ROW_CHUNK = 512
NORM_ROWS = 128


def _cparams(sem):
    return pltpu.CompilerParams(dimension_semantics=sem, vmem_limit_bytes=VMEM_LIMIT)


def _silu(g):
    return g * jax.nn.sigmoid(g)


def _split_bf16(x):
    hi = x.astype(BF16)
    return hi, (x - hi.astype(F32)).astype(BF16)


def _mod_kernel(c_ref, w_ref, b_ref, o_ref):
    a_hi, a_lo = _split_bf16(_silu(c_ref[...]))
    w_hi, w_lo = _split_bf16(w_ref[...])
    dot = functools.partial(jnp.dot, preferred_element_type=F32)
    o_ref[...] = dot(a_hi, w_hi) + (dot(a_hi, w_lo) + dot(a_lo, w_hi)) + b_ref[...]


def _mod_call(c_pad, w_ada, b_ada):
    rows, d = c_pad.shape
    n = w_ada.shape[1]
    tn = 1024
    return pl.pallas_call(
        _mod_kernel,
        grid=(n // tn,),
        in_specs=[pl.BlockSpec((rows, d), lambda j: (0, 0)),
                  pl.BlockSpec((d, tn), lambda j: (0, j)),
                  pl.BlockSpec((1, tn), lambda j: (0, j))],
        out_specs=pl.BlockSpec((rows, tn), lambda j: (0, j)),
        out_shape=jax.ShapeDtypeStruct((rows, n), F32),
        compiler_params=_cparams(("arbitrary",)),
        name="mod",
    )(c_pad, w_ada, b_ada)


def _proj_kernel(x_ref, mod_ref, g_ref, w_ref, small_ref, qkv_ref, gates_ref, h_ref, *,
                 n_small, n_qkv):
    j = pl.program_id(1)

    @pl.when(j == 0)
    def _():
        shift = mod_ref[0, 0:1, :]
        scale1 = 1.0 + mod_ref[0, 1:2, :]
        g = g_ref[...]

        def body(r, carry):
            rows = pl.ds(pl.multiple_of(r * NORM_ROWS, NORM_ROWS), NORM_ROWS)
            x = x_ref[rows, :]
            ms = jnp.mean(x * x, axis=-1, keepdims=True)
            y = x * lax.rsqrt(ms + EPS) * g
            h_ref[rows, :] = (y * scale1 + shift).astype(BF16)
            return carry

        lax.fori_loop(0, x_ref.shape[0] // NORM_ROWS, body, 0)

    acc = jnp.dot(h_ref[...], w_ref[...], preferred_element_type=F32)

    @pl.when(j < n_small)
    def _():
        small_ref[...] = acc

    @pl.when((j >= n_small) & (j < n_small + n_qkv))
    def _():
        qkv_ref[...] = acc

    @pl.when(j >= n_small + n_qkv)
    def _():
        gates_ref[...] = acc.astype(BF16)


def _proj_call(x2, mod3, g_pre, w_all, seq):
    rows, d = x2.shape
    tm = min(PROJ_TM, seq)
    tn = PROJ_TN
    n_small = SMALL_W // tn
    n_qkv = 3 * WIDTH_B // tn
    n_gate = (WIDTH_A + WIDTH_B) // tn
    tiles_per_batch = seq // tm
    kern = functools.partial(_proj_kernel, n_small=n_small, n_qkv=n_qkv)
    return pl.pallas_call(
        kern,
        grid=(rows // tm, n_small + n_qkv + n_gate),
        in_specs=[pl.BlockSpec((tm, d), lambda i, j: (i, 0)),
                  pl.BlockSpec((1, 3, d), lambda i, j: (i // tiles_per_batch, 0, 0)),
                  pl.BlockSpec((1, d), lambda i, j: (0, 0)),
                  pl.BlockSpec((d, tn), lambda i, j: (0, j))],
        out_specs=[pl.BlockSpec((tm, tn), lambda i, j: (i, jnp.minimum(j, n_small - 1))),
                   pl.BlockSpec((tm, tn), lambda i, j: (i, jnp.clip(j - n_small, 0, n_qkv - 1))),
                   pl.BlockSpec((tm, tn),
                                lambda i, j: (i, jnp.clip(j - n_small - n_qkv, 0, n_gate - 1)))],
        out_shape=[jax.ShapeDtypeStruct((rows, SMALL_W), F32),
                   jax.ShapeDtypeStruct((rows, 3 * WIDTH_B), F32),
                   jax.ShapeDtypeStruct((rows, WIDTH_A + WIDTH_B), BF16)],
        scratch_shapes=[pltpu.VMEM((tm, d), BF16)],
        compiler_params=_cparams(("parallel", "arbitrary")),
        name="in_proj",
    )(x2, mod3, g_pre, w_all)


def _prep_kernel(small_ref, gq_ref, gkv_ref, wuq_ref, wuqi_ref, wuk_ref,
                 cosq_ref, sinq_ref, cosi_ref, sini_ref, ck_ref, sk_ref, ci_ref, si_ref,
                 qcat_ref, qidx_ref, wt_ref, kcat_ref, kidx_ref, ckvt_ref, *, qscale):
    tq = small_ref.shape[0]
    nblk = tq // QUERY_BLOCK
    cq = small_ref[:, 0:Q_RANK]
    cqn = (cq * lax.rsqrt(jnp.mean(cq * cq, axis=-1, keepdims=True) + EPS)
           * gq_ref[...]).astype(BF16)
    nt = (((1,), (1,)), ((), ()))

    q_t = lax.dot_general(wuq_ref[...], cqn, nt, preferred_element_type=F32)
    cosq = cosq_ref[...]
    sinq = sinq_ref[...]
    half = ROPE_DIM // 2
    zpad = jnp.zeros((KCAT - KV_RANK - ROPE_DIM, tq), F32)
    for h in range(N_HEADS_A):
        base = h * HEAD_DIM
        x1 = q_t[base:base + half]
        x2 = q_t[base + half:base + ROPE_DIM]
        q_rope = jnp.concatenate([x1 * cosq - x2 * sinq, x2 * cosq + x1 * sinq, zpad], axis=0)
        q_lat = jnp.dot(wuk_ref[h], q_t[base:base + HEAD_DIM].astype(BF16),
                        preferred_element_type=F32)
        q_lat = (q_lat * qscale).astype(BF16)
        q_rope = (q_rope * qscale).astype(BF16)
        for blk in range(nblk):
            cols = slice(blk * QUERY_BLOCK, (blk + 1) * QUERY_BLOCK)
            lanes = slice(h * QUERY_BLOCK, (h + 1) * QUERY_BLOCK)
            qcat_ref[blk, 0:KV_RANK, lanes] = q_lat[:, cols]
            qcat_ref[blk, KV_RANK:KCAT, lanes] = q_rope[:, cols]

    qi_t = lax.dot_general(wuqi_ref[...], cqn, nt, preferred_element_type=F32)
    cosi = cosi_ref[...]
    sini = sini_ref[...]
    ihalf = IDX_ROPE // 2
    ipad = jnp.zeros((KIDX_PAD - IDX_DIM, tq), F32)
    for h in range(IDX_HEADS):
        base = h * IDX_DIM
        x1 = qi_t[base:base + ihalf]
        x2 = qi_t[base + ihalf:base + IDX_ROPE]
        qi = jnp.concatenate([x1 * cosi - x2 * sini, x2 * cosi + x1 * sini,
                              qi_t[base + IDX_ROPE:base + IDX_DIM], ipad], axis=0).astype(BF16)
        for blk in range(nblk):
            cols = slice(blk * QUERY_BLOCK, (blk + 1) * QUERY_BLOCK)
            qidx_ref[blk, :, h * QUERY_BLOCK:(h + 1) * QUERY_BLOCK] = qi[:, cols]

    lane = lax.broadcasted_iota(I32, (tq, LANE), 1)

    slab_b = small_ref[:, 896:1024]
    swap_b = jnp.where(lane < ihalf, pltpu.roll(slab_b, LANE - ihalf, 1), pltpu.roll(slab_b, ihalf, 1))
    kidx_ref[...] = (slab_b * ci_ref[...] + swap_b * si_ref[...]).astype(BF16)
    w_t = slab_b.T[IDX_DIM:IDX_DIM + IDX_HEADS] * (IDX_HEADS ** -0.5 * IDX_DIM ** -0.5)
    for blk in range(nblk):
        wt_ref[blk] = w_t[:, blk * QUERY_BLOCK:(blk + 1) * QUERY_BLOCK]

    ckv = small_ref[:, Q_RANK:Q_RANK + KV_RANK]
    cn = ckv * lax.rsqrt(jnp.mean(ckv * ckv, axis=-1, keepdims=True) + EPS) * gkv_ref[...]
    kcat_ref[:, 0:KV_RANK] = cn.astype(BF16)
    ckvt_ref[0] = cn.T.astype(BF16)
    slab_a = small_ref[:, 768:896]
    swap_a = jnp.where(lane < half, pltpu.roll(slab_a, LANE - half, 1), pltpu.roll(slab_a, half, 1))
    kcat_ref[:, KV_RANK:KCAT] = (slab_a * ck_ref[...] + swap_a * sk_ref[...]).astype(BF16)


def _prep_call(small, g_q, g_kv, wuq_t, wuqi_t, wuk_t, tabs, seq):
    rows = small.shape[0]
    tq = PREP_TQ
    nblk = tq // QUERY_BLOCK
    tpb = seq // tq
    nqb = rows // QUERY_BLOCK
    qscale = HEAD_DIM ** -0.5 * LOG2E
    kern = functools.partial(_prep_kernel, qscale=qscale)
    const = lambda t: (0, 0)
    tcol = lambda t: (0, t % tpb)
    trow = lambda t: (t % tpb, 0)
    return pl.pallas_call(
        kern,
        grid=(rows // tq,),
        in_specs=[pl.BlockSpec((tq, SMALL_W), lambda t: (t, 0)),
                  pl.BlockSpec((1, Q_RANK), const),
                  pl.BlockSpec((1, KV_RANK), const),
                  pl.BlockSpec((WIDTH_A, Q_RANK), const),
                  pl.BlockSpec((IDX_HEADS * IDX_DIM, Q_RANK), const),
                  pl.BlockSpec((N_HEADS_A, KV_RANK, HEAD_DIM), lambda t: (0, 0, 0)),
                  pl.BlockSpec((ROPE_DIM // 2, tq), tcol),
                  pl.BlockSpec((ROPE_DIM // 2, tq), tcol),
                  pl.BlockSpec((IDX_ROPE // 2, tq), tcol),
                  pl.BlockSpec((IDX_ROPE // 2, tq), tcol),
                  pl.BlockSpec((tq, LANE), trow),
                  pl.BlockSpec((tq, LANE), trow),
                  pl.BlockSpec((tq, LANE), trow),
                  pl.BlockSpec((tq, LANE), trow)],
        out_specs=[pl.BlockSpec((nblk, KCAT, N_HEADS_A * QUERY_BLOCK), lambda t: (t, 0, 0)),
                   pl.BlockSpec((nblk, KIDX_PAD, IDX_HEADS * QUERY_BLOCK), lambda t: (t, 0, 0)),
                   pl.BlockSpec((nblk, IDX_HEADS, QUERY_BLOCK), lambda t: (t, 0, 0)),
                   pl.BlockSpec((tq, KCAT), lambda t: (t, 0)),
                   pl.BlockSpec((tq, KIDX_PAD), lambda t: (t, 0)),
                   pl.BlockSpec((1, KV_RANK, tq), lambda t: (t, 0, 0))],
        out_shape=[jax.ShapeDtypeStruct((nqb, KCAT, N_HEADS_A * QUERY_BLOCK), BF16),
                   jax.ShapeDtypeStruct((nqb, KIDX_PAD, IDX_HEADS * QUERY_BLOCK), BF16),
                   jax.ShapeDtypeStruct((nqb, IDX_HEADS, QUERY_BLOCK), F32),
                   jax.ShapeDtypeStruct((rows, KCAT), BF16),
                   jax.ShapeDtypeStruct((rows, KIDX_PAD), BF16),
                   jax.ShapeDtypeStruct((rows // tq, KV_RANK, tq), BF16)],
        compiler_params=_cparams(("parallel",)),
        name="dsa_prep",
    )(small, g_q, g_kv, wuq_t, wuqi_t, wuk_t, *tabs)


def _ordered_key(x):
    bits = lax.bitcast_convert_type(x, I32)
    return bits ^ (lax.shift_right_arithmetic(bits, 31) & 0x7FFFFFFF)


def _ordered_key_const(value):
    bits = int(np.float32(value).view(np.int32))
    return bits ^ ((bits >> 31) & 0x7FFFFFFF)


_NEG_KEY = _ordered_key_const(NEG)


def _dsa_kernel(kidx_ref, kcat_ref, ckvt_ref, qcat_ref, qidx_ref, wt_ref, gate_ref, wuv_ref,
                o_ref, keys_ref, k16_ref, m_ref, l_ref, acc_ref, s0_ref, s1_ref, *, k_sel):
    i = pl.program_id(1)
    kc = KEY_CHUNK
    qb = QUERY_BLOCK
    per = kc // qb
    nch = (i + per) // per
    qpos = i * qb + lax.broadcasted_iota(I32, (kc, qb), 1)
    krow = lax.broadcasted_iota(I32, (kc, qb), 0)
    neg_key = _NEG_KEY

    def idx_body(c, carry):
        k0 = pl.multiple_of(c * kc, kc)
        kblk = kidx_ref[pl.ds(k0, kc), :]
        acc = jnp.zeros((kc, qb), F32)
        group = 4
        for g in range(IDX_HEADS // group):
            lg = jnp.dot(kblk, qidx_ref[0, :, g * group * qb:(g + 1) * group * qb],
                         preferred_element_type=F32)
            for hh in range(group):
                h = g * group + hh
                acc = acc + jnp.maximum(lg[:, hh * qb:(hh + 1) * qb], 0.0) * wt_ref[0, h:h + 1, :]
        causal = (k0 + krow) <= qpos
        key = jnp.where(causal, _ordered_key(acc), neg_key)
        keys_ref[pl.ds(k0, kc), :] = key
        k16_ref[pl.ds(k0, kc), :] = lax.shift_right_arithmetic(key, 16).astype(I16)
        return carry

    lax.fori_loop(0, nch, idx_body, 0)

    half_rows = kc // PACK16
    lowest16 = -2 ** 15

    @pl.when(nch % 2 == 1)
    def _():
        k16_ref[pl.ds(pl.multiple_of(nch * kc, kc), kc), :] = jnp.full((kc, qb), lowest16, I16)

    def count16(cand_s):
        cand16 = jnp.broadcast_to(cand_s, (PACK16, qb)).astype(I16)[None]

        def cnt_body(c, cnt):
            rows = pl.ds(pl.multiple_of(c * 2 * kc, 2 * kc), 2 * kc)
            k = k16_ref[rows, :].reshape(2 * half_rows, PACK16, qb)
            sel = jnp.where(k >= cand16, jnp.int16(1), jnp.int16(0))
            parts = [sel[r] for r in range(2 * half_rows)]
            while len(parts) > 1:
                parts = [a + b for a, b in zip(parts[::2], parts[1::2])]
            return cnt + parts[0]

        cnt = lax.fori_loop(0, (nch + 1) // 2, cnt_body, jnp.zeros((PACK16, qb), I16))
        return jnp.sum(cnt.astype(I32), axis=0, keepdims=True)

    def bisect16(need):
        def body(s, tu):
            cand_u = tu | lax.shift_left(jnp.int32(1), 15 - s)
            return jnp.where(count16(cand_u - 2 ** 15) >= need, cand_u, tu)

        return lax.fori_loop(0, 16, body, jnp.zeros((1, qb), I32))

    hi_s = bisect16(k_sel) - 2 ** 15
    top = 2 ** 15 - 1
    above = jnp.where(hi_s >= top, 0, count16(jnp.minimum(hi_s + 1, top)))
    need_lo = k_sel - above
    hi16 = jnp.broadcast_to(hi_s, (PACK16, qb)).astype(I16)[None]

    def lo_body(c, carry):
        rows = pl.ds(pl.multiple_of(c * kc, kc), kc)
        lo = (keys_ref[rows, :] ^ 0x8000).astype(I16).reshape(half_rows, PACK16, qb)
        hi = k16_ref[rows, :].reshape(half_rows, PACK16, qb)
        k16_ref[rows, :] = jnp.where(hi == hi16, lo, jnp.int16(lowest16)).reshape(kc, qb)
        return carry

    lax.fori_loop(0, nch, lo_body, 0)
    lo_u = bisect16(need_lo)
    thr = jnp.maximum(lax.shift_left(hi_s, 16) | lo_u, INT_MIN + 1)

    m_ref[...] = jnp.full(m_ref.shape, NEG, F32)
    l_ref[...] = jnp.zeros(l_ref.shape, F32)
    acc_ref[...] = jnp.zeros(acc_ref.shape, F32)
    pair = 2 * qb

    def scores(c, dst):
        k0 = pl.multiple_of(c * kc, kc)
        dst[...] = jnp.dot(kcat_ref[pl.ds(k0, kc), :], qcat_ref[0], preferred_element_type=F32)

    def att_step(c, cur, nxt):
        if nxt is not None:
            scores(c + 1, nxt)
        k0 = pl.multiple_of(c * kc, kc)
        key = keys_ref[pl.ds(k0, kc), :]
        causal = (k0 + krow) <= qpos
        sel = jnp.where(causal, key, INT_MIN) >= thr
        bias = jnp.where(sel, 0.0, NEG).astype(F32)
        s = cur[...] + jnp.concatenate([bias] * N_HEADS_A, axis=1)
        m_old = m_ref[...]
        m_new = jnp.maximum(m_old, jnp.max(s, axis=0, keepdims=True))
        alpha = jnp.exp2(m_old - m_new)
        p = jnp.exp2(s - m_new)
        l_ref[...] = alpha * l_ref[...] + jnp.sum(p, axis=0, keepdims=True)
        acc_ref[...] = alpha * acc_ref[...] + jnp.dot(ckvt_ref[c], p.astype(BF16),
                                                      preferred_element_type=F32)
        m_ref[...] = m_new

    scores(0, s0_ref)
    npair = (nch - 1) // 2

    def att_body(j, carry):
        att_step(2 * j, s0_ref, s1_ref)
        att_step(2 * j + 1, s1_ref, s0_ref)
        return carry

    lax.fori_loop(0, npair, att_body, 0)

    @pl.when(nch % 2 == 0)
    def _():
        att_step(nch - 2, s0_ref, s1_ref)
        att_step(nch - 1, s1_ref, None)

    @pl.when(nch % 2 == 1)
    def _():
        att_step(nch - 1, s0_ref, None)

    inv_l = 1.0 / l_ref[...]
    outs = []
    for h in range(N_HEADS_A):
        cols = slice(h * qb, (h + 1) * qb)
        o_t = acc_ref[:, cols] * inv_l[:, cols]
        outs.append(jnp.dot(o_t.T.astype(BF16), wuv_ref[h], preferred_element_type=F32))
    o = jnp.concatenate(outs, axis=1)
    g = gate_ref[...].astype(F32)
    o_ref[...] = (o * _silu(g)).astype(BF16)


def _dsa_call(kidx, kcat, ckvt, qcat, qidx, w_t, gates, wuv, batch, seq):
    rows = kidx.shape[0]
    nb = seq // QUERY_BLOCK
    nkc = seq // KEY_CHUNK
    k_sel = min(TOPK_MAX, seq // 4)
    kern = functools.partial(_dsa_kernel, k_sel=k_sel)
    return pl.pallas_call(
        kern,
        grid=(batch, nb),
        in_specs=[pl.BlockSpec((seq, KIDX_PAD), lambda b, i: (b, 0)),
                  pl.BlockSpec((seq, KCAT), lambda b, i: (b, 0)),
                  pl.BlockSpec((nkc, KV_RANK, KEY_CHUNK), lambda b, i: (b, 0, 0)),
                  pl.BlockSpec((1, KCAT, N_HEADS_A * QUERY_BLOCK), lambda b, i: (b * nb + i, 0, 0)),
                  pl.BlockSpec((1, KIDX_PAD, IDX_HEADS * QUERY_BLOCK), lambda b, i: (b * nb + i, 0, 0)),
                  pl.BlockSpec((1, IDX_HEADS, QUERY_BLOCK), lambda b, i: (b * nb + i, 0, 0)),
                  pl.BlockSpec((QUERY_BLOCK, WIDTH_A), lambda b, i: (b * nb + i, 0)),
                  pl.BlockSpec((N_HEADS_A, KV_RANK, HEAD_DIM), lambda b, i: (0, 0, 0))],
        out_specs=pl.BlockSpec((QUERY_BLOCK, WIDTH_A), lambda b, i: (b * nb + i, 0)),
        out_shape=jax.ShapeDtypeStruct((rows, WIDTH_A), BF16),
        scratch_shapes=[pltpu.VMEM((seq, QUERY_BLOCK), I32),
                        pltpu.VMEM((seq + KEY_CHUNK, QUERY_BLOCK), I16),
                        pltpu.VMEM((1, N_HEADS_A * QUERY_BLOCK), F32),
                        pltpu.VMEM((1, N_HEADS_A * QUERY_BLOCK), F32),
                        pltpu.VMEM((KV_RANK, N_HEADS_A * QUERY_BLOCK), F32),
                        pltpu.VMEM((KEY_CHUNK, N_HEADS_A * QUERY_BLOCK), F32),
                        pltpu.VMEM((KEY_CHUNK, N_HEADS_A * QUERY_BLOCK), F32)],
        compiler_params=_cparams(("parallel", "arbitrary")),
        name="dsa_attn",
    )(kidx, kcat, ckvt, qcat, qidx, w_t, gates, wuv)


def _dil_kernel(q_ref, k_ref, v_ref, gate_ref, c_ref, s_ref, o_ref,
                nat_ref, a4_ref, qn, kn, vn, q4, k4, v4, q16, k16, v16,
                oc0, ls0, oc1, ls1, oc2, ls2, *, qscale, unroll):
    seq = q_ref.shape[0]
    half = ROPE_DIM // 2
    blk = N_BACK
    rc = ROW_CHUNK
    m4 = seq // 4
    m16 = seq // 16
    lane = lax.broadcasted_iota(I32, (rc, LANE), 1)

    def build(src_ref, dn, d4, d16, rope_scale):
        def nat_body(c, carry):
            rows = pl.ds(pl.multiple_of(c * rc, rc), rc)
            x = src_ref[rows, :]
            if rope_scale is not None:
                sw = jnp.where(lane < half, pltpu.roll(x, LANE - half, 1), pltpu.roll(x, half, 1))
                x = x * c_ref[rows, :] + sw * s_ref[rows, :]
                if rope_scale != 1.0:
                    x = x * rope_scale
                nat_ref[rows, :] = x
            dn[rows, :] = x.astype(BF16)
            return carry

        lax.fori_loop(0, seq // rc, nat_body, 0)
        nat = src_ref if rope_scale is None else nat_ref

        def a4_body(c, carry):
            r4 = c // (m4 // rc)
            j0 = (c % (m4 // rc)) * rc
            x = nat[pl.ds(r4 + 4 * j0, rc, stride=4), :]
            rows = pl.ds(pl.multiple_of(c * rc, rc), rc)
            a4_ref[rows, :] = x
            d4[rows, :] = x.astype(BF16)
            return carry

        lax.fori_loop(0, seq // rc, a4_body, 0)

        def a16_body(r16, carry):
            r4 = r16 % 4
            s = r16 // 4
            x = a4_ref[pl.ds(r4 * m4 + s, m16, stride=4), :]
            d16[pl.ds(pl.multiple_of(r16 * m16, m16), m16), :] = x.astype(BF16)
            return carry

        lax.fori_loop(0, 16, a16_body, 0)

    build(q_ref, qn, q4, q16, qscale)
    build(k_ref, kn, k4, k16, 1.0)
    build(v_ref, vn, v4, v16, None)

    qi = lax.broadcasted_iota(I32, (blk, 2 * blk), 0) + blk
    kj = lax.broadcasted_iota(I32, (blk, 2 * blk), 1)
    rel = qi - kj
    bias_band = jnp.where((rel >= 0) & (rel <= N_BACK), 0.0, NEG).astype(F32)
    no_prev = jnp.where(kj < blk, NEG, 0.0).astype(F32)
    qi1 = lax.broadcasted_iota(I32, (blk, blk), 0)
    kj1 = lax.broadcasted_iota(I32, (blk, blk), 1)
    bias_first = jnp.where(kj1 <= qi1, 0.0, NEG).astype(F32)
    nt = (((1,), (1,)), ((), ()))

    def run_config(qs, ks, vs, nmb, store):
        def rows_of(u, first):
            cur = pl.ds(pl.multiple_of(u * blk, blk), blk)
            if first is True:
                return (cur,)
            return (pl.ds(pl.multiple_of(jnp.maximum(u * blk - blk, 0), blk), blk), cur)

        def gather(ref, rows):
            parts = [ref[r, :] for r in rows]
            return parts[0] if len(parts) == 1 else jnp.concatenate(parts, axis=0)

        def loop_body(g, carry):
            units = []
            for t in range(unroll):
                if nmb <= unroll:
                    first = (t % nmb == 0)
                else:
                    first = (g == 0).astype(F32) if t == 0 else False
                units.append((g * unroll + t, first))
            scores = []
            for u, first in units:
                rows = rows_of(u, first)
                if first is True:
                    bias = bias_first
                elif first is False:
                    bias = bias_band
                else:
                    bias = bias_band + no_prev * first
                scores.append(lax.dot_general(qs[rows[-1], :], gather(ks, rows), nt,
                                              preferred_element_type=F32) + bias)
            probs = []
            for s in scores:
                m = jnp.max(s, axis=-1, keepdims=True)
                p = jnp.exp2(s - m)
                l = jnp.sum(p, axis=-1, keepdims=True)
                probs.append((p.astype(BF16), 1.0 / l, jnp.broadcast_to(m + jnp.log2(l), (blk, LANE))))
            for (u, first), (p, inv_l, lse) in zip(units, probs):
                o = jnp.dot(p, gather(vs, rows_of(u, first)), preferred_element_type=F32) * inv_l
                store(u, o, lse)
            return carry

        lax.fori_loop(0, seq // blk // unroll, loop_body, 0)

    def store_contig(oc, ls):
        def store(u, o, lse):
            rows = pl.ds(pl.multiple_of(u * blk, blk), blk)
            oc[rows, :] = o
            ls[rows, :] = lse
        return store

    def store_a16_to_a4(u, o, lse):
        r16 = u // (m16 // blk)
        mb = u % (m16 // blk)
        rows = pl.ds((r16 % 4) * m4 + 4 * blk * mb + r16 // 4, blk, stride=4)
        oc2[rows, :] = o
        ls2[rows, :] = lse

    run_config(qn, kn, vn, seq // blk, store_contig(oc0, ls0))
    run_config(q4, k4, v4, m4 // blk, store_contig(oc1, ls1))
    run_config(q16, k16, v16, m16 // blk, store_a16_to_a4)

    def merge_body(c, carry):
        r4 = c // (m4 // rc)
        j0 = (c % (m4 // rc)) * rc
        nat_rows = pl.ds(r4 + 4 * j0, rc, stride=4)
        rows = pl.ds(pl.multiple_of(c * rc, rc), rc)
        l0 = ls0[nat_rows, :]
        l1 = ls1[rows, :]
        l2 = ls2[rows, :]
        mx = jnp.maximum(jnp.maximum(l0, l1), l2)
        e0 = jnp.exp2(l0 - mx)
        e1 = jnp.exp2(l1 - mx)
        e2 = jnp.exp2(l2 - mx)
        nat_ref[nat_rows, :] = (e0 * oc0[nat_rows, :] + e1 * oc1[rows, :] + e2 * oc2[rows, :]) / (e0 + e1 + e2)
        return carry

    lax.fori_loop(0, seq // rc, merge_body, 0)

    def gate_body(c, carry):
        rows = pl.ds(pl.multiple_of(c * rc, rc), rc)
        g = gate_ref[rows, :].astype(F32)
        o_ref[rows, :] = (nat_ref[rows, :] * _silu(g)).astype(BF16)
        return carry

    lax.fori_loop(0, seq // rc, gate_body, 0)


def _dil_call(qkv, gates, ctab, stab, batch, seq):
    rows = qkv.shape[0]
    nh = N_HEADS_B
    qscale = HEAD_DIM ** -0.5 * LOG2E
    kern = functools.partial(_dil_kernel, qscale=qscale, unroll=16)
    blk = (seq, HEAD_DIM)
    return pl.pallas_call(
        kern,
        grid=(batch, nh),
        in_specs=[pl.BlockSpec(blk, lambda b, h: (b, h)),
                  pl.BlockSpec(blk, lambda b, h: (b, nh + h)),
                  pl.BlockSpec(blk, lambda b, h: (b, 2 * nh + h)),
                  pl.BlockSpec(blk, lambda b, h: (b, nh + h)),
                  pl.BlockSpec(blk, lambda b, h: (0, 0), pipeline_mode=pl.Buffered(1)),
                  pl.BlockSpec(blk, lambda b, h: (0, 0), pipeline_mode=pl.Buffered(1))],
        out_specs=pl.BlockSpec(blk, lambda b, h: (b, h)),
        out_shape=jax.ShapeDtypeStruct((rows, WIDTH_B), BF16),
        scratch_shapes=([pltpu.VMEM(blk, F32)] * 2 + [pltpu.VMEM(blk, BF16)] * 9
                        + [pltpu.VMEM(blk, F32)] * 6),
        compiler_params=_cparams(("parallel", "arbitrary")),
        name="dilated",
    )(qkv, qkv, qkv, gates, ctab, stab)


def _out_kernel(oa_ref, ob_ref, w_ref, x_ref, mod_ref, g_ref, o_ref):
    y = jnp.dot(oa_ref[...], w_ref[0:WIDTH_A, :], preferred_element_type=F32)
    y = y + jnp.dot(ob_ref[...], w_ref[WIDTH_A:WIDTH_A + WIDTH_B, :], preferred_element_type=F32)
    yn = y * lax.rsqrt(jnp.mean(y * y, axis=-1, keepdims=True) + EPS) * g_ref[...]
    o_ref[...] = x_ref[...] + mod_ref[0, 2:3, :] * yn


def _out_call(o_a, o_b, w_out, x2, mod3, g_post, seq):
    rows, d = x2.shape
    tm = OUT_TM
    tpb = seq // tm
    return pl.pallas_call(
        _out_kernel,
        grid=(rows // tm,),
        in_specs=[pl.BlockSpec((tm, WIDTH_A), lambda i: (i, 0)),
                  pl.BlockSpec((tm, WIDTH_B), lambda i: (i, 0)),
                  pl.BlockSpec((WIDTH_A + WIDTH_B, d), lambda i: (0, 0)),
                  pl.BlockSpec((tm, d), lambda i: (i, 0)),
                  pl.BlockSpec((1, 3, d), lambda i: (i // tpb, 0, 0)),
                  pl.BlockSpec((1, d), lambda i: (0, 0))],
        out_specs=pl.BlockSpec((tm, d), lambda i: (i, 0)),
        out_shape=jax.ShapeDtypeStruct((rows, d), F32),
        compiler_params=_cparams(("parallel",)),
        name="out_proj",
    )(o_a, o_b, w_out, x2, mod3, g_post)


def _rope_tables(seq):
    pos = jnp.arange(seq, dtype=F32)

    def cs(n_rot):
        inv = ROPE_THETA ** (-jnp.arange(0, n_rot, 2, dtype=F32) / n_rot)
        ang = pos[:, None] * inv[None, :]
        return jnp.cos(ang), jnp.sin(ang)

    c32, s32 = cs(ROPE_DIM)
    c16, s16 = cs(IDX_ROPE)
    ck = jnp.concatenate([c32, c32, jnp.ones((seq, LANE - ROPE_DIM), F32)], axis=1)
    sk = jnp.concatenate([-s32, s32, jnp.zeros((seq, LANE - ROPE_DIM), F32)], axis=1)
    ci = jnp.concatenate([c16, c16, jnp.ones((seq, LANE - IDX_ROPE), F32)], axis=1)
    si = jnp.concatenate([-s16, s16, jnp.zeros((seq, LANE - IDX_ROPE), F32)], axis=1)
    return (c32.T, s32.T, c16.T, s16.T, ck, sk, ci, si)


def _layer(x, c, w_ada, b_ada, g_pre, g_post, w_in, g_q, g_kv, w_uq, w_uq_idx, w_uk, w_uv, w_out):
    batch, seq, d = x.shape
    assert seq % PROJ_TM == 0 or seq < PROJ_TM
    assert seq % KEY_CHUNK == 0 and seq // 16 >= N_BACK
    assert all(w // dl == N_BACK for w, dl in DILATED_CONFIGS)
    rows = batch * seq
    x2 = x.reshape(rows, d)

    o_cq, o_ckv, o_kr, o_ki, o_wi = 0, 512, 768, 800, 864
    o_ga, o_q, o_k, o_v, o_gb = 880, 1904, 2928, 3952, 4976
    zeros = lambda n: jnp.zeros((d, n), w_in.dtype)
    w_all = jnp.concatenate([
        w_in[:, o_cq:o_ki], zeros(896 - 800), w_in[:, o_ki:o_ga], zeros(SMALL_W - 976),
        w_in[:, o_q:o_gb], w_in[:, o_ga:o_q], w_in[:, o_gb:o_gb + WIDTH_B]], axis=1).astype(BF16)
    assert w_all.shape[1] == IN_PAD

    c_pad = jnp.zeros((PACK16, d), F32).at[:batch].set(c)
    mod = _mod_call(c_pad, w_ada, b_ada.reshape(1, -1))[:batch]
    mod3 = mod.reshape(batch, 3, d)

    small, qkv, gates = _proj_call(x2, mod3, g_pre.reshape(1, d), w_all, seq)

    tabs = _rope_tables(seq)
    wuk_t = jnp.transpose(jnp.pad(w_uk, ((0, 0), (ROPE_DIM, 0), (0, 0))), (0, 2, 1)).astype(BF16)
    qcat, qidx, w_t, kcat, kidx, ckvt = _prep_call(
        small, g_q.reshape(1, -1), g_kv.reshape(1, -1), w_uq.T.astype(BF16),
        w_uq_idx.T.astype(BF16), wuk_t, tabs, seq)

    o_a = _dsa_call(kidx, kcat, ckvt, qcat, qidx, w_t, gates, w_uv.astype(BF16), batch, seq)
    o_b = _dil_call(qkv, gates, tabs[4], tabs[5], batch, seq)
    out = _out_call(o_a, o_b, w_out.astype(BF16), x2, mod3, g_post.reshape(1, d), seq)
    return out.reshape(batch, seq, d)


def kernel(x, c, w_ada, b_ada, g_pre, g_post, w_in, g_q, g_kv, w_uq, w_uq_idx, w_uk, w_uv, w_out):
    for layer in range(w_ada.shape[0]):
        x = _layer(x, c, w_ada[layer], b_ada[layer], g_pre[layer], g_post[layer], w_in[layer],
                   g_q[layer], g_kv[layer], w_uq[layer], w_uq_idx[layer], w_uk[layer],
                   w_uv[layer], w_out[layer])
    return x
```

```python
import functools
import math

import numpy as np
import jax
import jax.numpy as jnp
from jax import lax
from jax.experimental import pallas as pl
from jax.experimental.pallas import tpu as pltpu

F32 = jnp.float32
BF16 = jnp.bfloat16
I32 = jnp.int32

HEAD_DIM = 128
ROPE_DIM = HEAD_DIM // 4
ROPE_THETA = 500000.0
EPS = 1e-6
NEG = -1e30
N_HEADS_A = 8
WIDTH_A = N_HEADS_A * HEAD_DIM
Q_RANK = 512
KV_RANK = 256
IDX_HEADS = 16
IDX_DIM = 64
IDX_ROPE = IDX_DIM // 4
TOPK_MAX = 256
QUERY_BLOCK = 128
N_HEADS_B = 8
WIDTH_B = N_HEADS_B * HEAD_DIM
DILATED_CONFIGS = ((128, 1), (512, 4), (2048, 16))
N_BACK = 128
SMALL_W = 1024
IN_PAD = SMALL_W + 5 * 1024
KCAT = 384
KIDX_PAD = 128
INT_MIN = -(2 ** 31)
LOG2E = 1.4426950408889634

LANE = 128
PACK16 = 16
VMEM_LIMIT = 56 * 1024 * 1024
PROJ_TM = 1024
PROJ_TN = 512
PREP_TQ = 512
KEY_CHUNK = 512
OUT_TM = 512
ROW_CHUNK = 512
NORM_ROWS = 128


def _cparams(sem):
    return pltpu.CompilerParams(dimension_semantics=sem, vmem_limit_bytes=VMEM_LIMIT)


def _silu(g):
    return g * jax.nn.sigmoid(g)


def _split_bf16(x):
    hi = x.astype(BF16)
    return hi, (x - hi.astype(F32)).astype(BF16)


def _mod_kernel(c_ref, w_ref, b_ref, o_ref):
    a_hi, a_lo = _split_bf16(_silu(c_ref[...]))
    w_hi, w_lo = _split_bf16(w_ref[...])
    dot = functools.partial(jnp.dot, preferred_element_type=F32)
    o_ref[...] = dot(a_hi, w_hi) + (dot(a_hi, w_lo) + dot(a_lo, w_hi)) + b_ref[...]


def _mod_call(c_pad, w_ada, b_ada):
    rows, d = c_pad.shape
    n = w_ada.shape[1]
    tn = 1024
    return pl.pallas_call(
        _mod_kernel,
        grid=(n // tn,),
        in_specs=[pl.BlockSpec((rows, d), lambda j: (0, 0)),
                  pl.BlockSpec((d, tn), lambda j: (0, j)),
                  pl.BlockSpec((1, tn), lambda j: (0, j))],
        out_specs=pl.BlockSpec((rows, tn), lambda j: (0, j)),
        out_shape=jax.ShapeDtypeStruct((rows, n), F32),
        compiler_params=_cparams(("arbitrary",)),
        name="mod",
    )(c_pad, w_ada, b_ada)


def _proj_kernel(x_ref, mod_ref, g_ref, w_ref, small_ref, qkv_ref, gates_ref, h_ref, *,
                 n_small, n_qkv):
    j = pl.program_id(1)

    @pl.when(j == 0)
    def _():
        shift = mod_ref[0, 0:1, :]
        scale1 = 1.0 + mod_ref[0, 1:2, :]
        g = g_ref[...]

        def body(r, carry):
            rows = pl.ds(pl.multiple_of(r * NORM_ROWS, NORM_ROWS), NORM_ROWS)
            x = x_ref[rows, :]
            ms = jnp.mean(x * x, axis=-1, keepdims=True)
            y = x * lax.rsqrt(ms + EPS) * g
            h_ref[rows, :] = (y * scale1 + shift).astype(BF16)
            return carry

        lax.fori_loop(0, x_ref.shape[0] // NORM_ROWS, body, 0)

    acc = jnp.dot(h_ref[...], w_ref[...], preferred_element_type=F32)

    @pl.when(j < n_small)
    def _():
        small_ref[...] = acc

    @pl.when((j >= n_small) & (j < n_small + n_qkv))
    def _():
        qkv_ref[...] = acc

    @pl.when(j >= n_small + n_qkv)
    def _():
        gates_ref[...] = acc.astype(BF16)


def _proj_call(x2, mod3, g_pre, w_all, seq):
    rows, d = x2.shape
    tm = min(PROJ_TM, seq)
    tn = PROJ_TN
    n_small = SMALL_W // tn
    n_qkv = 3 * WIDTH_B // tn
    n_gate = (WIDTH_A + WIDTH_B) // tn
    tiles_per_batch = seq // tm
    kern = functools.partial(_proj_kernel, n_small=n_small, n_qkv=n_qkv)
    return pl.pallas_call(
        kern,
        grid=(rows // tm, n_small + n_qkv + n_gate),
        in_specs=[pl.BlockSpec((tm, d), lambda i, j: (i, 0)),
                  pl.BlockSpec((1, 3, d), lambda i, j: (i // tiles_per_batch, 0, 0)),
                  pl.BlockSpec((1, d), lambda i, j: (0, 0)),
                  pl.BlockSpec((d, tn), lambda i, j: (0, j))],
        out_specs=[pl.BlockSpec((tm, tn), lambda i, j: (i, jnp.minimum(j, n_small - 1))),
                   pl.BlockSpec((tm, tn), lambda i, j: (i, jnp.clip(j - n_small, 0, n_qkv - 1))),
                   pl.BlockSpec((tm, tn),
                                lambda i, j: (i, jnp.clip(j - n_small - n_qkv, 0, n_gate - 1)))],
        out_shape=[jax.ShapeDtypeStruct((rows, SMALL_W), F32),
                   jax.ShapeDtypeStruct((rows, 3 * WIDTH_B), F32),
                   jax.ShapeDtypeStruct((rows, WIDTH_A + WIDTH_B), BF16)],
        scratch_shapes=[pltpu.VMEM((tm, d), BF16)],
        compiler_params=_cparams(("parallel", "arbitrary")),
        name="in_proj",
    )(x2, mod3, g_pre, w_all)


def _prep_kernel(small_ref, gq_ref, gkv_ref, wuq_ref, wuqi_ref, wuk_ref,
                 cosq_ref, sinq_ref, cosi_ref, sini_ref, ck_ref, sk_ref, ci_ref, si_ref,
                 qcat_ref, qidx_ref, wt_ref, kcat_ref, kidx_ref, ckvt_ref, *, qscale):
    tq = small_ref.shape[0]
    nblk = tq // QUERY_BLOCK
    cq = small_ref[:, 0:Q_RANK]
    cqn = (cq * lax.rsqrt(jnp.mean(cq * cq, axis=-1, keepdims=True) + EPS)
           * gq_ref[...]).astype(BF16)
    nt = (((1,), (1,)), ((), ()))

    q_t = lax.dot_general(wuq_ref[...], cqn, nt, preferred_element_type=F32)
    cosq = cosq_ref[...]
    sinq = sinq_ref[...]
    half = ROPE_DIM // 2
    zpad = jnp.zeros((KCAT - KV_RANK - ROPE_DIM, tq), F32)
    for h in range(N_HEADS_A):
        base = h * HEAD_DIM
        x1 = q_t[base:base + half]
        x2 = q_t[base + half:base + ROPE_DIM]
        q_rope = jnp.concatenate([x1 * cosq - x2 * sinq, x2 * cosq + x1 * sinq, zpad], axis=0)
        q_lat = jnp.dot(wuk_ref[h], q_t[base:base + HEAD_DIM].astype(BF16),
                        preferred_element_type=F32)
        q_lat = (q_lat * qscale).astype(BF16)
        q_rope = (q_rope * qscale).astype(BF16)
        for blk in range(nblk):
            cols = slice(blk * QUERY_BLOCK, (blk + 1) * QUERY_BLOCK)
            lanes = slice(h * QUERY_BLOCK, (h + 1) * QUERY_BLOCK)
            qcat_ref[blk, 0:KV_RANK, lanes] = q_lat[:, cols]
            qcat_ref[blk, KV_RANK:KCAT, lanes] = q_rope[:, cols]

    qi_t = lax.dot_general(wuqi_ref[...], cqn, nt, preferred_element_type=F32)
    cosi = cosi_ref[...]
    sini = sini_ref[...]
    ihalf = IDX_ROPE // 2
    ipad = jnp.zeros((KIDX_PAD - IDX_DIM, tq), F32)
    for h in range(IDX_HEADS):
        base = h * IDX_DIM
        x1 = qi_t[base:base + ihalf]
        x2 = qi_t[base + ihalf:base + IDX_ROPE]
        qi = jnp.concatenate([x1 * cosi - x2 * sini, x2 * cosi + x1 * sini,
                              qi_t[base + IDX_ROPE:base + IDX_DIM], ipad], axis=0).astype(BF16)
        for blk in range(nblk):
            cols = slice(blk * QUERY_BLOCK, (blk + 1) * QUERY_BLOCK)
            qidx_ref[blk, :, h * QUERY_BLOCK:(h + 1) * QUERY_BLOCK] = qi[:, cols]

    lane = lax.broadcasted_iota(I32, (tq, LANE), 1)

    slab_b = small_ref[:, 896:1024]
    swap_b = jnp.where(lane < ihalf, pltpu.roll(slab_b, LANE - ihalf, 1), pltpu.roll(slab_b, ihalf, 1))
    kidx_ref[...] = (slab_b * ci_ref[...] + swap_b * si_ref[...]).astype(BF16)
    w_t = slab_b.T[IDX_DIM:IDX_DIM + IDX_HEADS] * (IDX_HEADS ** -0.5 * IDX_DIM ** -0.5)
    for blk in range(nblk):
        wt_ref[blk] = w_t[:, blk * QUERY_BLOCK:(blk + 1) * QUERY_BLOCK]

    ckv = small_ref[:, Q_RANK:Q_RANK + KV_RANK]
    cn = ckv * lax.rsqrt(jnp.mean(ckv * ckv, axis=-1, keepdims=True) + EPS) * gkv_ref[...]
    kcat_ref[:, 0:KV_RANK] = cn.astype(BF16)
    ckvt_ref[0] = cn.T.astype(BF16)
    slab_a = small_ref[:, 768:896]
    swap_a = jnp.where(lane < half, pltpu.roll(slab_a, LANE - half, 1), pltpu.roll(slab_a, half, 1))
    kcat_ref[:, KV_RANK:KCAT] = (slab_a * ck_ref[...] + swap_a * sk_ref[...]).astype(BF16)


def _prep_call(small, g_q, g_kv, wuq_t, wuqi_t, wuk_t, tabs, seq):
    rows = small.shape[0]
    tq = PREP_TQ
    nblk = tq // QUERY_BLOCK
    tpb = seq // tq
    nqb = rows // QUERY_BLOCK
    qscale = HEAD_DIM ** -0.5 * LOG2E
    kern = functools.partial(_prep_kernel, qscale=qscale)
    const = lambda t: (0, 0)
    tcol = lambda t: (0, t % tpb)
    trow = lambda t: (t % tpb, 0)
    return pl.pallas_call(
        kern,
        grid=(rows // tq,),
        in_specs=[pl.BlockSpec((tq, SMALL_W), lambda t: (t, 0)),
                  pl.BlockSpec((1, Q_RANK), const),
                  pl.BlockSpec((1, KV_RANK), const),
                  pl.BlockSpec((WIDTH_A, Q_RANK), const),
                  pl.BlockSpec((IDX_HEADS * IDX_DIM, Q_RANK), const),
                  pl.BlockSpec((N_HEADS_A, KV_RANK, HEAD_DIM), lambda t: (0, 0, 0)),
                  pl.BlockSpec((ROPE_DIM // 2, tq), tcol),
                  pl.BlockSpec((ROPE_DIM // 2, tq), tcol),
                  pl.BlockSpec((IDX_ROPE // 2, tq), tcol),
                  pl.BlockSpec((IDX_ROPE // 2, tq), tcol),
                  pl.BlockSpec((tq, LANE), trow),
                  pl.BlockSpec((tq, LANE), trow),
                  pl.BlockSpec((tq, LANE), trow),
                  pl.BlockSpec((tq, LANE), trow)],
        out_specs=[pl.BlockSpec((nblk, KCAT, N_HEADS_A * QUERY_BLOCK), lambda t: (t, 0, 0)),
                   pl.BlockSpec((nblk, KIDX_PAD, IDX_HEADS * QUERY_BLOCK), lambda t: (t, 0, 0)),
                   pl.BlockSpec((nblk, IDX_HEADS, QUERY_BLOCK), lambda t: (t, 0, 0)),
                   pl.BlockSpec((tq, KCAT), lambda t: (t, 0)),
                   pl.BlockSpec((tq, KIDX_PAD), lambda t: (t, 0)),
                   pl.BlockSpec((1, KV_RANK, tq), lambda t: (t, 0, 0))],
        out_shape=[jax.ShapeDtypeStruct((nqb, KCAT, N_HEADS_A * QUERY_BLOCK), BF16),
                   jax.ShapeDtypeStruct((nqb, KIDX_PAD, IDX_HEADS * QUERY_BLOCK), BF16),
                   jax.ShapeDtypeStruct((nqb, IDX_HEADS, QUERY_BLOCK), F32),
                   jax.ShapeDtypeStruct((rows, KCAT), BF16),
                   jax.ShapeDtypeStruct((rows, KIDX_PAD), BF16),
                   jax.ShapeDtypeStruct((rows // tq, KV_RANK, tq), BF16)],
        compiler_params=_cparams(("parallel",)),
        name="dsa_prep",
    )(small, g_q, g_kv, wuq_t, wuqi_t, wuk_t, *tabs)


def _float_of_rank(u):
    key = u ^ INT_MIN
    return lax.bitcast_convert_type(key ^ (lax.shift_right_arithmetic(key, 31) & 0x7FFFFFFF), F32)


_ABOVE_NEG = float(np.nextafter(np.float32(NEG), np.float32(0.0)))
TRIM_PASSES = 8


def _dsa_kernel(kidx_ref, kcat_ref, ckvt_ref, qcat_ref, qidx_ref, wt_ref, gate_ref, wuv_ref,
                o_ref, isc_ref, m_ref, l_ref, acc_ref, s0_ref, s1_ref, *, k_sel):
    i = pl.program_id(1)
    kc = KEY_CHUNK
    qb = QUERY_BLOCK
    per = kc // qb
    nch = (i + per) // per
    qpos = i * qb + lax.broadcasted_iota(I32, (kc, qb), 1)
    krow = lax.broadcasted_iota(I32, (kc, qb), 0)

    def chunk_rows(c):
        return pl.ds(pl.multiple_of(c * kc, kc), kc)

    def fold(op, x):
        return op(x.reshape(kc // 8, 8, qb), axis=0)

    def idx_body(c, carry):
        k0 = pl.multiple_of(c * kc, kc)
        kblk = kidx_ref[pl.ds(k0, kc), :]
        acc = jnp.zeros((kc, qb), F32)
        group = 4
        for g in range(IDX_HEADS // group):
            lg = jnp.dot(kblk, qidx_ref[0, :, g * group * qb:(g + 1) * group * qb],
                         preferred_element_type=F32)
            for hh in range(group):
                h = g * group + hh
                acc = acc + jnp.maximum(lg[:, hh * qb:(hh + 1) * qb], 0.0) * wt_ref[0, h:h + 1, :]
        causal = (k0 + krow) <= qpos
        isc_ref[pl.ds(k0, kc), :] = jnp.where(causal, acc, NEG)
        return carry

    lax.fori_loop(0, nch, idx_body, 0)

    def count_ge(t):
        def cnt_body(c, cnt8):
            sel = jnp.where(isc_ref[chunk_rows(c), :] >= t, 1, 0).astype(I32)
            return cnt8 + fold(jnp.sum, sel)

        cnt8 = lax.fori_loop(0, nch, cnt_body, jnp.zeros((8, qb), I32))
        return jnp.sum(cnt8, axis=0, keepdims=True)

    def bis_body(s, carry):
        tu, cnt_t = carry
        cand_u = tu | lax.shift_left(jnp.int32(1), 31 - s)
        cnt = count_ge(_float_of_rank(cand_u))
        ok = cnt >= k_sel
        return jnp.where(ok, cand_u, tu), jnp.where(ok, cnt, cnt_t)

    tu, cnt_t = lax.fori_loop(0, 32, bis_body,
                              (jnp.zeros((1, qb), I32), jnp.full((1, qb), k_sel, I32)))
    thr = jnp.where(tu == 0, -jnp.inf, _float_of_rank(tu))
    few = thr < _ABOVE_NEG
    thr = jnp.where(few, _ABOVE_NEG, thr)
    cnt_t = jnp.where(few, k_sel, cnt_t)

    def trim_cond(state):
        _, cnt, n = state
        return (jnp.max(cnt) > k_sel) & (n < TRIM_PASSES)

    def trim_body(state):
        low, cnt, n = state

        def min_body(c, mn8):
            x = isc_ref[chunk_rows(c), :]
            return jnp.minimum(mn8, fold(jnp.min, jnp.where(x >= thr, jnp.where(x > low, x, jnp.inf), jnp.inf)))

        mn = jnp.min(lax.fori_loop(0, nch, min_body, jnp.full((8, qb), jnp.inf, F32)),
                     axis=0, keepdims=True)

        def eq_body(c, cnt8):
            return cnt8 + fold(jnp.sum, jnp.where(isc_ref[chunk_rows(c), :] == mn, 1, 0).astype(I32))

        n_eq = jnp.sum(lax.fori_loop(0, nch, eq_body, jnp.zeros((8, qb), I32)), axis=0, keepdims=True)
        can = (cnt > k_sel) & (cnt - n_eq >= k_sel)
        stop = jnp.where(jnp.max(jnp.where(can, 1, 0)) > 0, n + 1, TRIM_PASSES)
        return jnp.where(can, mn, low), jnp.where(can, cnt - n_eq, cnt), stop

    low, _, _ = lax.while_loop(trim_cond, trim_body,
                               (jnp.full((1, qb), -jnp.inf, F32), cnt_t, jnp.int32(0)))

    m_ref[...] = jnp.full(m_ref.shape, NEG, F32)
    l_ref[...] = jnp.zeros(l_ref.shape, F32)
    acc_ref[...] = jnp.zeros(acc_ref.shape, F32)
    pair = 2 * qb

    def scores(c, dst):
        k0 = pl.multiple_of(c * kc, kc)
        dst[...] = jnp.dot(kcat_ref[pl.ds(k0, kc), :], qcat_ref[0], preferred_element_type=F32)

    def att_step(c, cur, nxt):
        if nxt is not None:
            scores(c + 1, nxt)
        k0 = pl.multiple_of(c * kc, kc)
        x = isc_ref[pl.ds(k0, kc), :]
        bias = jnp.where(x >= thr, jnp.where(x > low, 0.0, NEG), NEG).astype(F32)
        s = cur[...] + jnp.concatenate([bias] * N_HEADS_A, axis=1)
        m_old = m_ref[...]
        m_new = jnp.maximum(m_old, jnp.max(s, axis=0, keepdims=True))
        alpha = jnp.exp2(m_old - m_new)
        p = jnp.exp2(s - m_new)
        l_ref[...] = alpha * l_ref[...] + jnp.sum(p, axis=0, keepdims=True)
        acc_ref[...] = alpha * acc_ref[...] + jnp.dot(ckvt_ref[c], p.astype(BF16),
                                                      preferred_element_type=F32)
        m_ref[...] = m_new

    scores(0, s0_ref)
    npair = (nch - 1) // 2

    def att_body(j, carry):
        att_step(2 * j, s0_ref, s1_ref)
        att_step(2 * j + 1, s1_ref, s0_ref)
        return carry

    lax.fori_loop(0, npair, att_body, 0)

    @pl.when(nch % 2 == 0)
    def _():
        att_step(nch - 2, s0_ref, s1_ref)
        att_step(nch - 1, s1_ref, None)

    @pl.when(nch % 2 == 1)
    def _():
        att_step(nch - 1, s0_ref, None)

    inv_l = 1.0 / l_ref[...]
    outs = []
    for h in range(N_HEADS_A):
        cols = slice(h * qb, (h + 1) * qb)
        o_t = acc_ref[:, cols] * inv_l[:, cols]
        outs.append(jnp.dot(o_t.T.astype(BF16), wuv_ref[h], preferred_element_type=F32))
    o = jnp.concatenate(outs, axis=1)
    g = gate_ref[...].astype(F32)
    o_ref[...] = (o * _silu(g)).astype(BF16)


def _dsa_call(kidx, kcat, ckvt, qcat, qidx, w_t, gates, wuv, batch, seq):
    rows = kidx.shape[0]
    nb = seq // QUERY_BLOCK
    nkc = seq // KEY_CHUNK
    k_sel = min(TOPK_MAX, seq // 4)
    kern = functools.partial(_dsa_kernel, k_sel=k_sel)
    return pl.pallas_call(
        kern,
        grid=(batch, nb),
        in_specs=[pl.BlockSpec((seq, KIDX_PAD), lambda b, i: (b, 0)),
                  pl.BlockSpec((seq, KCAT), lambda b, i: (b, 0)),
                  pl.BlockSpec((nkc, KV_RANK, KEY_CHUNK), lambda b, i: (b, 0, 0)),
                  pl.BlockSpec((1, KCAT, N_HEADS_A * QUERY_BLOCK), lambda b, i: (b * nb + i, 0, 0)),
                  pl.BlockSpec((1, KIDX_PAD, IDX_HEADS * QUERY_BLOCK), lambda b, i: (b * nb + i, 0, 0)),
                  pl.BlockSpec((1, IDX_HEADS, QUERY_BLOCK), lambda b, i: (b * nb + i, 0, 0)),
                  pl.BlockSpec((QUERY_BLOCK, WIDTH_A), lambda b, i: (b * nb + i, 0)),
                  pl.BlockSpec((N_HEADS_A, KV_RANK, HEAD_DIM), lambda b, i: (0, 0, 0))],
        out_specs=pl.BlockSpec((QUERY_BLOCK, WIDTH_A), lambda b, i: (b * nb + i, 0)),
        out_shape=jax.ShapeDtypeStruct((rows, WIDTH_A), BF16),
        scratch_shapes=[pltpu.VMEM((seq, QUERY_BLOCK), F32),
                        pltpu.VMEM((1, N_HEADS_A * QUERY_BLOCK), F32),
                        pltpu.VMEM((1, N_HEADS_A * QUERY_BLOCK), F32),
                        pltpu.VMEM((KV_RANK, N_HEADS_A * QUERY_BLOCK), F32),
                        pltpu.VMEM((KEY_CHUNK, N_HEADS_A * QUERY_BLOCK), F32),
                        pltpu.VMEM((KEY_CHUNK, N_HEADS_A * QUERY_BLOCK), F32)],
        compiler_params=_cparams(("parallel", "arbitrary")),
        name="dsa_attn",
    )(kidx, kcat, ckvt, qcat, qidx, w_t, gates, wuv)


def _dil_kernel(q_ref, k_ref, v_ref, gate_ref, c_ref, s_ref, o_ref,
                nat_ref, a4_ref, qn, kn, vn, q4, k4, v4, q16, k16, v16,
                oc0, ls0, oc1, ls1, oc2, ls2, *, qscale, unroll):
    seq = q_ref.shape[0]
    half = ROPE_DIM // 2
    blk = N_BACK
    rc = ROW_CHUNK
    m4 = seq // 4
    m16 = seq // 16
    lane = lax.broadcasted_iota(I32, (rc, LANE), 1)

    def build(src_ref, dn, d4, d16, rope_scale):
        def nat_body(c, carry):
            rows = pl.ds(pl.multiple_of(c * rc, rc), rc)
            x = src_ref[rows, :]
            if rope_scale is not None:
                sw = jnp.where(lane < half, pltpu.roll(x, LANE - half, 1), pltpu.roll(x, half, 1))
                x = x * c_ref[rows, :] + sw * s_ref[rows, :]
                if rope_scale != 1.0:
                    x = x * rope_scale
                nat_ref[rows, :] = x
            dn[rows, :] = x.astype(BF16)
            return carry

        lax.fori_loop(0, seq // rc, nat_body, 0)
        nat = src_ref if rope_scale is None else nat_ref

        def a4_body(c, carry):
            r4 = c // (m4 // rc)
            j0 = (c % (m4 // rc)) * rc
            x = nat[pl.ds(r4 + 4 * j0, rc, stride=4), :]
            rows = pl.ds(pl.multiple_of(c * rc, rc), rc)
            a4_ref[rows, :] = x
            d4[rows, :] = x.astype(BF16)
            return carry

        lax.fori_loop(0, seq // rc, a4_body, 0)

        def a16_body(r16, carry):
            r4 = r16 % 4
            s = r16 // 4
            x = a4_ref[pl.ds(r4 * m4 + s, m16, stride=4), :]
            d16[pl.ds(pl.multiple_of(r16 * m16, m16), m16), :] = x.astype(BF16)
            return carry

        lax.fori_loop(0, 16, a16_body, 0)

    build(q_ref, qn, q4, q16, qscale)
    build(k_ref, kn, k4, k16, 1.0)
    build(v_ref, vn, v4, v16, None)

    qi = lax.broadcasted_iota(I32, (blk, 2 * blk), 0) + blk
    kj = lax.broadcasted_iota(I32, (blk, 2 * blk), 1)
    rel = qi - kj
    bias_band = jnp.where((rel >= 0) & (rel <= N_BACK), 0.0, NEG).astype(F32)
    no_prev = jnp.where(kj < blk, NEG, 0.0).astype(F32)
    qi1 = lax.broadcasted_iota(I32, (blk, blk), 0)
    kj1 = lax.broadcasted_iota(I32, (blk, blk), 1)
    bias_first = jnp.where(kj1 <= qi1, 0.0, NEG).astype(F32)
    nt = (((1,), (1,)), ((), ()))

    def run_config(qs, ks, vs, nmb, store):
        def rows_of(u, first):
            cur = pl.ds(pl.multiple_of(u * blk, blk), blk)
            if first is True:
                return (cur,)
            return (pl.ds(pl.multiple_of(jnp.maximum(u * blk - blk, 0), blk), blk), cur)

        def gather(ref, rows):
            parts = [ref[r, :] for r in rows]
            return parts[0] if len(parts) == 1 else jnp.concatenate(parts, axis=0)

        def loop_body(g, carry):
            units = []
            for t in range(unroll):
                if nmb <= unroll:
                    first = (t % nmb == 0)
                else:
                    first = jnp.where(g == 0, 1.0, 0.0).astype(F32) if t == 0 else False
                units.append((g * unroll + t, first))
            scores = []
            for u, first in units:
                rows = rows_of(u, first)
                if first is True:
                    bias = bias_first
                elif first is False:
                    bias = bias_band
                else:
                    bias = bias_band + no_prev * first
                scores.append(lax.dot_general(qs[rows[-1], :], gather(ks, rows), nt,
                                              preferred_element_type=F32) + bias)
            probs = []
            for s in scores:
                m = jnp.max(s, axis=-1, keepdims=True)
                p = jnp.exp2(s - m)
                l = jnp.sum(p, axis=-1, keepdims=True)
                probs.append((p.astype(BF16), 1.0 / l, jnp.broadcast_to(m + jnp.log2(l), (blk, LANE))))
            for (u, first), (p, inv_l, lse) in zip(units, probs):
                o = jnp.dot(p, gather(vs, rows_of(u, first)), preferred_element_type=F32) * inv_l
                store(u, o, lse)
            return carry

        lax.fori_loop(0, seq // blk // unroll, loop_body, 0)

    def store_contig(oc, ls):
        def store(u, o, lse):
            rows = pl.ds(pl.multiple_of(u * blk, blk), blk)
            oc[rows, :] = o
            ls[rows, :] = lse
        return store

    def store_a16_to_a4(u, o, lse):
        r16 = u // (m16 // blk)
        mb = u % (m16 // blk)
        rows = pl.ds((r16 % 4) * m4 + 4 * blk * mb + r16 // 4, blk, stride=4)
        oc2[rows, :] = o
        ls2[rows, :] = lse

    run_config(qn, kn, vn, seq // blk, store_contig(oc0, ls0))
    run_config(q4, k4, v4, m4 // blk, store_contig(oc1, ls1))
    run_config(q16, k16, v16, m16 // blk, store_a16_to_a4)

    def merge_body(c, carry):
        r4 = c // (m4 // rc)
        j0 = (c % (m4 // rc)) * rc
        nat_rows = pl.ds(r4 + 4 * j0, rc, stride=4)
        rows = pl.ds(pl.multiple_of(c * rc, rc), rc)
        l0 = ls0[nat_rows, :]
        l1 = ls1[rows, :]
        l2 = ls2[rows, :]
        mx = jnp.maximum(jnp.maximum(l0, l1), l2)
        e0 = jnp.exp2(l0 - mx)
        e1 = jnp.exp2(l1 - mx)
        e2 = jnp.exp2(l2 - mx)
        nat_ref[nat_rows, :] = (e0 * oc0[nat_rows, :] + e1 * oc1[rows, :] + e2 * oc2[rows, :]) / (e0 + e1 + e2)
        return carry

    lax.fori_loop(0, seq // rc, merge_body, 0)

    def gate_body(c, carry):
        rows = pl.ds(pl.multiple_of(c * rc, rc), rc)
        g = gate_ref[rows, :].astype(F32)
        o_ref[rows, :] = (nat_ref[rows, :] * _silu(g)).astype(BF16)
        return carry

    lax.fori_loop(0, seq // rc, gate_body, 0)


def _dil_call(qkv, gates, ctab, stab, batch, seq):
    rows = qkv.shape[0]
    nh = N_HEADS_B
    qscale = HEAD_DIM ** -0.5 * LOG2E
    kern = functools.partial(_dil_kernel, qscale=qscale, unroll=16)
    blk = (seq, HEAD_DIM)
    return pl.pallas_call(
        kern,
        grid=(batch, nh),
        in_specs=[pl.BlockSpec(blk, lambda b, h: (b, h)),
                  pl.BlockSpec(blk, lambda b, h: (b, nh + h)),
                  pl.BlockSpec(blk, lambda b, h: (b, 2 * nh + h)),
                  pl.BlockSpec(blk, lambda b, h: (b, nh + h)),
                  pl.BlockSpec(blk, lambda b, h: (0, 0), pipeline_mode=pl.Buffered(1)),
                  pl.BlockSpec(blk, lambda b, h: (0, 0), pipeline_mode=pl.Buffered(1))],
        out_specs=pl.BlockSpec(blk, lambda b, h: (b, h)),
        out_shape=jax.ShapeDtypeStruct((rows, WIDTH_B), BF16),
        scratch_shapes=([pltpu.VMEM(blk, F32)] * 2 + [pltpu.VMEM(blk, BF16)] * 9
                        + [pltpu.VMEM(blk, F32)] * 6),
        compiler_params=_cparams(("parallel", "arbitrary")),
        name="dilated",
    )(qkv, qkv, qkv, gates, ctab, stab)


def _out_kernel(oa_ref, ob_ref, w_ref, x_ref, mod_ref, g_ref, o_ref):
    y = jnp.dot(oa_ref[...], w_ref[0:WIDTH_A, :], preferred_element_type=F32)
    y = y + jnp.dot(ob_ref[...], w_ref[WIDTH_A:WIDTH_A + WIDTH_B, :], preferred_element_type=F32)
    yn = y * lax.rsqrt(jnp.mean(y * y, axis=-1, keepdims=True) + EPS) * g_ref[...]
    o_ref[...] = x_ref[...] + mod_ref[0, 2:3, :] * yn


def _out_call(o_a, o_b, w_out, x2, mod3, g_post, seq):
    rows, d = x2.shape
    tm = OUT_TM
    tpb = seq // tm
    return pl.pallas_call(
        _out_kernel,
        grid=(rows // tm,),
        in_specs=[pl.BlockSpec((tm, WIDTH_A), lambda i: (i, 0)),
                  pl.BlockSpec((tm, WIDTH_B), lambda i: (i, 0)),
                  pl.BlockSpec((WIDTH_A + WIDTH_B, d), lambda i: (0, 0)),
                  pl.BlockSpec((tm, d), lambda i: (i, 0)),
                  pl.BlockSpec((1, 3, d), lambda i: (i // tpb, 0, 0)),
                  pl.BlockSpec((1, d), lambda i: (0, 0))],
        out_specs=pl.BlockSpec((tm, d), lambda i: (i, 0)),
        out_shape=jax.ShapeDtypeStruct((rows, d), F32),
        compiler_params=_cparams(("parallel",)),
        name="out_proj",
    )(o_a, o_b, w_out, x2, mod3, g_post)


def _rope_tables(seq):
    pos = jnp.arange(seq, dtype=F32)

    def cs(n_rot):
        inv = ROPE_THETA ** (-jnp.arange(0, n_rot, 2, dtype=F32) / n_rot)
        ang = pos[:, None] * inv[None, :]
        return jnp.cos(ang), jnp.sin(ang)

    c32, s32 = cs(ROPE_DIM)
    c16, s16 = cs(IDX_ROPE)
    ck = jnp.concatenate([c32, c32, jnp.ones((seq, LANE - ROPE_DIM), F32)], axis=1)
    sk = jnp.concatenate([-s32, s32, jnp.zeros((seq, LANE - ROPE_DIM), F32)], axis=1)
    ci = jnp.concatenate([c16, c16, jnp.ones((seq, LANE - IDX_ROPE), F32)], axis=1)
    si = jnp.concatenate([-s16, s16, jnp.zeros((seq, LANE - IDX_ROPE), F32)], axis=1)
    return (c32.T, s32.T, c16.T, s16.T, ck, sk, ci, si)


def _layer(x, c, w_ada, b_ada, g_pre, g_post, w_in, g_q, g_kv, w_uq, w_uq_idx, w_uk, w_uv, w_out):
    batch, seq, d = x.shape
    assert seq % PROJ_TM == 0 or seq < PROJ_TM
    assert seq % KEY_CHUNK == 0 and seq // 16 >= N_BACK
    assert all(w // dl == N_BACK for w, dl in DILATED_CONFIGS)
    rows = batch * seq
    x2 = x.reshape(rows, d)

    o_cq, o_ckv, o_kr, o_ki, o_wi = 0, 512, 768, 800, 864
    o_ga, o_q, o_k, o_v, o_gb = 880, 1904, 2928, 3952, 4976
    zeros = lambda n: jnp.zeros((d, n), w_in.dtype)
    w_all = jnp.concatenate([
        w_in[:, o_cq:o_ki], zeros(896 - 800), w_in[:, o_ki:o_ga], zeros(SMALL_W - 976),
        w_in[:, o_q:o_gb], w_in[:, o_ga:o_q], w_in[:, o_gb:o_gb + WIDTH_B]], axis=1).astype(BF16)
    assert w_all.shape[1] == IN_PAD

    c_pad = jnp.zeros((PACK16, d), F32).at[:batch].set(c)
    mod = _mod_call(c_pad, w_ada, b_ada.reshape(1, -1))[:batch]
    mod3 = mod.reshape(batch, 3, d)

    small, qkv, gates = _proj_call(x2, mod3, g_pre.reshape(1, d), w_all, seq)

    tabs = _rope_tables(seq)
    wuk_t = jnp.transpose(jnp.pad(w_uk, ((0, 0), (ROPE_DIM, 0), (0, 0))), (0, 2, 1)).astype(BF16)
    qcat, qidx, w_t, kcat, kidx, ckvt = _prep_call(
        small, g_q.reshape(1, -1), g_kv.reshape(1, -1), w_uq.T.astype(BF16),
        w_uq_idx.T.astype(BF16), wuk_t, tabs, seq)

    o_a = _dsa_call(kidx, kcat, ckvt, qcat, qidx, w_t, gates, w_uv.astype(BF16), batch, seq)
    o_b = _dil_call(qkv, gates, tabs[4], tabs[5], batch, seq)
    out = _out_call(o_a, o_b, w_out.astype(BF16), x2, mod3, g_post.reshape(1, d), seq)
    return out.reshape(batch, seq, d)


def kernel(x, c, w_ada, b_ada, g_pre, g_post, w_in, g_q, g_kv, w_uq, w_uq_idx, w_uk, w_uv, w_out):
    for layer in range(w_ada.shape[0]):
        x = _layer(x, c, w_ada[layer], b_ada[layer], g_pre[layer], g_post[layer], w_in[layer],
                   g_q[layer], g_kv[layer], w_uq[layer], w_uq_idx[layer], w_uk[layer],
                   w_uv[layer], w_out[layer])
    return x
```

```python
import functools
import math

import numpy as np
import jax
import jax.numpy as jnp
from jax import lax
from jax.experimental import pallas as pl
from jax.experimental.pallas import tpu as pltpu

F32 = jnp.float32
BF16 = jnp.bfloat16
I32 = jnp.int32

HEAD_DIM = 128
ROPE_DIM = HEAD_DIM // 4
ROPE_THETA = 500000.0
EPS = 1e-6
NEG = -1e30
N_HEADS_A = 8
WIDTH_A = N_HEADS_A * HEAD_DIM
Q_RANK = 512
KV_RANK = 256
IDX_HEADS = 16
IDX_DIM = 64
IDX_ROPE = IDX_DIM // 4
TOPK_MAX = 256
QUERY_BLOCK = 128
N_HEADS_B = 8
WIDTH_B = N_HEADS_B * HEAD_DIM
DILATED_CONFIGS = ((128, 1), (512, 4), (2048, 16))
N_BACK = 128
SMALL_W = 1024
IN_PAD = SMALL_W + 5 * 1024
KCAT = 384
KIDX_PAD = 128
INT_MIN = -(2 ** 31)
LOG2E = 1.4426950408889634

LANE = 128
PACK16 = 16
VMEM_LIMIT = 56 * 1024 * 1024
PROJ_TM = 1024
PROJ_TN = 512
PREP_TQ = 512
KEY_CHUNK = 512
OUT_TM = 512
ROW_CHUNK = 512
NORM_ROWS = 128


def _cparams(sem):
    return pltpu.CompilerParams(dimension_semantics=sem, vmem_limit_bytes=VMEM_LIMIT)


def _silu(g):
    return g * jax.nn.sigmoid(g)


def _split_bf16(x):
    hi = x.astype(BF16)
    return hi, (x - hi.astype(F32)).astype(BF16)


def _mod_kernel(c_ref, w_ref, b_ref, o_ref):
    a_hi, a_lo = _split_bf16(_silu(c_ref[...]))
    w_hi, w_lo = _split_bf16(w_ref[...])
    dot = functools.partial(jnp.dot, preferred_element_type=F32)
    o_ref[...] = dot(a_hi, w_hi) + (dot(a_hi, w_lo) + dot(a_lo, w_hi)) + b_ref[...]


def _mod_call(c_pad, w_ada, b_ada):
    rows, d = c_pad.shape
    n = w_ada.shape[1]
    tn = 1024
    return pl.pallas_call(
        _mod_kernel,
        grid=(n // tn,),
        in_specs=[pl.BlockSpec((rows, d), lambda j: (0, 0)),
                  pl.BlockSpec((d, tn), lambda j: (0, j)),
                  pl.BlockSpec((1, tn), lambda j: (0, j))],
        out_specs=pl.BlockSpec((rows, tn), lambda j: (0, j)),
        out_shape=jax.ShapeDtypeStruct((rows, n), F32),
        compiler_params=_cparams(("arbitrary",)),
        name="mod",
    )(c_pad, w_ada, b_ada)


def _proj_kernel(x_ref, mod_ref, g_ref, w_ref, small_ref, qkv_ref, gates_ref, h_ref, *,
                 n_small, n_qkv):
    j = pl.program_id(1)

    @pl.when(j == 0)
    def _():
        shift = mod_ref[0, 0:1, :]
        scale1 = 1.0 + mod_ref[0, 1:2, :]
        g = g_ref[...]

        def body(r, carry):
            rows = pl.ds(pl.multiple_of(r * NORM_ROWS, NORM_ROWS), NORM_ROWS)
            x = x_ref[rows, :]
            ms = jnp.mean(x * x, axis=-1, keepdims=True)
            y = x * lax.rsqrt(ms + EPS) * g
            h_ref[rows, :] = (y * scale1 + shift).astype(BF16)
            return carry

        lax.fori_loop(0, x_ref.shape[0] // NORM_ROWS, body, 0)

    def project():
        return lax.dot_general(h_ref[...], w_ref[...], (((1,), (1,)), ((), ())),
                               preferred_element_type=F32)

    @pl.when(j < n_small)
    def _():
        small_ref[...] = project()

    @pl.when((j >= n_small) & (j < n_small + n_qkv))
    def _():
        qkv_ref[...] = project()

    @pl.when(j >= n_small + n_qkv)
    def _():
        gates_ref[...] = project().astype(BF16)


def _proj_call(x2, mod3, g_pre, w_all, seq):
    rows, d = x2.shape
    tm = min(PROJ_TM, seq)
    tn = PROJ_TN
    n_small = SMALL_W // tn
    n_qkv = 3 * WIDTH_B // tn
    n_gate = (WIDTH_A + WIDTH_B) // tn
    tiles_per_batch = seq // tm
    kern = functools.partial(_proj_kernel, n_small=n_small, n_qkv=n_qkv)
    return pl.pallas_call(
        kern,
        grid=(rows // tm, n_small + n_qkv + n_gate),
        in_specs=[pl.BlockSpec((tm, d), lambda i, j: (i, 0)),
                  pl.BlockSpec((1, 3, d), lambda i, j: (i // tiles_per_batch, 0, 0)),
                  pl.BlockSpec((1, d), lambda i, j: (0, 0)),
                  pl.BlockSpec((tn, d), lambda i, j: (j, 0))],
        out_specs=[pl.BlockSpec((tm, tn), lambda i, j: (i, jnp.minimum(j, n_small - 1))),
                   pl.BlockSpec((tm, tn), lambda i, j: (i, jnp.clip(j - n_small, 0, n_qkv - 1))),
                   pl.BlockSpec((tm, tn),
                                lambda i, j: (i, jnp.clip(j - n_small - n_qkv, 0, n_gate - 1)))],
        out_shape=[jax.ShapeDtypeStruct((rows, SMALL_W), F32),
                   jax.ShapeDtypeStruct((rows, 3 * WIDTH_B), F32),
                   jax.ShapeDtypeStruct((rows, WIDTH_A + WIDTH_B), BF16)],
        scratch_shapes=[pltpu.VMEM((tm, d), BF16)],
        compiler_params=_cparams(("parallel", "arbitrary")),
        name="in_proj",
    )(x2, mod3, g_pre, w_all)


def _prep_kernel(small_ref, gq_ref, gkv_ref, wuq_ref, wuqi_ref, wuk_ref,
                 cosq_ref, sinq_ref, cosi_ref, sini_ref, ck_ref, sk_ref, ci_ref, si_ref,
                 qcat_ref, qidx_ref, wt_ref, kcat_ref, kidx_ref, ckvt_ref, *, qscale):
    tq = small_ref.shape[0]
    nblk = tq // QUERY_BLOCK
    cq = small_ref[:, 0:Q_RANK]
    cqn = (cq * lax.rsqrt(jnp.mean(cq * cq, axis=-1, keepdims=True) + EPS)
           * gq_ref[...]).astype(BF16)
    nt = (((1,), (1,)), ((), ()))

    q_t = lax.dot_general(wuq_ref[...], cqn, nt, preferred_element_type=F32)
    cosq = cosq_ref[...]
    sinq = sinq_ref[...]
    half = ROPE_DIM // 2
    zpad = jnp.zeros((KCAT - KV_RANK - ROPE_DIM, tq), F32)
    for h in range(N_HEADS_A):
        base = h * HEAD_DIM
        x1 = q_t[base:base + half]
        x2 = q_t[base + half:base + ROPE_DIM]
        q_rope = jnp.concatenate([x1 * cosq - x2 * sinq, x2 * cosq + x1 * sinq, zpad], axis=0)
        q_lat = jnp.dot(wuk_ref[h], q_t[base:base + HEAD_DIM].astype(BF16),
                        preferred_element_type=F32)
        q_lat = (q_lat * qscale).astype(BF16)
        q_rope = (q_rope * qscale).astype(BF16)
        for blk in range(nblk):
            cols = slice(blk * QUERY_BLOCK, (blk + 1) * QUERY_BLOCK)
            lanes = slice(h * QUERY_BLOCK, (h + 1) * QUERY_BLOCK)
            qcat_ref[blk, 0:KV_RANK, lanes] = q_lat[:, cols]
            qcat_ref[blk, KV_RANK:KCAT, lanes] = q_rope[:, cols]

    qi_t = lax.dot_general(wuqi_ref[...], cqn, nt, preferred_element_type=F32)
    cosi = cosi_ref[...]
    sini = sini_ref[...]
    ihalf = IDX_ROPE // 2
    ipad = jnp.zeros((KIDX_PAD - IDX_DIM, tq), F32)
    for h in range(IDX_HEADS):
        base = h * IDX_DIM
        x1 = qi_t[base:base + ihalf]
        x2 = qi_t[base + ihalf:base + IDX_ROPE]
        qi = jnp.concatenate([x1 * cosi - x2 * sini, x2 * cosi + x1 * sini,
                              qi_t[base + IDX_ROPE:base + IDX_DIM], ipad], axis=0).astype(BF16)
        for blk in range(nblk):
            cols = slice(blk * QUERY_BLOCK, (blk + 1) * QUERY_BLOCK)
            qidx_ref[blk, :, h * QUERY_BLOCK:(h + 1) * QUERY_BLOCK] = qi[:, cols]

    lane = lax.broadcasted_iota(I32, (tq, LANE), 1)

    slab_b = small_ref[:, 896:1024]
    swap_b = jnp.where(lane < ihalf, pltpu.roll(slab_b, LANE - ihalf, 1), pltpu.roll(slab_b, ihalf, 1))
    kidx_ref[...] = (slab_b * ci_ref[...] + swap_b * si_ref[...]).astype(BF16)
    w_t = slab_b.T[IDX_DIM:IDX_DIM + IDX_HEADS] * (IDX_HEADS ** -0.5 * IDX_DIM ** -0.5)
    for blk in range(nblk):
        wt_ref[blk] = w_t[:, blk * QUERY_BLOCK:(blk + 1) * QUERY_BLOCK]

    ckv = small_ref[:, Q_RANK:Q_RANK + KV_RANK]
    cn = ckv * lax.rsqrt(jnp.mean(ckv * ckv, axis=-1, keepdims=True) + EPS) * gkv_ref[...]
    kcat_ref[:, 0:KV_RANK] = cn.astype(BF16)
    ckvt_ref[0] = cn.T.astype(BF16)
    slab_a = small_ref[:, 768:896]
    swap_a = jnp.where(lane < half, pltpu.roll(slab_a, LANE - half, 1), pltpu.roll(slab_a, half, 1))
    kcat_ref[:, KV_RANK:KCAT] = (slab_a * ck_ref[...] + swap_a * sk_ref[...]).astype(BF16)


def _prep_call(small, g_q, g_kv, wuq_t, wuqi_t, wuk_t, tabs, seq):
    rows = small.shape[0]
    tq = PREP_TQ
    nblk = tq // QUERY_BLOCK
    tpb = seq // tq
    nqb = rows // QUERY_BLOCK
    qscale = HEAD_DIM ** -0.5 * LOG2E
    kern = functools.partial(_prep_kernel, qscale=qscale)
    const = lambda t: (0, 0)
    tcol = lambda t: (0, t % tpb)
    trow = lambda t: (t % tpb, 0)
    return pl.pallas_call(
        kern,
        grid=(rows // tq,),
        in_specs=[pl.BlockSpec((tq, SMALL_W), lambda t: (t, 0)),
                  pl.BlockSpec((1, Q_RANK), const),
                  pl.BlockSpec((1, KV_RANK), const),
                  pl.BlockSpec((WIDTH_A, Q_RANK), const),
                  pl.BlockSpec((IDX_HEADS * IDX_DIM, Q_RANK), const),
                  pl.BlockSpec((N_HEADS_A, KV_RANK, HEAD_DIM), lambda t: (0, 0, 0)),
                  pl.BlockSpec((ROPE_DIM // 2, tq), tcol),
                  pl.BlockSpec((ROPE_DIM // 2, tq), tcol),
                  pl.BlockSpec((IDX_ROPE // 2, tq), tcol),
                  pl.BlockSpec((IDX_ROPE // 2, tq), tcol),
                  pl.BlockSpec((tq, LANE), trow),
                  pl.BlockSpec((tq, LANE), trow),
                  pl.BlockSpec((tq, LANE), trow),
                  pl.BlockSpec((tq, LANE), trow)],
        out_specs=[pl.BlockSpec((nblk, KCAT, N_HEADS_A * QUERY_BLOCK), lambda t: (t, 0, 0)),
                   pl.BlockSpec((nblk, KIDX_PAD, IDX_HEADS * QUERY_BLOCK), lambda t: (t, 0, 0)),
                   pl.BlockSpec((nblk, IDX_HEADS, QUERY_BLOCK), lambda t: (t, 0, 0)),
                   pl.BlockSpec((tq, KCAT), lambda t: (t, 0)),
                   pl.BlockSpec((tq, KIDX_PAD), lambda t: (t, 0)),
                   pl.BlockSpec((1, KV_RANK, tq), lambda t: (t, 0, 0))],
        out_shape=[jax.ShapeDtypeStruct((nqb, KCAT, N_HEADS_A * QUERY_BLOCK), BF16),
                   jax.ShapeDtypeStruct((nqb, KIDX_PAD, IDX_HEADS * QUERY_BLOCK), BF16),
                   jax.ShapeDtypeStruct((nqb, IDX_HEADS, QUERY_BLOCK), F32),
                   jax.ShapeDtypeStruct((rows, KCAT), BF16),
                   jax.ShapeDtypeStruct((rows, KIDX_PAD), BF16),
                   jax.ShapeDtypeStruct((rows // tq, KV_RANK, tq), BF16)],
        compiler_params=_cparams(("parallel",)),
        name="dsa_prep",
    )(small, g_q, g_kv, wuq_t, wuqi_t, wuk_t, *tabs)


def _float_of_rank(u):
    key = u ^ INT_MIN
    return lax.bitcast_convert_type(key ^ (lax.shift_right_arithmetic(key, 31) & 0x7FFFFFFF), F32)


_ABOVE_NEG = float(np.nextafter(np.float32(NEG), np.float32(0.0)))
TRIM_PASSES = 8
SELECT_BITS = 24


def _dsa_kernel(kidx_ref, kcat_ref, ckvt_ref, qcat_ref, qidx_ref, wt_ref, gate_ref, wuv_ref,
                o_ref, isc_ref, m_ref, l_ref, acc_ref, s0_ref, s1_ref, *, k_sel):
    i = pl.program_id(1)
    kc = KEY_CHUNK
    qb = QUERY_BLOCK
    per = kc // qb
    nch = (i + per) // per
    qpos = i * qb + lax.broadcasted_iota(I32, (kc, qb), 1)
    krow = lax.broadcasted_iota(I32, (kc, qb), 0)

    def chunk_rows(c):
        return pl.ds(pl.multiple_of(c * kc, kc), kc)

    def fold(op, x):
        return op(x.reshape(kc // 8, 8, qb), axis=0)

    def idx_body(c, carry):
        k0 = pl.multiple_of(c * kc, kc)
        kblk = kidx_ref[pl.ds(k0, kc), :]
        acc = jnp.zeros((kc, qb), F32)
        group = 4
        for g in range(IDX_HEADS // group):
            lg = jnp.dot(kblk, qidx_ref[0, :, g * group * qb:(g + 1) * group * qb],
                         preferred_element_type=F32)
            for hh in range(group):
                h = g * group + hh
                acc = acc + jnp.maximum(lg[:, hh * qb:(hh + 1) * qb], 0.0) * wt_ref[0, h:h + 1, :]
        causal = (k0 + krow) <= qpos
        isc_ref[pl.ds(k0, kc), :] = jnp.where(causal, acc, NEG)
        return carry

    lax.fori_loop(0, nch, idx_body, 0)

    def count_ge(t):
        def cnt_body(c, cnt8):
            sel = jnp.where(isc_ref[chunk_rows(c), :] >= t, 1, 0).astype(I32)
            return cnt8 + fold(jnp.sum, sel)

        cnt8 = lax.fori_loop(0, nch, cnt_body, jnp.zeros((8, qb), I32))
        return jnp.sum(cnt8, axis=0, keepdims=True)

    def bis_body(s, carry):
        tu, cnt_t = carry
        cand_u = tu | lax.shift_left(jnp.int32(1), 31 - s)
        cnt = count_ge(_float_of_rank(cand_u))
        ok = cnt >= k_sel
        return jnp.where(ok, cand_u, tu), jnp.where(ok, cnt, cnt_t)

    def trim(tu, cnt_t):
        thr = jnp.where(tu == 0, -jnp.inf, _float_of_rank(tu))
        few = thr < _ABOVE_NEG
        thr = jnp.where(few, _ABOVE_NEG, thr)
        cnt0 = jnp.where(few, k_sel, cnt_t)

        def trim_cond(state):
            _, cnt, n = state
            return (jnp.max(cnt) > k_sel) & (n < TRIM_PASSES)

        def trim_body(state):
            low, cnt, n = state

            def min_body(c, mn8):
                x = isc_ref[chunk_rows(c), :]
                sel = jnp.where(x >= thr, jnp.where(x > low, x, jnp.inf), jnp.inf)
                return jnp.minimum(mn8, fold(jnp.min, sel))

            mn = jnp.min(lax.fori_loop(0, nch, min_body, jnp.full((8, qb), jnp.inf, F32)),
                         axis=0, keepdims=True)

            def eq_body(c, cnt8):
                return cnt8 + fold(jnp.sum, jnp.where(isc_ref[chunk_rows(c), :] == mn, 1, 0).astype(I32))

            n_eq = jnp.sum(lax.fori_loop(0, nch, eq_body, jnp.zeros((8, qb), I32)),
                           axis=0, keepdims=True)
            can = (cnt > k_sel) & (cnt - n_eq >= k_sel)
            stalled = jnp.max(jnp.where(can, 1, 0)) == 0
            return (jnp.where(can, mn, low), jnp.where(can, cnt - n_eq, cnt),
                    jnp.where(stalled, TRIM_PASSES + 1, n + 1))

        low, cnt, n = lax.while_loop(trim_cond, trim_body,
                                     (jnp.full((1, qb), -jnp.inf, F32), cnt0, jnp.int32(0)))
        return thr, low, (n == TRIM_PASSES) & (jnp.max(cnt) > k_sel)

    start = (jnp.zeros((1, qb), I32), jnp.full((1, qb), k_sel, I32))
    tu, cnt_t = lax.fori_loop(0, SELECT_BITS, bis_body, start)
    thr, low, unresolved = trim(tu, cnt_t)
    thr, low = lax.cond(unresolved,
                        lambda: trim(*lax.fori_loop(SELECT_BITS, 32, bis_body, (tu, cnt_t)))[:2],
                        lambda: (thr, low))

    m_ref[...] = jnp.full(m_ref.shape, NEG, F32)
    l_ref[...] = jnp.zeros(l_ref.shape, F32)
    acc_ref[...] = jnp.zeros(acc_ref.shape, F32)
    pair = 2 * qb

    def scores(c, dst):
        k0 = pl.multiple_of(c * kc, kc)
        dst[...] = jnp.dot(kcat_ref[pl.ds(k0, kc), :], qcat_ref[0], preferred_element_type=F32)

    def att_step(c, cur, nxt):
        if nxt is not None:
            scores(c + 1, nxt)
        k0 = pl.multiple_of(c * kc, kc)
        x = isc_ref[pl.ds(k0, kc), :]
        bias = jnp.where(x >= thr, jnp.where(x > low, 0.0, NEG), NEG).astype(F32)
        s = cur[...] + jnp.concatenate([bias] * N_HEADS_A, axis=1)
        m_old = m_ref[...]
        m_new = jnp.maximum(m_old, jnp.max(s, axis=0, keepdims=True))
        alpha = jnp.exp2(m_old - m_new)
        p = jnp.exp2(s - m_new)
        l_ref[...] = alpha * l_ref[...] + jnp.sum(p, axis=0, keepdims=True)
        acc_ref[...] = alpha * acc_ref[...] + jnp.dot(ckvt_ref[c], p.astype(BF16),
                                                      preferred_element_type=F32)
        m_ref[...] = m_new

    scores(0, s0_ref)
    npair = (nch - 1) // 2

    def att_body(j, carry):
        att_step(2 * j, s0_ref, s1_ref)
        att_step(2 * j + 1, s1_ref, s0_ref)
        return carry

    lax.fori_loop(0, npair, att_body, 0)

    @pl.when(nch % 2 == 0)
    def _():
        att_step(nch - 2, s0_ref, s1_ref)
        att_step(nch - 1, s1_ref, None)

    @pl.when(nch % 2 == 1)
    def _():
        att_step(nch - 1, s0_ref, None)

    inv_l = 1.0 / l_ref[...]
    outs = []
    for h in range(N_HEADS_A):
        cols = slice(h * qb, (h + 1) * qb)
        o_t = acc_ref[:, cols] * inv_l[:, cols]
        outs.append(jnp.dot(o_t.T.astype(BF16), wuv_ref[h], preferred_element_type=F32))
    o = jnp.concatenate(outs, axis=1)
    g = gate_ref[...].astype(F32)
    o_ref[...] = (o * _silu(g)).astype(BF16)


def _dsa_call(kidx, kcat, ckvt, qcat, qidx, w_t, gates, wuv, batch, seq):
    rows = kidx.shape[0]
    nb = seq // QUERY_BLOCK
    nkc = seq // KEY_CHUNK
    k_sel = min(TOPK_MAX, seq // 4)
    kern = functools.partial(_dsa_kernel, k_sel=k_sel)
    return pl.pallas_call(
        kern,
        grid=(batch, nb),
        in_specs=[pl.BlockSpec((seq, KIDX_PAD), lambda b, i: (b, 0)),
                  pl.BlockSpec((seq, KCAT), lambda b, i: (b, 0)),
                  pl.BlockSpec((nkc, KV_RANK, KEY_CHUNK), lambda b, i: (b, 0, 0)),
                  pl.BlockSpec((1, KCAT, N_HEADS_A * QUERY_BLOCK), lambda b, i: (b * nb + i, 0, 0)),
                  pl.BlockSpec((1, KIDX_PAD, IDX_HEADS * QUERY_BLOCK), lambda b, i: (b * nb + i, 0, 0)),
                  pl.BlockSpec((1, IDX_HEADS, QUERY_BLOCK), lambda b, i: (b * nb + i, 0, 0)),
                  pl.BlockSpec((QUERY_BLOCK, WIDTH_A), lambda b, i: (b * nb + i, 0)),
                  pl.BlockSpec((N_HEADS_A, KV_RANK, HEAD_DIM), lambda b, i: (0, 0, 0))],
        out_specs=pl.BlockSpec((QUERY_BLOCK, WIDTH_A), lambda b, i: (b * nb + i, 0)),
        out_shape=jax.ShapeDtypeStruct((rows, WIDTH_A), BF16),
        scratch_shapes=[pltpu.VMEM((seq, QUERY_BLOCK), F32),
                        pltpu.VMEM((1, N_HEADS_A * QUERY_BLOCK), F32),
                        pltpu.VMEM((1, N_HEADS_A * QUERY_BLOCK), F32),
                        pltpu.VMEM((KV_RANK, N_HEADS_A * QUERY_BLOCK), F32),
                        pltpu.VMEM((KEY_CHUNK, N_HEADS_A * QUERY_BLOCK), F32),
                        pltpu.VMEM((KEY_CHUNK, N_HEADS_A * QUERY_BLOCK), F32)],
        compiler_params=_cparams(("parallel", "arbitrary")),
        name="dsa_attn",
    )(kidx, kcat, ckvt, qcat, qidx, w_t, gates, wuv)


def _dil_kernel(q_ref, k_ref, v_ref, gate_ref, c_ref, s_ref, o_ref,
                nat_ref, a4_ref, qn, kn, vn, q4, k4, v4, q16, k16, v16,
                oc0, ls0, oc1, ls1, oc2, ls2, *, qscale, unroll):
    seq = q_ref.shape[0]
    blk = N_BACK
    rc = ROW_CHUNK
    m4 = seq // 4
    m16 = seq // 16

    def build(src_ref, dn, d4, d16, rope_scale):
        def nat_body(c, carry):
            rows = pl.ds(pl.multiple_of(c * rc, rc), rc)
            x = src_ref[rows, :]
            if rope_scale is not None:
                x = x * c_ref[rows, :] + pltpu.roll(x, LANE // 2, 1) * s_ref[rows, :]
                if rope_scale != 1.0:
                    x = x * rope_scale
                nat_ref[rows, :] = x
            dn[rows, :] = x.astype(BF16)
            return carry

        lax.fori_loop(0, seq // rc, nat_body, 0)
        nat = src_ref if rope_scale is None else nat_ref

        def a4_body(c, carry):
            r4 = c // (m4 // rc)
            j0 = (c % (m4 // rc)) * rc
            x = nat[pl.ds(r4 + 4 * j0, rc, stride=4), :]
            rows = pl.ds(pl.multiple_of(c * rc, rc), rc)
            a4_ref[rows, :] = x
            d4[rows, :] = x.astype(BF16)
            return carry

        lax.fori_loop(0, seq // rc, a4_body, 0)

        def a16_body(r16, carry):
            r4 = r16 % 4
            s = r16 // 4
            x = a4_ref[pl.ds(r4 * m4 + s, m16, stride=4), :]
            d16[pl.ds(pl.multiple_of(r16 * m16, m16), m16), :] = x.astype(BF16)
            return carry

        lax.fori_loop(0, 16, a16_body, 0)

    build(q_ref, qn, q4, q16, qscale)
    build(k_ref, kn, k4, k16, 1.0)
    build(v_ref, vn, v4, v16, None)

    qi = lax.broadcasted_iota(I32, (blk, 2 * blk), 0) + blk
    kj = lax.broadcasted_iota(I32, (blk, 2 * blk), 1)
    rel = qi - kj
    bias_band = jnp.where((rel >= 0) & (rel <= N_BACK), 0.0, NEG).astype(F32)
    no_prev = jnp.where(kj < blk, NEG, 0.0).astype(F32)
    qi1 = lax.broadcasted_iota(I32, (blk, blk), 0)
    kj1 = lax.broadcasted_iota(I32, (blk, blk), 1)
    bias_first = jnp.where(kj1 <= qi1, 0.0, NEG).astype(F32)
    nt = (((1,), (1,)), ((), ()))

    def run_config(qs, ks, vs, nmb, store):
        def rows_of(u, first):
            cur = pl.ds(pl.multiple_of(u * blk, blk), blk)
            if first is True:
                return (cur,)
            return (pl.ds(pl.multiple_of(jnp.maximum(u * blk - blk, 0), blk), blk), cur)

        def gather(ref, rows):
            parts = [ref[r, :] for r in rows]
            return parts[0] if len(parts) == 1 else jnp.concatenate(parts, axis=0)

        def loop_body(g, carry):
            units = []
            for t in range(unroll):
                if nmb <= unroll:
                    first = (t % nmb == 0)
                else:
                    first = jnp.where(g == 0, 1.0, 0.0).astype(F32) if t == 0 else False
                units.append((g * unroll + t, first))
            scores = []
            for u, first in units:
                rows = rows_of(u, first)
                if first is True:
                    bias = bias_first
                elif first is False:
                    bias = bias_band
                else:
                    bias = bias_band + no_prev * first
                scores.append(lax.dot_general(qs[rows[-1], :], gather(ks, rows), nt,
                                              preferred_element_type=F32) + bias)
            probs = []
            for s in scores:
                m = jnp.max(s, axis=-1, keepdims=True)
                p = jnp.exp2(s - m)
                l = jnp.sum(p, axis=-1, keepdims=True)
                probs.append((p.astype(BF16), 1.0 / l, jnp.broadcast_to(m + jnp.log2(l), (blk, LANE))))
            for (u, first), (p, inv_l, lse) in zip(units, probs):
                o = jnp.dot(p, gather(vs, rows_of(u, first)), preferred_element_type=F32) * inv_l
                store(u, o, lse)
            return carry

        lax.fori_loop(0, seq // blk // unroll, loop_body, 0)

    def store_contig(oc, ls):
        def store(u, o, lse):
            rows = pl.ds(pl.multiple_of(u * blk, blk), blk)
            oc[rows, :] = o
            ls[rows, :] = lse
        return store

    def store_a16_to_a4(u, o, lse):
        r16 = u // (m16 // blk)
        mb = u % (m16 // blk)
        rows = pl.ds((r16 % 4) * m4 + 4 * blk * mb + r16 // 4, blk, stride=4)
        oc2[rows, :] = o
        ls2[rows, :] = lse

    run_config(qn, kn, vn, seq // blk, store_contig(oc0, ls0))
    run_config(q4, k4, v4, m4 // blk, store_contig(oc1, ls1))
    run_config(q16, k16, v16, m16 // blk, store_a16_to_a4)

    def merge_body(c, carry):
        r4 = c // (m4 // rc)
        j0 = (c % (m4 // rc)) * rc
        nat_rows = pl.ds(r4 + 4 * j0, rc, stride=4)
        rows = pl.ds(pl.multiple_of(c * rc, rc), rc)
        l0 = ls0[nat_rows, :]
        l1 = ls1[rows, :]
        l2 = ls2[rows, :]
        mx = jnp.maximum(jnp.maximum(l0, l1), l2)
        e0 = jnp.exp2(l0 - mx)
        e1 = jnp.exp2(l1 - mx)
        e2 = jnp.exp2(l2 - mx)
        nat_ref[nat_rows, :] = (e0 * oc0[nat_rows, :] + e1 * oc1[rows, :] + e2 * oc2[rows, :]) / (e0 + e1 + e2)
        return carry

    lax.fori_loop(0, seq // rc, merge_body, 0)

    def gate_body(c, carry):
        rows = pl.ds(pl.multiple_of(c * rc, rc), rc)
        g = gate_ref[rows, :].astype(F32)
        o_ref[rows, :] = (nat_ref[rows, :] * _silu(g)).astype(BF16)
        return carry

    lax.fori_loop(0, seq // rc, gate_body, 0)


def _dil_call(qkv, gates, ctab, stab, batch, seq):
    rows = qkv.shape[0]
    nh = N_HEADS_B
    qscale = HEAD_DIM ** -0.5 * LOG2E
    kern = functools.partial(_dil_kernel, qscale=qscale, unroll=16)
    blk = (seq, HEAD_DIM)
    return pl.pallas_call(
        kern,
        grid=(batch, nh),
        in_specs=[pl.BlockSpec(blk, lambda b, h: (b, h)),
                  pl.BlockSpec(blk, lambda b, h: (b, nh + h)),
                  pl.BlockSpec(blk, lambda b, h: (b, 2 * nh + h)),
                  pl.BlockSpec(blk, lambda b, h: (b, nh + h)),
                  pl.BlockSpec(blk, lambda b, h: (0, 0), pipeline_mode=pl.Buffered(1)),
                  pl.BlockSpec(blk, lambda b, h: (0, 0), pipeline_mode=pl.Buffered(1))],
        out_specs=pl.BlockSpec(blk, lambda b, h: (b, h)),
        out_shape=jax.ShapeDtypeStruct((rows, WIDTH_B), BF16),
        scratch_shapes=([pltpu.VMEM(blk, F32)] * 2 + [pltpu.VMEM(blk, BF16)] * 9
                        + [pltpu.VMEM(blk, F32)] * 6),
        compiler_params=_cparams(("parallel", "arbitrary")),
        name="dilated",
    )(qkv, qkv, qkv, gates, ctab, stab)


def _out_kernel(oa_ref, ob_ref, w_ref, x_ref, mod_ref, g_ref, o_ref):
    y = jnp.dot(oa_ref[...], w_ref[0:WIDTH_A, :], preferred_element_type=F32)
    y = y + jnp.dot(ob_ref[...], w_ref[WIDTH_A:WIDTH_A + WIDTH_B, :], preferred_element_type=F32)
    yn = y * lax.rsqrt(jnp.mean(y * y, axis=-1, keepdims=True) + EPS) * g_ref[...]
    o_ref[...] = x_ref[...] + mod_ref[0, 2:3, :] * yn


def _out_call(o_a, o_b, w_out, x2, mod3, g_post, seq):
    rows, d = x2.shape
    tm = OUT_TM
    tpb = seq // tm
    return pl.pallas_call(
        _out_kernel,
        grid=(rows // tm,),
        in_specs=[pl.BlockSpec((tm, WIDTH_A), lambda i: (i, 0)),
                  pl.BlockSpec((tm, WIDTH_B), lambda i: (i, 0)),
                  pl.BlockSpec((WIDTH_A + WIDTH_B, d), lambda i: (0, 0)),
                  pl.BlockSpec((tm, d), lambda i: (i, 0)),
                  pl.BlockSpec((1, 3, d), lambda i: (i // tpb, 0, 0)),
                  pl.BlockSpec((1, d), lambda i: (0, 0))],
        out_specs=pl.BlockSpec((tm, d), lambda i: (i, 0)),
        out_shape=jax.ShapeDtypeStruct((rows, d), F32),
        compiler_params=_cparams(("parallel",)),
        name="out_proj",
    )(o_a, o_b, w_out, x2, mod3, g_post)


def _rope_tables(seq):
    pos = jnp.arange(seq, dtype=F32)

    def cs(n_rot):
        inv = ROPE_THETA ** (-jnp.arange(0, n_rot, 2, dtype=F32) / n_rot)
        ang = pos[:, None] * inv[None, :]
        return jnp.cos(ang), jnp.sin(ang)

    c32, s32 = cs(ROPE_DIM)
    c16, s16 = cs(IDX_ROPE)
    ck = jnp.concatenate([c32, c32, jnp.ones((seq, LANE - ROPE_DIM), F32)], axis=1)
    sk = jnp.concatenate([-s32, s32, jnp.zeros((seq, LANE - ROPE_DIM), F32)], axis=1)
    ci = jnp.concatenate([c16, c16, jnp.ones((seq, LANE - IDX_ROPE), F32)], axis=1)
    si = jnp.concatenate([-s16, s16, jnp.zeros((seq, LANE - IDX_ROPE), F32)], axis=1)
    gap = jnp.ones((seq, LANE // 2 - ROPE_DIM // 2), F32)
    cd = jnp.concatenate([c32, gap, c32, gap], axis=1)
    sd = jnp.concatenate([-s32, 0.0 * gap, s32, 0.0 * gap], axis=1)
    return (c32.T, s32.T, c16.T, s16.T, ck, sk, ci, si, cd, sd)


def _pair_heads(wt):
    half = ROPE_DIM // 2
    mid = ROPE_DIM + (LANE // 2 - half)
    w3 = wt.reshape(-1, HEAD_DIM, wt.shape[-1])
    w3 = jnp.concatenate([w3[:, :half], w3[:, ROPE_DIM:mid], w3[:, half:ROPE_DIM], w3[:, mid:]], axis=1)
    return w3.reshape(wt.shape)


def _layer(x, c, w_ada, b_ada, g_pre, g_post, w_in, g_q, g_kv, w_uq, w_uq_idx, w_uk, w_uv, w_out):
    batch, seq, d = x.shape
    assert seq % PROJ_TM == 0 or seq < PROJ_TM
    assert seq % KEY_CHUNK == 0 and seq // 16 >= N_BACK
    assert all(w // dl == N_BACK for w, dl in DILATED_CONFIGS)
    rows = batch * seq
    x2 = x.reshape(rows, d)

    o_cq, o_ckv, o_kr, o_ki, o_wi = 0, 512, 768, 800, 864
    o_ga, o_q, o_k, o_v, o_gb = 880, 1904, 2928, 3952, 4976
    wt = w_in.T.astype(BF16)
    zeros = lambda n: jnp.zeros((n, d), BF16)
    w_all = jnp.concatenate([
        wt[o_cq:o_ki], zeros(896 - 800), wt[o_ki:o_ga], zeros(SMALL_W - 976),
        _pair_heads(wt[o_q:o_k]), _pair_heads(wt[o_k:o_v]), wt[o_v:o_gb],
        wt[o_ga:o_q], wt[o_gb:o_gb + WIDTH_B]], axis=0)
    assert w_all.shape[0] == IN_PAD

    c_pad = jnp.zeros((PACK16, d), F32).at[:batch].set(c)
    mod = _mod_call(c_pad, w_ada, b_ada.reshape(1, -1))[:batch]
    mod3 = mod.reshape(batch, 3, d)

    small, qkv, gates = _proj_call(x2, mod3, g_pre.reshape(1, d), w_all, seq)

    tabs = _rope_tables(seq)
    wuk_t = jnp.transpose(jnp.pad(w_uk, ((0, 0), (ROPE_DIM, 0), (0, 0))), (0, 2, 1)).astype(BF16)
    qcat, qidx, w_t, kcat, kidx, ckvt = _prep_call(
        small, g_q.reshape(1, -1), g_kv.reshape(1, -1), w_uq.T.astype(BF16),
        w_uq_idx.T.astype(BF16), wuk_t, tabs[:8], seq)

    o_a = _dsa_call(kidx, kcat, ckvt, qcat, qidx, w_t, gates, w_uv.astype(BF16), batch, seq)
    o_b = _dil_call(qkv, gates, tabs[8], tabs[9], batch, seq)
    out = _out_call(o_a, o_b, w_out.astype(BF16), x2, mod3, g_post.reshape(1, d), seq)
    return out.reshape(batch, seq, d)


def kernel(x, c, w_ada, b_ada, g_pre, g_post, w_in, g_q, g_kv, w_uq, w_uq_idx, w_uk, w_uv, w_out):
    for layer in range(w_ada.shape[0]):
        x = _layer(x, c, w_ada[layer], b_ada[layer], g_pre[layer], g_post[layer], w_in[layer],
                   g_q[layer], g_kv[layer], w_uq[layer], w_uq_idx[layer], w_uk[layer],
                   w_uv[layer], w_out[layer])
    return x
```

```python
import functools
import math

import numpy as np
import jax
import jax.numpy as jnp
from jax import lax
from jax.experimental import pallas as pl
from jax.experimental.pallas import tpu as pltpu

F32 = jnp.float32
BF16 = jnp.bfloat16
I32 = jnp.int32

HEAD_DIM = 128
ROPE_DIM = HEAD_DIM // 4
ROPE_THETA = 500000.0
EPS = 1e-6
NEG = -1e30
N_HEADS_A = 8
WIDTH_A = N_HEADS_A * HEAD_DIM
Q_RANK = 512
KV_RANK = 256
IDX_HEADS = 16
IDX_DIM = 64
IDX_ROPE = IDX_DIM // 4
TOPK_MAX = 256
QUERY_BLOCK = 128
N_HEADS_B = 8
WIDTH_B = N_HEADS_B * HEAD_DIM
DILATED_CONFIGS = ((128, 1), (512, 4), (2048, 16))
N_BACK = 128
SMALL_W = 1024
IN_PAD = SMALL_W + 5 * 1024
KCAT = 384
KIDX_PAD = 128
INT_MIN = -(2 ** 31)
LOG2E = 1.4426950408889634

LANE = 128
PACK16 = 16
VMEM_LIMIT = 56 * 1024 * 1024
PROJ_TM = 1024
PROJ_TN = 1024
PREP_TQ = 512
KEY_CHUNK = 512
OUT_TM = 512
ROW_CHUNK = 512
NORM_ROWS = 128


def _cparams(sem):
    return pltpu.CompilerParams(dimension_semantics=sem, vmem_limit_bytes=VMEM_LIMIT)


def _silu(g):
    return g * jax.nn.sigmoid(g)


def _split_bf16(x):
    hi = x.astype(BF16)
    return hi, (x - hi.astype(F32)).astype(BF16)


def _mod_kernel(c_ref, w_ref, b_ref, o_ref):
    a_hi, a_lo = _split_bf16(_silu(c_ref[...]))
    w_hi, w_lo = _split_bf16(w_ref[...])
    dot = functools.partial(jnp.dot, preferred_element_type=F32)
    o_ref[...] = dot(a_hi, w_hi) + (dot(a_hi, w_lo) + dot(a_lo, w_hi)) + b_ref[...]


def _mod_call(c_pad, w_ada, b_ada):
    rows, d = c_pad.shape
    n = w_ada.shape[1]
    tn = 1024
    return pl.pallas_call(
        _mod_kernel,
        grid=(n // tn,),
        in_specs=[pl.BlockSpec((rows, d), lambda j: (0, 0)),
                  pl.BlockSpec((d, tn), lambda j: (0, j)),
                  pl.BlockSpec((1, tn), lambda j: (0, j))],
        out_specs=pl.BlockSpec((rows, tn), lambda j: (0, j)),
        out_shape=jax.ShapeDtypeStruct((rows, n), F32),
        compiler_params=_cparams(("arbitrary",)),
        name="mod",
    )(c_pad, w_ada, b_ada)


def _proj_kernel(x_ref, mod_ref, g_ref, w_ref, small_ref, qkv_ref, gates_ref, h_ref, *,
                 n_small, n_qkv):
    j = pl.program_id(1)

    @pl.when(j == 0)
    def _():
        shift = mod_ref[0, 0:1, :]
        scale1 = 1.0 + mod_ref[0, 1:2, :]
        g = g_ref[...]

        def body(r, carry):
            rows = pl.ds(pl.multiple_of(r * NORM_ROWS, NORM_ROWS), NORM_ROWS)
            x = x_ref[rows, :]
            ms = jnp.mean(x * x, axis=-1, keepdims=True)
            y = x * lax.rsqrt(ms + EPS) * g
            h_ref[rows, :] = (y * scale1 + shift).astype(BF16)
            return carry

        lax.fori_loop(0, x_ref.shape[0] // NORM_ROWS, body, 0)

    def project():
        return lax.dot_general(h_ref[...], w_ref[...], (((1,), (1,)), ((), ())),
                               preferred_element_type=F32)

    @pl.when(j < n_small)
    def _():
        small_ref[...] = project()

    @pl.when((j >= n_small) & (j < n_small + n_qkv))
    def _():
        qkv_ref[...] = project()

    @pl.when(j >= n_small + n_qkv)
    def _():
        gates_ref[...] = project().astype(BF16)


def _proj_call(x2, mod3, g_pre, w_all, seq):
    rows, d = x2.shape
    tm = min(PROJ_TM, seq)
    tn = PROJ_TN
    n_small = SMALL_W // tn
    n_qkv = 3 * WIDTH_B // tn
    n_gate = (WIDTH_A + WIDTH_B) // tn
    tiles_per_batch = seq // tm
    kern = functools.partial(_proj_kernel, n_small=n_small, n_qkv=n_qkv)
    return pl.pallas_call(
        kern,
        grid=(rows // tm, n_small + n_qkv + n_gate),
        in_specs=[pl.BlockSpec((tm, d), lambda i, j: (i, 0)),
                  pl.BlockSpec((1, 3, d), lambda i, j: (i // tiles_per_batch, 0, 0)),
                  pl.BlockSpec((1, d), lambda i, j: (0, 0)),
                  pl.BlockSpec((tn, d), lambda i, j: (j, 0))],
        out_specs=[pl.BlockSpec((tm, tn), lambda i, j: (i, jnp.minimum(j, n_small - 1))),
                   pl.BlockSpec((tm, tn), lambda i, j: (i, jnp.clip(j - n_small, 0, n_qkv - 1))),
                   pl.BlockSpec((tm, tn),
                                lambda i, j: (i, jnp.clip(j - n_small - n_qkv, 0, n_gate - 1)))],
        out_shape=[jax.ShapeDtypeStruct((rows, SMALL_W), F32),
                   jax.ShapeDtypeStruct((rows, 3 * WIDTH_B), F32),
                   jax.ShapeDtypeStruct((rows, WIDTH_A + WIDTH_B), BF16)],
        scratch_shapes=[pltpu.VMEM((tm, d), BF16)],
        compiler_params=_cparams(("parallel", "arbitrary")),
        name="in_proj",
    )(x2, mod3, g_pre, w_all)


def _prep_kernel(small_ref, gq_ref, gkv_ref, wuq_ref, wuqi_ref, wuk_ref,
                 cosq_ref, sinq_ref, cosi_ref, sini_ref, ck_ref, sk_ref, ci_ref, si_ref,
                 qcat_ref, qidx_ref, wt_ref, kcat_ref, kidx_ref, ckvt_ref, *, qscale):
    tq = small_ref.shape[0]
    nblk = tq // QUERY_BLOCK
    cq = small_ref[:, 0:Q_RANK]
    cqn = (cq * lax.rsqrt(jnp.mean(cq * cq, axis=-1, keepdims=True) + EPS)
           * gq_ref[...]).astype(BF16)
    nt = (((1,), (1,)), ((), ()))

    q_t = lax.dot_general(wuq_ref[...], cqn, nt, preferred_element_type=F32)
    cosq = cosq_ref[...]
    sinq = sinq_ref[...]
    half = ROPE_DIM // 2
    zpad = jnp.zeros((KCAT - KV_RANK - ROPE_DIM, tq), F32)
    for h in range(N_HEADS_A):
        base = h * HEAD_DIM
        x1 = q_t[base:base + half]
        x2 = q_t[base + half:base + ROPE_DIM]
        q_rope = jnp.concatenate([x1 * cosq - x2 * sinq, x2 * cosq + x1 * sinq, zpad], axis=0)
        q_lat = jnp.dot(wuk_ref[h], q_t[base:base + HEAD_DIM].astype(BF16),
                        preferred_element_type=F32)
        q_lat = (q_lat * qscale).astype(BF16)
        q_rope = (q_rope * qscale).astype(BF16)
        for blk in range(nblk):
            cols = slice(blk * QUERY_BLOCK, (blk + 1) * QUERY_BLOCK)
            lanes = slice(h * QUERY_BLOCK, (h + 1) * QUERY_BLOCK)
            qcat_ref[blk, 0:KV_RANK, lanes] = q_lat[:, cols]
            qcat_ref[blk, KV_RANK:KCAT, lanes] = q_rope[:, cols]

    qi_t = lax.dot_general(wuqi_ref[...], cqn, nt, preferred_element_type=F32)
    cosi = cosi_ref[...]
    sini = sini_ref[...]
    ihalf = IDX_ROPE // 2
    ipad = jnp.zeros((KIDX_PAD - IDX_DIM, tq), F32)
    for h in range(IDX_HEADS):
        base = h * IDX_DIM
        x1 = qi_t[base:base + ihalf]
        x2 = qi_t[base + ihalf:base + IDX_ROPE]
        qi = jnp.concatenate([x1 * cosi - x2 * sini, x2 * cosi + x1 * sini,
                              qi_t[base + IDX_ROPE:base + IDX_DIM], ipad], axis=0).astype(BF16)
        for blk in range(nblk):
            cols = slice(blk * QUERY_BLOCK, (blk + 1) * QUERY_BLOCK)
            qidx_ref[blk, :, h * QUERY_BLOCK:(h + 1) * QUERY_BLOCK] = qi[:, cols]

    lane = lax.broadcasted_iota(I32, (tq, LANE), 1)

    slab_b = small_ref[:, 896:1024]
    swap_b = jnp.where(lane < ihalf, pltpu.roll(slab_b, LANE - ihalf, 1), pltpu.roll(slab_b, ihalf, 1))
    kidx_ref[...] = (slab_b * ci_ref[...] + swap_b * si_ref[...]).astype(BF16)
    w_t = slab_b.T[IDX_DIM:IDX_DIM + IDX_HEADS] * (IDX_HEADS ** -0.5 * IDX_DIM ** -0.5)
    for blk in range(nblk):
        wt_ref[blk] = w_t[:, blk * QUERY_BLOCK:(blk + 1) * QUERY_BLOCK]

    ckv = small_ref[:, Q_RANK:Q_RANK + KV_RANK]
    cn = ckv * lax.rsqrt(jnp.mean(ckv * ckv, axis=-1, keepdims=True) + EPS) * gkv_ref[...]
    kcat_ref[:, 0:KV_RANK] = cn.astype(BF16)
    ckvt_ref[0] = cn.T.astype(BF16)
    slab_a = small_ref[:, 768:896]
    swap_a = jnp.where(lane < half, pltpu.roll(slab_a, LANE - half, 1), pltpu.roll(slab_a, half, 1))
    kcat_ref[:, KV_RANK:KCAT] = (slab_a * ck_ref[...] + swap_a * sk_ref[...]).astype(BF16)


def _prep_call(small, g_q, g_kv, wuq_t, wuqi_t, wuk_t, tabs, seq):
    rows = small.shape[0]
    tq = PREP_TQ
    nblk = tq // QUERY_BLOCK
    tpb = seq // tq
    nqb = rows // QUERY_BLOCK
    qscale = HEAD_DIM ** -0.5 * LOG2E
    kern = functools.partial(_prep_kernel, qscale=qscale)
    const = lambda t: (0, 0)
    tcol = lambda t: (0, t % tpb)
    trow = lambda t: (t % tpb, 0)
    return pl.pallas_call(
        kern,
        grid=(rows // tq,),
        in_specs=[pl.BlockSpec((tq, SMALL_W), lambda t: (t, 0)),
                  pl.BlockSpec((1, Q_RANK), const),
                  pl.BlockSpec((1, KV_RANK), const),
                  pl.BlockSpec((WIDTH_A, Q_RANK), const),
                  pl.BlockSpec((IDX_HEADS * IDX_DIM, Q_RANK), const),
                  pl.BlockSpec((N_HEADS_A, KV_RANK, HEAD_DIM), lambda t: (0, 0, 0)),
                  pl.BlockSpec((ROPE_DIM // 2, tq), tcol),
                  pl.BlockSpec((ROPE_DIM // 2, tq), tcol),
                  pl.BlockSpec((IDX_ROPE // 2, tq), tcol),
                  pl.BlockSpec((IDX_ROPE // 2, tq), tcol),
                  pl.BlockSpec((tq, LANE), trow),
                  pl.BlockSpec((tq, LANE), trow),
                  pl.BlockSpec((tq, LANE), trow),
                  pl.BlockSpec((tq, LANE), trow)],
        out_specs=[pl.BlockSpec((nblk, KCAT, N_HEADS_A * QUERY_BLOCK), lambda t: (t, 0, 0)),
                   pl.BlockSpec((nblk, KIDX_PAD, IDX_HEADS * QUERY_BLOCK), lambda t: (t, 0, 0)),
                   pl.BlockSpec((nblk, IDX_HEADS, QUERY_BLOCK), lambda t: (t, 0, 0)),
                   pl.BlockSpec((tq, KCAT), lambda t: (t, 0)),
                   pl.BlockSpec((tq, KIDX_PAD), lambda t: (t, 0)),
                   pl.BlockSpec((1, KV_RANK, tq), lambda t: (t, 0, 0))],
        out_shape=[jax.ShapeDtypeStruct((nqb, KCAT, N_HEADS_A * QUERY_BLOCK), BF16),
                   jax.ShapeDtypeStruct((nqb, KIDX_PAD, IDX_HEADS * QUERY_BLOCK), BF16),
                   jax.ShapeDtypeStruct((nqb, IDX_HEADS, QUERY_BLOCK), F32),
                   jax.ShapeDtypeStruct((rows, KCAT), BF16),
                   jax.ShapeDtypeStruct((rows, KIDX_PAD), BF16),
                   jax.ShapeDtypeStruct((rows // tq, KV_RANK, tq), BF16)],
        compiler_params=_cparams(("parallel",)),
        name="dsa_prep",
    )(small, g_q, g_kv, wuq_t, wuqi_t, wuk_t, *tabs)


def _float_of_rank(u):
    key = u ^ INT_MIN
    return lax.bitcast_convert_type(key ^ (lax.shift_right_arithmetic(key, 31) & 0x7FFFFFFF), F32)


_ABOVE_NEG = float(np.nextafter(np.float32(NEG), np.float32(0.0)))
TRIM_PASSES = 8
SELECT_BITS = 24


def _dsa_kernel(kidx_ref, kcat_ref, ckvt_ref, qcat_ref, qidx_ref, wt_ref, gate_ref, wuv_ref,
                o_ref, isc_ref, m_ref, l_ref, acc_ref, s0_ref, s1_ref, *, k_sel):
    i = pl.program_id(1)
    kc = KEY_CHUNK
    qb = QUERY_BLOCK
    per = kc // qb
    nch = (i + per) // per
    qpos = i * qb + lax.broadcasted_iota(I32, (kc, qb), 1)
    krow = lax.broadcasted_iota(I32, (kc, qb), 0)

    def chunk_rows(c):
        return pl.ds(pl.multiple_of(c * kc, kc), kc)

    def fold(op, x):
        return op(x.reshape(kc // 8, 8, qb), axis=0)

    def pipelined(produce, consume, bufs):
        produce(0, bufs[0])

        def body(j, carry):
            produce(2 * j + 1, bufs[1])
            consume(2 * j, bufs[0])
            produce(2 * j + 2, bufs[0])
            consume(2 * j + 1, bufs[1])
            return carry

        lax.fori_loop(0, (nch - 1) // 2, body, 0)

        @pl.when(nch % 2 == 0)
        def _():
            produce(nch - 1, bufs[1])
            consume(nch - 2, bufs[0])
            consume(nch - 1, bufs[1])

        @pl.when(nch % 2 == 1)
        def _():
            consume(nch - 1, bufs[0])

    def idx_body(c, carry):
        kblk = kidx_ref[chunk_rows(c), :]
        acc = jnp.zeros((kc, qb), F32)
        group = 4
        for g in range(IDX_HEADS // group):
            lg = jnp.dot(kblk, qidx_ref[0, :, g * group * qb:(g + 1) * group * qb],
                         preferred_element_type=F32)
            for hh in range(group):
                h = g * group + hh
                acc = acc + jnp.maximum(lg[:, hh * qb:(hh + 1) * qb], 0.0) * wt_ref[0, h:h + 1, :]
        causal = (c * kc + krow) <= qpos
        isc_ref[chunk_rows(c), :] = jnp.where(causal, acc, NEG)
        return carry

    lax.fori_loop(0, nch, idx_body, 0)

    def count_ge(t):
        def cnt_body(c, cnt8):
            sel = jnp.where(isc_ref[chunk_rows(c), :] >= t, 1, 0).astype(I32)
            return cnt8 + fold(jnp.sum, sel)

        cnt8 = lax.fori_loop(0, nch, cnt_body, jnp.zeros((8, qb), I32))
        return jnp.sum(cnt8, axis=0, keepdims=True)

    def bis_body(s, carry):
        tu, cnt_t = carry
        cand_u = tu | lax.shift_left(jnp.int32(1), 31 - s)
        cnt = count_ge(_float_of_rank(cand_u))
        ok = cnt >= k_sel
        return jnp.where(ok, cand_u, tu), jnp.where(ok, cnt, cnt_t)

    def trim(tu, cnt_t):
        thr = jnp.where(tu == 0, -jnp.inf, _float_of_rank(tu))
        few = thr < _ABOVE_NEG
        thr = jnp.where(few, _ABOVE_NEG, thr)
        cnt0 = jnp.where(few, k_sel, cnt_t)

        def trim_cond(state):
            _, cnt, n = state
            return (jnp.max(cnt) > k_sel) & (n < TRIM_PASSES)

        def trim_body(state):
            low, cnt, n = state

            def min_body(c, mn8):
                x = isc_ref[chunk_rows(c), :]
                sel = jnp.where(x >= thr, jnp.where(x > low, x, jnp.inf), jnp.inf)
                return jnp.minimum(mn8, fold(jnp.min, sel))

            mn = jnp.min(lax.fori_loop(0, nch, min_body, jnp.full((8, qb), jnp.inf, F32)),
                         axis=0, keepdims=True)

            def eq_body(c, cnt8):
                return cnt8 + fold(jnp.sum, jnp.where(isc_ref[chunk_rows(c), :] == mn, 1, 0).astype(I32))

            n_eq = jnp.sum(lax.fori_loop(0, nch, eq_body, jnp.zeros((8, qb), I32)),
                           axis=0, keepdims=True)
            can = (cnt > k_sel) & (cnt - n_eq >= k_sel)
            stalled = jnp.max(jnp.where(can, 1, 0)) == 0
            return (jnp.where(can, mn, low), jnp.where(can, cnt - n_eq, cnt),
                    jnp.where(stalled, TRIM_PASSES + 1, n + 1))

        low, cnt, n = lax.while_loop(trim_cond, trim_body,
                                     (jnp.full((1, qb), -jnp.inf, F32), cnt0, jnp.int32(0)))
        return thr, low, (n == TRIM_PASSES) & (jnp.max(cnt) > k_sel)

    start = (jnp.zeros((1, qb), I32), jnp.full((1, qb), k_sel, I32))
    tu, cnt_t = lax.fori_loop(0, SELECT_BITS, bis_body, start)
    thr, low, unresolved = trim(tu, cnt_t)
    thr, low = lax.cond(unresolved,
                        lambda: trim(*lax.fori_loop(SELECT_BITS, 32, bis_body, (tu, cnt_t)))[:2],
                        lambda: (thr, low))

    m_ref[...] = jnp.full(m_ref.shape, NEG, F32)
    l_ref[...] = jnp.zeros(l_ref.shape, F32)
    acc_ref[...] = jnp.zeros(acc_ref.shape, F32)

    def scores(c, dst):
        dst[...] = jnp.dot(kcat_ref[chunk_rows(c), :], qcat_ref[0], preferred_element_type=F32)

    def attend(c, src):
        x = isc_ref[chunk_rows(c), :]
        bias = jnp.where(x >= thr, jnp.where(x > low, 0.0, NEG), NEG).astype(F32)
        s = src[...] + jnp.concatenate([bias] * N_HEADS_A, axis=1)
        m_old = m_ref[...]
        m_new = jnp.maximum(m_old, jnp.max(s, axis=0, keepdims=True))
        alpha = jnp.exp2(m_old - m_new)
        p = jnp.exp2(s - m_new)
        l_ref[...] = alpha * l_ref[...] + jnp.sum(p, axis=0, keepdims=True)
        acc_ref[...] = alpha * acc_ref[...] + jnp.dot(ckvt_ref[c], p.astype(BF16),
                                                      preferred_element_type=F32)
        m_ref[...] = m_new

    pipelined(scores, attend, (s0_ref, s1_ref))

    inv_l = 1.0 / l_ref[...]
    outs = []
    for h in range(N_HEADS_A):
        cols = slice(h * qb, (h + 1) * qb)
        o_t = acc_ref[:, cols] * inv_l[:, cols]
        outs.append(jnp.dot(o_t.T.astype(BF16), wuv_ref[h], preferred_element_type=F32))
    o = jnp.concatenate(outs, axis=1)
    g = gate_ref[...].astype(F32)
    o_ref[...] = (o * _silu(g)).astype(BF16)


def _dsa_call(kidx, kcat, ckvt, qcat, qidx, w_t, gates, wuv, batch, seq):
    rows = kidx.shape[0]
    nb = seq // QUERY_BLOCK
    nkc = seq // KEY_CHUNK
    k_sel = min(TOPK_MAX, seq // 4)
    kern = functools.partial(_dsa_kernel, k_sel=k_sel)
    return pl.pallas_call(
        kern,
        grid=(batch, nb),
        in_specs=[pl.BlockSpec((seq, KIDX_PAD), lambda b, i: (b, 0)),
                  pl.BlockSpec((seq, KCAT), lambda b, i: (b, 0)),
                  pl.BlockSpec((nkc, KV_RANK, KEY_CHUNK), lambda b, i: (b, 0, 0)),
                  pl.BlockSpec((1, KCAT, N_HEADS_A * QUERY_BLOCK), lambda b, i: (b * nb + i, 0, 0)),
                  pl.BlockSpec((1, KIDX_PAD, IDX_HEADS * QUERY_BLOCK), lambda b, i: (b * nb + i, 0, 0)),
                  pl.BlockSpec((1, IDX_HEADS, QUERY_BLOCK), lambda b, i: (b * nb + i, 0, 0)),
                  pl.BlockSpec((QUERY_BLOCK, WIDTH_A), lambda b, i: (b * nb + i, 0)),
                  pl.BlockSpec((N_HEADS_A, KV_RANK, HEAD_DIM), lambda b, i: (0, 0, 0))],
        out_specs=pl.BlockSpec((QUERY_BLOCK, WIDTH_A), lambda b, i: (b * nb + i, 0)),
        out_shape=jax.ShapeDtypeStruct((rows, WIDTH_A), BF16),
        scratch_shapes=[pltpu.VMEM((seq, QUERY_BLOCK), F32),
                        pltpu.VMEM((1, N_HEADS_A * QUERY_BLOCK), F32),
                        pltpu.VMEM((1, N_HEADS_A * QUERY_BLOCK), F32),
                        pltpu.VMEM((KV_RANK, N_HEADS_A * QUERY_BLOCK), F32),
                        pltpu.VMEM((KEY_CHUNK, N_HEADS_A * QUERY_BLOCK), F32),
                        pltpu.VMEM((KEY_CHUNK, N_HEADS_A * QUERY_BLOCK), F32)],
        compiler_params=_cparams(("parallel", "arbitrary")),
        name="dsa_attn",
    )(kidx, kcat, ckvt, qcat, qidx, w_t, gates, wuv)


def _dil_kernel(q_ref, k_ref, v_ref, gate_ref, c_ref, s_ref, o_ref,
                nat_ref, a4_ref, qn, kn, vn, q4, k4, v4, q16, k16, v16,
                oc0, ms0, ls0, oc1, ms1, ls1, oc2, ms2, ls2, *, qscale, unroll):
    seq = q_ref.shape[0]
    blk = N_BACK
    rc = ROW_CHUNK
    m4 = seq // 4
    m16 = seq // 16

    def build(src_ref, dn, d4, d16, rope_scale):
        def nat_body(c, carry):
            rows = pl.ds(pl.multiple_of(c * rc, rc), rc)
            x = src_ref[rows, :]
            if rope_scale is not None:
                x = x * c_ref[rows, :] + pltpu.roll(x, LANE // 2, 1) * s_ref[rows, :]
                if rope_scale != 1.0:
                    x = x * rope_scale
                nat_ref[rows, :] = x
            dn[rows, :] = x.astype(BF16)
            return carry

        lax.fori_loop(0, seq // rc, nat_body, 0)
        nat = src_ref if rope_scale is None else nat_ref

        def a4_body(c, carry):
            r4 = c // (m4 // rc)
            j0 = (c % (m4 // rc)) * rc
            x = nat[pl.ds(r4 + 4 * j0, rc, stride=4), :]
            rows = pl.ds(pl.multiple_of(c * rc, rc), rc)
            a4_ref[rows, :] = x
            d4[rows, :] = x.astype(BF16)
            return carry

        lax.fori_loop(0, seq // rc, a4_body, 0)

        def a16_body(r16, carry):
            r4 = r16 % 4
            s = r16 // 4
            x = a4_ref[pl.ds(r4 * m4 + s, m16, stride=4), :]
            d16[pl.ds(pl.multiple_of(r16 * m16, m16), m16), :] = x.astype(BF16)
            return carry

        lax.fori_loop(0, 16, a16_body, 0)

    build(q_ref, qn, q4, q16, qscale)
    build(k_ref, kn, k4, k16, 1.0)
    build(v_ref, vn, v4, v16, None)

    qi = lax.broadcasted_iota(I32, (blk, 2 * blk), 0) + blk
    kj = lax.broadcasted_iota(I32, (blk, 2 * blk), 1)
    rel = qi - kj
    bias_band = jnp.where((rel >= 0) & (rel <= N_BACK), 0.0, NEG).astype(F32)
    no_prev = jnp.where(kj < blk, NEG, 0.0).astype(F32)
    qi1 = lax.broadcasted_iota(I32, (blk, blk), 0)
    kj1 = lax.broadcasted_iota(I32, (blk, blk), 1)
    bias_first = jnp.where(kj1 <= qi1, 0.0, NEG).astype(F32)
    nt = (((1,), (1,)), ((), ()))

    def run_config(qs, ks, vs, nmb, store):
        def rows_of(u, first):
            cur = pl.ds(pl.multiple_of(u * blk, blk), blk)
            if first is True:
                return (cur,)
            return (pl.ds(pl.multiple_of(jnp.maximum(u * blk - blk, 0), blk), blk), cur)

        def gather(ref, rows):
            parts = [ref[r, :] for r in rows]
            return parts[0] if len(parts) == 1 else jnp.concatenate(parts, axis=0)

        def loop_body(g, carry):
            units = []
            for t in range(unroll):
                if nmb <= unroll:
                    first = (t % nmb == 0)
                else:
                    first = jnp.where(g == 0, 1.0, 0.0).astype(F32) if t == 0 else False
                units.append((g * unroll + t, first))
            scores = []
            for u, first in units:
                rows = rows_of(u, first)
                if first is True:
                    bias = bias_first
                elif first is False:
                    bias = bias_band
                else:
                    bias = bias_band + no_prev * first
                scores.append(lax.dot_general(qs[rows[-1], :], gather(ks, rows), nt,
                                              preferred_element_type=F32) + bias)
            probs = []
            for s in scores:
                m = jnp.max(s, axis=-1, keepdims=True)
                p = jnp.exp2(s - m)
                l = jnp.sum(p, axis=-1, keepdims=True)
                probs.append((p.astype(BF16), jnp.broadcast_to(m, (blk, LANE)),
                              jnp.broadcast_to(l, (blk, LANE))))
            for (u, first), (p, m_b, l_b) in zip(units, probs):
                store(u, jnp.dot(p, gather(vs, rows_of(u, first)), preferred_element_type=F32), m_b, l_b)
            return carry

        lax.fori_loop(0, seq // blk // unroll, loop_body, 0)

    def store_contig(oc, ms, ls):
        def store(u, o, m_b, l_b):
            rows = pl.ds(pl.multiple_of(u * blk, blk), blk)
            oc[rows, :] = o
            ms[rows, :] = m_b
            ls[rows, :] = l_b
        return store

    def store_a16_to_a4(u, o, m_b, l_b):
        r16 = u // (m16 // blk)
        mb = u % (m16 // blk)
        rows = pl.ds((r16 % 4) * m4 + 4 * blk * mb + r16 // 4, blk, stride=4)
        oc2[rows, :] = o
        ms2[rows, :] = m_b
        ls2[rows, :] = l_b

    run_config(qn, kn, vn, seq // blk, store_contig(oc0, ms0, ls0))
    run_config(q4, k4, v4, m4 // blk, store_contig(oc1, ms1, ls1))
    run_config(q16, k16, v16, m16 // blk, store_a16_to_a4)

    def merge_body(c, carry):
        r4 = c // (m4 // rc)
        j0 = (c % (m4 // rc)) * rc
        nat_rows = pl.ds(r4 + 4 * j0, rc, stride=4)
        rows = pl.ds(pl.multiple_of(c * rc, rc), rc)
        m0 = ms0[nat_rows, :]
        m1 = ms1[rows, :]
        m2 = ms2[rows, :]
        mx = jnp.maximum(jnp.maximum(m0, m1), m2)
        e0 = jnp.exp2(m0 - mx)
        e1 = jnp.exp2(m1 - mx)
        e2 = jnp.exp2(m2 - mx)
        den = e0 * ls0[nat_rows, :] + e1 * ls1[rows, :] + e2 * ls2[rows, :]
        nat_ref[nat_rows, :] = (e0 * oc0[nat_rows, :] + e1 * oc1[rows, :] + e2 * oc2[rows, :]) / den
        return carry

    lax.fori_loop(0, seq // rc, merge_body, 0)

    def gate_body(c, carry):
        rows = pl.ds(pl.multiple_of(c * rc, rc), rc)
        g = gate_ref[rows, :].astype(F32)
        o_ref[rows, :] = (nat_ref[rows, :] * _silu(g)).astype(BF16)
        return carry

    lax.fori_loop(0, seq // rc, gate_body, 0)


def _dil_call(qkv, gates, ctab, stab, batch, seq):
    rows = qkv.shape[0]
    nh = N_HEADS_B
    qscale = HEAD_DIM ** -0.5 * LOG2E
    kern = functools.partial(_dil_kernel, qscale=qscale, unroll=16)
    blk = (seq, HEAD_DIM)
    return pl.pallas_call(
        kern,
        grid=(batch, nh),
        in_specs=[pl.BlockSpec(blk, lambda b, h: (b, h)),
                  pl.BlockSpec(blk, lambda b, h: (b, nh + h)),
                  pl.BlockSpec(blk, lambda b, h: (b, 2 * nh + h)),
                  pl.BlockSpec(blk, lambda b, h: (b, nh + h)),
                  pl.BlockSpec(blk, lambda b, h: (0, 0), pipeline_mode=pl.Buffered(1)),
                  pl.BlockSpec(blk, lambda b, h: (0, 0), pipeline_mode=pl.Buffered(1))],
        out_specs=pl.BlockSpec(blk, lambda b, h: (b, h)),
        out_shape=jax.ShapeDtypeStruct((rows, WIDTH_B), BF16),
        scratch_shapes=([pltpu.VMEM(blk, F32)] * 2 + [pltpu.VMEM(blk, BF16)] * 9
                        + [pltpu.VMEM(blk, F32)] * 9),
        compiler_params=_cparams(("parallel", "arbitrary")),
        name="dilated",
    )(qkv, qkv, qkv, gates, ctab, stab)


def _out_kernel(oa_ref, ob_ref, w_ref, x_ref, mod_ref, g_ref, o_ref):
    y = jnp.dot(oa_ref[...], w_ref[0:WIDTH_A, :], preferred_element_type=F32)
    y = y + jnp.dot(ob_ref[...], w_ref[WIDTH_A:WIDTH_A + WIDTH_B, :], preferred_element_type=F32)
    yn = y * lax.rsqrt(jnp.mean(y * y, axis=-1, keepdims=True) + EPS) * g_ref[...]
    o_ref[...] = x_ref[...] + mod_ref[0, 2:3, :] * yn


def _out_call(o_a, o_b, w_out, x2, mod3, g_post, seq):
    rows, d = x2.shape
    tm = OUT_TM
    tpb = seq // tm
    return pl.pallas_call(
        _out_kernel,
        grid=(rows // tm,),
        in_specs=[pl.BlockSpec((tm, WIDTH_A), lambda i: (i, 0)),
                  pl.BlockSpec((tm, WIDTH_B), lambda i: (i, 0)),
                  pl.BlockSpec((WIDTH_A + WIDTH_B, d), lambda i: (0, 0)),
                  pl.BlockSpec((tm, d), lambda i: (i, 0)),
                  pl.BlockSpec((1, 3, d), lambda i: (i // tpb, 0, 0)),
                  pl.BlockSpec((1, d), lambda i: (0, 0))],
        out_specs=pl.BlockSpec((tm, d), lambda i: (i, 0)),
        out_shape=jax.ShapeDtypeStruct((rows, d), F32),
        compiler_params=_cparams(("parallel",)),
        name="out_proj",
    )(o_a, o_b, w_out, x2, mod3, g_post)


def _rope_tables(seq):
    pos = jnp.arange(seq, dtype=F32)

    def cs(n_rot):
        inv = ROPE_THETA ** (-jnp.arange(0, n_rot, 2, dtype=F32) / n_rot)
        ang = pos[:, None] * inv[None, :]
        return jnp.cos(ang), jnp.sin(ang)

    c32, s32 = cs(ROPE_DIM)
    c16, s16 = cs(IDX_ROPE)
    ck = jnp.concatenate([c32, c32, jnp.ones((seq, LANE - ROPE_DIM), F32)], axis=1)
    sk = jnp.concatenate([-s32, s32, jnp.zeros((seq, LANE - ROPE_DIM), F32)], axis=1)
    ci = jnp.concatenate([c16, c16, jnp.ones((seq, LANE - IDX_ROPE), F32)], axis=1)
    si = jnp.concatenate([-s16, s16, jnp.zeros((seq, LANE - IDX_ROPE), F32)], axis=1)
    gap = jnp.ones((seq, LANE // 2 - ROPE_DIM // 2), F32)
    cd = jnp.concatenate([c32, gap, c32, gap], axis=1)
    sd = jnp.concatenate([-s32, 0.0 * gap, s32, 0.0 * gap], axis=1)
    return (c32.T, s32.T, c16.T, s16.T, ck, sk, ci, si, cd, sd)


def _pair_heads(wt):
    half = ROPE_DIM // 2
    mid = ROPE_DIM + (LANE // 2 - half)
    w3 = wt.reshape(-1, HEAD_DIM, wt.shape[-1])
    w3 = jnp.concatenate([w3[:, :half], w3[:, ROPE_DIM:mid], w3[:, half:ROPE_DIM], w3[:, mid:]], axis=1)
    return w3.reshape(wt.shape)


def _layer(x, c, w_ada, b_ada, g_pre, g_post, w_in, g_q, g_kv, w_uq, w_uq_idx, w_uk, w_uv, w_out):
    batch, seq, d = x.shape
    assert seq % PROJ_TM == 0 or seq < PROJ_TM
    assert seq % KEY_CHUNK == 0 and seq // 16 >= N_BACK
    assert all(w // dl == N_BACK for w, dl in DILATED_CONFIGS)
    rows = batch * seq
    x2 = x.reshape(rows, d)

    o_cq, o_ckv, o_kr, o_ki, o_wi = 0, 512, 768, 800, 864
    o_ga, o_q, o_k, o_v, o_gb = 880, 1904, 2928, 3952, 4976
    wt = w_in.T.astype(BF16)
    zeros = lambda n: jnp.zeros((n, d), BF16)
    w_all = jnp.concatenate([
        wt[o_cq:o_ki], zeros(896 - 800), wt[o_ki:o_ga], zeros(SMALL_W - 976),
        _pair_heads(wt[o_q:o_k]), _pair_heads(wt[o_k:o_v]), wt[o_v:o_gb],
        wt[o_ga:o_q], wt[o_gb:o_gb + WIDTH_B]], axis=0)
    assert w_all.shape[0] == IN_PAD

    c_pad = jnp.zeros((PACK16, d), F32).at[:batch].set(c)
    mod = _mod_call(c_pad, w_ada, b_ada.reshape(1, -1))[:batch]
    mod3 = mod.reshape(batch, 3, d)

    small, qkv, gates = _proj_call(x2, mod3, g_pre.reshape(1, d), w_all, seq)

    tabs = _rope_tables(seq)
    wuk_t = jnp.transpose(jnp.pad(w_uk, ((0, 0), (ROPE_DIM, 0), (0, 0))), (0, 2, 1)).astype(BF16)
    qcat, qidx, w_t, kcat, kidx, ckvt = _prep_call(
        small, g_q.reshape(1, -1), g_kv.reshape(1, -1), w_uq.T.astype(BF16),
        w_uq_idx.T.astype(BF16), wuk_t, tabs[:8], seq)

    o_a = _dsa_call(kidx, kcat, ckvt, qcat, qidx, w_t, gates, w_uv.astype(BF16), batch, seq)
    o_b = _dil_call(qkv, gates, tabs[8], tabs[9], batch, seq)
    out = _out_call(o_a, o_b, w_out.astype(BF16), x2, mod3, g_post.reshape(1, d), seq)
    return out.reshape(batch, seq, d)


def kernel(x, c, w_ada, b_ada, g_pre, g_post, w_in, g_q, g_kv, w_uq, w_uq_idx, w_uk, w_uv, w_out):
    for layer in range(w_ada.shape[0]):
        x = _layer(x, c, w_ada[layer], b_ada[layer], g_pre[layer], g_post[layer], w_in[layer],
                   g_q[layer], g_kv[layer], w_uq[layer], w_uq_idx[layer], w_uk[layer],
                   w_uv[layer], w_out[layer])
    return x
```

```python
import functools
import math

import numpy as np
import jax
import jax.numpy as jnp
from jax import lax
from jax.experimental import pallas as pl
from jax.experimental.pallas import tpu as pltpu

F32 = jnp.float32
BF16 = jnp.bfloat16
I32 = jnp.int32

HEAD_DIM = 128
ROPE_DIM = HEAD_DIM // 4
ROPE_THETA = 500000.0
EPS = 1e-6
NEG = -1e30
N_HEADS_A = 8
WIDTH_A = N_HEADS_A * HEAD_DIM
Q_RANK = 512
KV_RANK = 256
IDX_HEADS = 16
IDX_DIM = 64
IDX_ROPE = IDX_DIM // 4
TOPK_MAX = 256
QUERY_BLOCK = 128
N_HEADS_B = 8
WIDTH_B = N_HEADS_B * HEAD_DIM
DILATED_CONFIGS = ((128, 1), (512, 4), (2048, 16))
N_BACK = 128
SMALL_W = 1024
IN_PAD = SMALL_W + 5 * 1024
KCAT = 384
KIDX_PAD = 128
INT_MIN = -(2 ** 31)
LOG2E = 1.4426950408889634

LANE = 128
PACK16 = 16
VMEM_LIMIT = 56 * 1024 * 1024
PROJ_TM = 1024
PROJ_TN = 1024
PREP_TQ = 512
KEY_CHUNK = 512
OUT_TM = 512
ROW_CHUNK = 512
NORM_ROWS = 128


def _cparams(sem):
    return pltpu.CompilerParams(dimension_semantics=sem, vmem_limit_bytes=VMEM_LIMIT)


def _silu(g):
    return g * jax.nn.sigmoid(g)


def _split_bf16(x):
    hi = x.astype(BF16)
    return hi, (x - hi.astype(F32)).astype(BF16)


def _mod_kernel(c_ref, w_ref, b_ref, o_ref):
    a_hi, a_lo = _split_bf16(_silu(c_ref[...]))
    w_hi, w_lo = _split_bf16(w_ref[...])
    dot = functools.partial(jnp.dot, preferred_element_type=F32)
    o_ref[...] = dot(a_hi, w_hi) + (dot(a_hi, w_lo) + dot(a_lo, w_hi)) + b_ref[...]


def _mod_call(c_pad, w_ada, b_ada):
    rows, d = c_pad.shape
    n = w_ada.shape[1]
    tn = 1024
    return pl.pallas_call(
        _mod_kernel,
        grid=(n // tn,),
        in_specs=[pl.BlockSpec((rows, d), lambda j: (0, 0)),
                  pl.BlockSpec((d, tn), lambda j: (0, j)),
                  pl.BlockSpec((1, tn), lambda j: (0, j))],
        out_specs=pl.BlockSpec((rows, tn), lambda j: (0, j)),
        out_shape=jax.ShapeDtypeStruct((rows, n), F32),
        compiler_params=_cparams(("arbitrary",)),
        name="mod",
    )(c_pad, w_ada, b_ada)


def _proj_kernel(x_ref, mod_ref, g_ref, w_ref, small_ref, qkv_ref, gates_ref, h_ref, *,
                 n_small, n_qkv):
    j = pl.program_id(1)
    tm = x_ref.shape[0]
    half = tm // 2
    assert n_small == 1 and half % NORM_ROWS == 0

    def project(rows=slice(None)):
        return lax.dot_general(h_ref[rows, :], w_ref[...], (((1,), (1,)), ((), ())),
                               preferred_element_type=F32)

    @pl.when(j == 0)
    def _():
        shift = mod_ref[0, 0:1, :]
        scale1 = 1.0 + mod_ref[0, 1:2, :]
        g = g_ref[...]

        def norm_rows(rows):
            x = x_ref[rows, :]
            ms = jnp.mean(x * x, axis=-1, keepdims=True)
            y = x * lax.rsqrt(ms + EPS) * g
            h_ref[rows, :] = (y * scale1 + shift).astype(BF16)

        def body(r, carry):
            norm_rows(pl.ds(pl.multiple_of(r * NORM_ROWS, NORM_ROWS), NORM_ROWS))
            return carry

        lax.fori_loop(0, half // NORM_ROWS, body, 0)
        small_ref[0:half, :] = project(slice(0, half))
        for r in range(half // NORM_ROWS):
            norm_rows(slice(half + r * NORM_ROWS, half + (r + 1) * NORM_ROWS))
        small_ref[half:tm, :] = project(slice(half, tm))

    @pl.when((j >= n_small) & (j < n_small + n_qkv))
    def _():
        qkv_ref[...] = project()

    @pl.when(j >= n_small + n_qkv)
    def _():
        gates_ref[...] = project().astype(BF16)


def _proj_call(x2, mod3, g_pre, w_all, seq):
    rows, d = x2.shape
    tm = min(PROJ_TM, seq)
    tn = PROJ_TN
    n_small = SMALL_W // tn
    n_qkv = 3 * WIDTH_B // tn
    n_gate = (WIDTH_A + WIDTH_B) // tn
    tiles_per_batch = seq // tm
    kern = functools.partial(_proj_kernel, n_small=n_small, n_qkv=n_qkv)
    return pl.pallas_call(
        kern,
        grid=(rows // tm, n_small + n_qkv + n_gate),
        in_specs=[pl.BlockSpec((tm, d), lambda i, j: (i, 0)),
                  pl.BlockSpec((1, 3, d), lambda i, j: (i // tiles_per_batch, 0, 0)),
                  pl.BlockSpec((1, d), lambda i, j: (0, 0)),
                  pl.BlockSpec((tn, d), lambda i, j: (j, 0))],
        out_specs=[pl.BlockSpec((tm, tn), lambda i, j: (i, jnp.minimum(j, n_small - 1))),
                   pl.BlockSpec((tm, tn), lambda i, j: (i, jnp.clip(j - n_small, 0, n_qkv - 1))),
                   pl.BlockSpec((tm, tn),
                                lambda i, j: (i, jnp.clip(j - n_small - n_qkv, 0, n_gate - 1)))],
        out_shape=[jax.ShapeDtypeStruct((rows, SMALL_W), F32),
                   jax.ShapeDtypeStruct((rows, 3 * WIDTH_B), F32),
                   jax.ShapeDtypeStruct((rows, WIDTH_A + WIDTH_B), BF16)],
        scratch_shapes=[pltpu.VMEM((tm, d), BF16)],
        compiler_params=_cparams(("parallel", "arbitrary")),
        name="in_proj",
    )(x2, mod3, g_pre, w_all)


def _prep_kernel(small_ref, gq_ref, gkv_ref, wuq_ref, wuqi_ref, wuk_ref,
                 cosq_ref, sinq_ref, cosi_ref, sini_ref, ck_ref, sk_ref, ci_ref, si_ref,
                 qcat_ref, qidx_ref, wt_ref, kcat_ref, kidx_ref, ckvt_ref, *, qscale):
    tq = small_ref.shape[0]
    nblk = tq // QUERY_BLOCK
    cq = small_ref[:, 0:Q_RANK]
    cqn = (cq * lax.rsqrt(jnp.mean(cq * cq, axis=-1, keepdims=True) + EPS)
           * gq_ref[...]).astype(BF16)
    nt = (((1,), (1,)), ((), ()))

    q_t = lax.dot_general(wuq_ref[...], cqn, nt, preferred_element_type=F32)
    cosq = cosq_ref[...]
    sinq = sinq_ref[...]
    half = ROPE_DIM // 2
    zpad = jnp.zeros((KCAT - KV_RANK - ROPE_DIM, tq), F32)
    for h in range(N_HEADS_A):
        base = h * HEAD_DIM
        x1 = q_t[base:base + half]
        x2 = q_t[base + half:base + ROPE_DIM]
        q_rope = jnp.concatenate([x1 * cosq - x2 * sinq, x2 * cosq + x1 * sinq, zpad], axis=0)
        q_lat = jnp.dot(wuk_ref[h], q_t[base:base + HEAD_DIM].astype(BF16),
                        preferred_element_type=F32)
        q_lat = (q_lat * qscale).astype(BF16)
        q_rope = (q_rope * qscale).astype(BF16)
        for blk in range(nblk):
            cols = slice(blk * QUERY_BLOCK, (blk + 1) * QUERY_BLOCK)
            lanes = slice(h * QUERY_BLOCK, (h + 1) * QUERY_BLOCK)
            qcat_ref[blk, 0:KV_RANK, lanes] = q_lat[:, cols]
            qcat_ref[blk, KV_RANK:KCAT, lanes] = q_rope[:, cols]

    qi_t = lax.dot_general(wuqi_ref[...], cqn, nt, preferred_element_type=F32)
    cosi = cosi_ref[...]
    sini = sini_ref[...]
    ihalf = IDX_ROPE // 2
    ipad = jnp.zeros((KIDX_PAD - IDX_DIM, tq), F32)
    for h in range(IDX_HEADS):
        base = h * IDX_DIM
        x1 = qi_t[base:base + ihalf]
        x2 = qi_t[base + ihalf:base + IDX_ROPE]
        qi = jnp.concatenate([x1 * cosi - x2 * sini, x2 * cosi + x1 * sini,
                              qi_t[base + IDX_ROPE:base + IDX_DIM], ipad], axis=0).astype(BF16)
        for blk in range(nblk):
            cols = slice(blk * QUERY_BLOCK, (blk + 1) * QUERY_BLOCK)
            qidx_ref[blk, :, h * QUERY_BLOCK:(h + 1) * QUERY_BLOCK] = qi[:, cols]

    lane = lax.broadcasted_iota(I32, (tq, LANE), 1)

    slab_b = small_ref[:, 896:1024]
    swap_b = jnp.where(lane < ihalf, pltpu.roll(slab_b, LANE - ihalf, 1), pltpu.roll(slab_b, ihalf, 1))
    kidx_ref[...] = (slab_b * ci_ref[...] + swap_b * si_ref[...]).astype(BF16)
    w_t = slab_b.T[IDX_DIM:IDX_DIM + IDX_HEADS] * (IDX_HEADS ** -0.5 * IDX_DIM ** -0.5)
    for blk in range(nblk):
        wt_ref[blk] = w_t[:, blk * QUERY_BLOCK:(blk + 1) * QUERY_BLOCK]

    ckv = small_ref[:, Q_RANK:Q_RANK + KV_RANK]
    cn = ckv * lax.rsqrt(jnp.mean(ckv * ckv, axis=-1, keepdims=True) + EPS) * gkv_ref[...]
    kcat_ref[:, 0:KV_RANK] = cn.astype(BF16)
    ckvt_ref[0] = cn.T.astype(BF16)
    slab_a = small_ref[:, 768:896]
    swap_a = jnp.where(lane < half, pltpu.roll(slab_a, LANE - half, 1), pltpu.roll(slab_a, half, 1))
    kcat_ref[:, KV_RANK:KCAT] = (slab_a * ck_ref[...] + swap_a * sk_ref[...]).astype(BF16)


def _prep_call(small, g_q, g_kv, wuq_t, wuqi_t, wuk_t, tabs, seq):
    rows = small.shape[0]
    tq = PREP_TQ
    nblk = tq // QUERY_BLOCK
    tpb = seq // tq
    nqb = rows // QUERY_BLOCK
    qscale = HEAD_DIM ** -0.5 * LOG2E
    kern = functools.partial(_prep_kernel, qscale=qscale)
    const = lambda t: (0, 0)
    tcol = lambda t: (0, t % tpb)
    trow = lambda t: (t % tpb, 0)
    return pl.pallas_call(
        kern,
        grid=(rows // tq,),
        in_specs=[pl.BlockSpec((tq, SMALL_W), lambda t: (t, 0)),
                  pl.BlockSpec((1, Q_RANK), const),
                  pl.BlockSpec((1, KV_RANK), const),
                  pl.BlockSpec((WIDTH_A, Q_RANK), const),
                  pl.BlockSpec((IDX_HEADS * IDX_DIM, Q_RANK), const),
                  pl.BlockSpec((N_HEADS_A, KV_RANK, HEAD_DIM), lambda t: (0, 0, 0)),
                  pl.BlockSpec((ROPE_DIM // 2, tq), tcol),
                  pl.BlockSpec((ROPE_DIM // 2, tq), tcol),
                  pl.BlockSpec((IDX_ROPE // 2, tq), tcol),
                  pl.BlockSpec((IDX_ROPE // 2, tq), tcol),
                  pl.BlockSpec((tq, LANE), trow),
                  pl.BlockSpec((tq, LANE), trow),
                  pl.BlockSpec((tq, LANE), trow),
                  pl.BlockSpec((tq, LANE), trow)],
        out_specs=[pl.BlockSpec((nblk, KCAT, N_HEADS_A * QUERY_BLOCK), lambda t: (t, 0, 0)),
                   pl.BlockSpec((nblk, KIDX_PAD, IDX_HEADS * QUERY_BLOCK), lambda t: (t, 0, 0)),
                   pl.BlockSpec((nblk, IDX_HEADS, QUERY_BLOCK), lambda t: (t, 0, 0)),
                   pl.BlockSpec((tq, KCAT), lambda t: (t, 0)),
                   pl.BlockSpec((tq, KIDX_PAD), lambda t: (t, 0)),
                   pl.BlockSpec((1, KV_RANK, tq), lambda t: (t, 0, 0))],
        out_shape=[jax.ShapeDtypeStruct((nqb, KCAT, N_HEADS_A * QUERY_BLOCK), BF16),
                   jax.ShapeDtypeStruct((nqb, KIDX_PAD, IDX_HEADS * QUERY_BLOCK), BF16),
                   jax.ShapeDtypeStruct((nqb, IDX_HEADS, QUERY_BLOCK), F32),
                   jax.ShapeDtypeStruct((rows, KCAT), BF16),
                   jax.ShapeDtypeStruct((rows, KIDX_PAD), BF16),
                   jax.ShapeDtypeStruct((rows // tq, KV_RANK, tq), BF16)],
        compiler_params=_cparams(("parallel",)),
        name="dsa_prep",
    )(small, g_q, g_kv, wuq_t, wuqi_t, wuk_t, *tabs)


def _float_of_rank(u):
    key = u ^ INT_MIN
    return lax.bitcast_convert_type(key ^ (lax.shift_right_arithmetic(key, 31) & 0x7FFFFFFF), F32)


_ABOVE_NEG = float(np.nextafter(np.float32(NEG), np.float32(0.0)))
TRIM_PASSES = 8
SELECT_BITS = 24


def _dsa_kernel(kidx_ref, kcat_ref, ckvt_ref, qcat_ref, qidx_ref, wt_ref, gate_ref, wuv_ref,
                o_ref, isc_ref, m_ref, l_ref, acc_ref, s0_ref, s1_ref, *, k_sel):
    i = pl.program_id(1)
    kc = KEY_CHUNK
    qb = QUERY_BLOCK
    per = kc // qb
    nch = (i + per) // per
    qpos = i * qb + lax.broadcasted_iota(I32, (kc, qb), 1)
    krow = lax.broadcasted_iota(I32, (kc, qb), 0)

    def chunk_rows(c):
        return pl.ds(pl.multiple_of(c * kc, kc), kc)

    def fold(op, x):
        return op(x.reshape(kc // 8, 8, qb), axis=0)

    def pipelined(produce, consume, bufs):
        produce(0, bufs[0])

        def body(j, carry):
            produce(2 * j + 1, bufs[1])
            consume(2 * j, bufs[0])
            produce(2 * j + 2, bufs[0])
            consume(2 * j + 1, bufs[1])
            return carry

        lax.fori_loop(0, (nch - 1) // 2, body, 0)

        @pl.when(nch % 2 == 0)
        def _():
            produce(nch - 1, bufs[1])
            consume(nch - 2, bufs[0])
            consume(nch - 1, bufs[1])

        @pl.when(nch % 2 == 1)
        def _():
            consume(nch - 1, bufs[0])

    def index_chunk(c):
        kblk = kidx_ref[chunk_rows(c), :]
        acc = jnp.zeros((kc, qb), F32)
        group = 4
        for g in range(IDX_HEADS // group):
            lg = jnp.dot(kblk, qidx_ref[0, :, g * group * qb:(g + 1) * group * qb],
                         preferred_element_type=F32)
            for hh in range(group):
                h = g * group + hh
                acc = acc + jnp.maximum(lg[:, hh * qb:(hh + 1) * qb], 0.0) * wt_ref[0, h:h + 1, :]
        causal = (c * kc + krow) <= qpos
        isc_ref[chunk_rows(c), :] = jnp.where(causal, acc, NEG)

    def idx_body(j, carry):
        index_chunk(2 * j)
        index_chunk(2 * j + 1)
        return carry

    lax.fori_loop(0, nch // 2, idx_body, 0)

    @pl.when(nch % 2 == 1)
    def _():
        index_chunk(nch - 1)

    def count_ge(t):
        def chunk_count(c):
            return fold(jnp.sum, jnp.where(isc_ref[chunk_rows(c), :] >= t, 1, 0).astype(I32))

        def cnt_body(j, cnt8):
            return cnt8 + (chunk_count(2 * j) + chunk_count(2 * j + 1))

        cnt8 = lax.fori_loop(0, nch // 2, cnt_body, jnp.zeros((8, qb), I32))
        cnt8 = cnt8 + lax.cond(nch % 2 == 1, lambda: chunk_count(nch - 1),
                               lambda: jnp.zeros((8, qb), I32))
        return jnp.sum(cnt8, axis=0, keepdims=True)

    def bis_body(s, carry):
        tu, cnt_t = carry
        cand_u = tu | lax.shift_left(jnp.int32(1), 31 - s)
        cnt = count_ge(_float_of_rank(cand_u))
        ok = cnt >= k_sel
        return jnp.where(ok, cand_u, tu), jnp.where(ok, cnt, cnt_t)

    def trim(tu, cnt_t):
        thr = jnp.where(tu == 0, -jnp.inf, _float_of_rank(tu))
        few = thr < _ABOVE_NEG
        thr = jnp.where(few, _ABOVE_NEG, thr)
        cnt0 = jnp.where(few, k_sel, cnt_t)

        def trim_cond(state):
            _, cnt, n = state
            return (jnp.max(cnt) > k_sel) & (n < TRIM_PASSES)

        def trim_body(state):
            low, cnt, n = state

            def min_body(c, mn8):
                x = isc_ref[chunk_rows(c), :]
                sel = jnp.where(x >= thr, jnp.where(x > low, x, jnp.inf), jnp.inf)
                return jnp.minimum(mn8, fold(jnp.min, sel))

            mn = jnp.min(lax.fori_loop(0, nch, min_body, jnp.full((8, qb), jnp.inf, F32)),
                         axis=0, keepdims=True)

            def eq_body(c, cnt8):
                return cnt8 + fold(jnp.sum, jnp.where(isc_ref[chunk_rows(c), :] == mn, 1, 0).astype(I32))

            n_eq = jnp.sum(lax.fori_loop(0, nch, eq_body, jnp.zeros((8, qb), I32)),
                           axis=0, keepdims=True)
            can = (cnt > k_sel) & (cnt - n_eq >= k_sel)
            stalled = jnp.max(jnp.where(can, 1, 0)) == 0
            return (jnp.where(can, mn, low), jnp.where(can, cnt - n_eq, cnt),
                    jnp.where(stalled, TRIM_PASSES + 1, n + 1))

        low, cnt, n = lax.while_loop(trim_cond, trim_body,
                                     (jnp.full((1, qb), -jnp.inf, F32), cnt0, jnp.int32(0)))
        return thr, low, (n == TRIM_PASSES) & (jnp.max(cnt) > k_sel)

    start = (jnp.zeros((1, qb), I32), jnp.full((1, qb), k_sel, I32))
    tu, cnt_t = lax.fori_loop(0, SELECT_BITS, bis_body, start)
    thr, low, unresolved = trim(tu, cnt_t)
    thr, low = lax.cond(unresolved,
                        lambda: trim(*lax.fori_loop(SELECT_BITS, 32, bis_body, (tu, cnt_t)))[:2],
                        lambda: (thr, low))

    m_ref[...] = jnp.full(m_ref.shape, NEG, F32)
    l_ref[...] = jnp.zeros(l_ref.shape, F32)
    acc_ref[...] = jnp.zeros(acc_ref.shape, F32)

    def scores(c, dst):
        dst[...] = jnp.dot(kcat_ref[chunk_rows(c), :], qcat_ref[0], preferred_element_type=F32)

    def attend(c, src):
        x = isc_ref[chunk_rows(c), :]
        bias = jnp.where(x >= thr, jnp.where(x > low, 0.0, NEG), NEG).astype(F32)
        s = src[...] + jnp.concatenate([bias] * N_HEADS_A, axis=1)
        m_old = m_ref[...]
        m_new = jnp.maximum(m_old, jnp.max(s, axis=0, keepdims=True))
        alpha = jnp.exp2(m_old - m_new)
        p = jnp.exp2(s - m_new)
        l_ref[...] = alpha * l_ref[...] + jnp.sum(p, axis=0, keepdims=True)
        acc_ref[...] = alpha * acc_ref[...] + jnp.dot(ckvt_ref[c], p.astype(BF16),
                                                      preferred_element_type=F32)
        m_ref[...] = m_new

    pipelined(scores, attend, (s0_ref, s1_ref))

    inv_l = 1.0 / l_ref[...]
    outs = []
    for h in range(N_HEADS_A):
        cols = slice(h * qb, (h + 1) * qb)
        o_t = acc_ref[:, cols] * inv_l[:, cols]
        outs.append(jnp.dot(o_t.T.astype(BF16), wuv_ref[h], preferred_element_type=F32))
    o = jnp.concatenate(outs, axis=1)
    g = gate_ref[...].astype(F32)
    o_ref[...] = (o * _silu(g)).astype(BF16)


def _dsa_call(kidx, kcat, ckvt, qcat, qidx, w_t, gates, wuv, batch, seq):
    rows = kidx.shape[0]
    nb = seq // QUERY_BLOCK
    nkc = seq // KEY_CHUNK
    k_sel = min(TOPK_MAX, seq // 4)
    kern = functools.partial(_dsa_kernel, k_sel=k_sel)
    return pl.pallas_call(
        kern,
        grid=(batch, nb),
        in_specs=[pl.BlockSpec((seq, KIDX_PAD), lambda b, i: (b, 0)),
                  pl.BlockSpec((seq, KCAT), lambda b, i: (b, 0)),
                  pl.BlockSpec((nkc, KV_RANK, KEY_CHUNK), lambda b, i: (b, 0, 0)),
                  pl.BlockSpec((1, KCAT, N_HEADS_A * QUERY_BLOCK), lambda b, i: (b * nb + i, 0, 0)),
                  pl.BlockSpec((1, KIDX_PAD, IDX_HEADS * QUERY_BLOCK), lambda b, i: (b * nb + i, 0, 0)),
                  pl.BlockSpec((1, IDX_HEADS, QUERY_BLOCK), lambda b, i: (b * nb + i, 0, 0)),
                  pl.BlockSpec((QUERY_BLOCK, WIDTH_A), lambda b, i: (b * nb + i, 0)),
                  pl.BlockSpec((N_HEADS_A, KV_RANK, HEAD_DIM), lambda b, i: (0, 0, 0))],
        out_specs=pl.BlockSpec((QUERY_BLOCK, WIDTH_A), lambda b, i: (b * nb + i, 0)),
        out_shape=jax.ShapeDtypeStruct((rows, WIDTH_A), BF16),
        scratch_shapes=[pltpu.VMEM((seq, QUERY_BLOCK), F32),
                        pltpu.VMEM((1, N_HEADS_A * QUERY_BLOCK), F32),
                        pltpu.VMEM((1, N_HEADS_A * QUERY_BLOCK), F32),
                        pltpu.VMEM((KV_RANK, N_HEADS_A * QUERY_BLOCK), F32),
                        pltpu.VMEM((KEY_CHUNK, N_HEADS_A * QUERY_BLOCK), F32),
                        pltpu.VMEM((KEY_CHUNK, N_HEADS_A * QUERY_BLOCK), F32)],
        compiler_params=_cparams(("parallel", "arbitrary")),
        name="dsa_attn",
    )(kidx, kcat, ckvt, qcat, qidx, w_t, gates, wuv)


def _dil_kernel(q_ref, k_ref, v_ref, gate_ref, c_ref, s_ref, o_ref,
                nat_ref, a4_ref, qn, kn, vn, q4, k4, v4, q16, k16, v16,
                oc0, ms0, ls0, oc1, ms1, ls1, oc2, ms2, ls2, *, qscale, unroll):
    seq = q_ref.shape[0]
    blk = N_BACK
    rc = ROW_CHUNK
    m4 = seq // 4
    m16 = seq // 16

    def build(src_ref, dn, d4, d16, rope_scale):
        def nat_body(c, carry):
            rows = pl.ds(pl.multiple_of(c * rc, rc), rc)
            x = src_ref[rows, :]
            if rope_scale is not None:
                x = x * c_ref[rows, :] + pltpu.roll(x, LANE // 2, 1) * s_ref[rows, :]
                if rope_scale != 1.0:
                    x = x * rope_scale
                nat_ref[rows, :] = x
            dn[rows, :] = x.astype(BF16)
            return carry

        lax.fori_loop(0, seq // rc, nat_body, 0)
        nat = src_ref if rope_scale is None else nat_ref

        def a4_body(c, carry):
            r4 = c // (m4 // rc)
            j0 = (c % (m4 // rc)) * rc
            x = nat[pl.ds(r4 + 4 * j0, rc, stride=4), :]
            rows = pl.ds(pl.multiple_of(c * rc, rc), rc)
            a4_ref[rows, :] = x
            d4[rows, :] = x.astype(BF16)
            return carry

        lax.fori_loop(0, seq // rc, a4_body, 0)

        def a16_body(r16, carry):
            r4 = r16 % 4
            s = r16 // 4
            x = a4_ref[pl.ds(r4 * m4 + s, m16, stride=4), :]
            d16[pl.ds(pl.multiple_of(r16 * m16, m16), m16), :] = x.astype(BF16)
            return carry

        lax.fori_loop(0, 16, a16_body, 0)

    build(q_ref, qn, q4, q16, qscale)
    build(k_ref, kn, k4, k16, 1.0)
    build(v_ref, vn, v4, v16, None)

    qi = lax.broadcasted_iota(I32, (blk, 2 * blk), 0) + blk
    kj = lax.broadcasted_iota(I32, (blk, 2 * blk), 1)
    rel = qi - kj
    bias_band = jnp.where((rel >= 0) & (rel <= N_BACK), 0.0, NEG).astype(F32)
    no_prev = jnp.where(kj < blk, NEG, 0.0).astype(F32)
    qi1 = lax.broadcasted_iota(I32, (blk, blk), 0)
    kj1 = lax.broadcasted_iota(I32, (blk, blk), 1)
    bias_first = jnp.where(kj1 <= qi1, 0.0, NEG).astype(F32)
    nt = (((1,), (1,)), ((), ()))

    def run_config(qs, ks, vs, nmb, store):
        def rows_of(u, first):
            cur = pl.ds(pl.multiple_of(u * blk, blk), blk)
            if first is True:
                return (cur,)
            return (pl.ds(pl.multiple_of(jnp.maximum(u * blk - blk, 0), blk), blk), cur)

        def gather(ref, rows):
            parts = [ref[r, :] for r in rows]
            return parts[0] if len(parts) == 1 else jnp.concatenate(parts, axis=0)

        def loop_body(g, carry):
            units = []
            for t in range(unroll):
                if nmb <= unroll:
                    first = (t % nmb == 0)
                else:
                    first = jnp.where(g == 0, 1.0, 0.0).astype(F32) if t == 0 else False
                units.append((g * unroll + t, first))
            scores = []
            for u, first in units:
                rows = rows_of(u, first)
                if first is True:
                    bias = bias_first
                elif first is False:
                    bias = bias_band
                else:
                    bias = bias_band + no_prev * first
                scores.append(lax.dot_general(qs[rows[-1], :], gather(ks, rows), nt,
                                              preferred_element_type=F32) + bias)
            probs = []
            for s in scores:
                m = jnp.max(s, axis=-1, keepdims=True)
                p = jnp.exp2(s - m)
                l = jnp.sum(p, axis=-1, keepdims=True)
                probs.append((p.astype(BF16), jnp.broadcast_to(m, (blk, LANE)),
                              jnp.broadcast_to(l, (blk, LANE))))
            for (u, first), (p, m_b, l_b) in zip(units, probs):
                store(u, jnp.dot(p, gather(vs, rows_of(u, first)), preferred_element_type=F32), m_b, l_b)
            return carry

        lax.fori_loop(0, seq // blk // unroll, loop_body, 0)

    def store_contig(oc, ms, ls):
        def store(u, o, m_b, l_b):
            rows = pl.ds(pl.multiple_of(u * blk, blk), blk)
            oc[rows, :] = o
            ms[rows, :] = m_b
            ls[rows, :] = l_b
        return store

    def store_a16_to_a4(u, o, m_b, l_b):
        r16 = u // (m16 // blk)
        mb = u % (m16 // blk)
        rows = pl.ds((r16 % 4) * m4 + 4 * blk * mb + r16 // 4, blk, stride=4)
        oc2[rows, :] = o
        ms2[rows, :] = m_b
        ls2[rows, :] = l_b

    run_config(qn, kn, vn, seq // blk, store_contig(oc0, ms0, ls0))
    run_config(q4, k4, v4, m4 // blk, store_contig(oc1, ms1, ls1))
    run_config(q16, k16, v16, m16 // blk, store_a16_to_a4)

    def merge_body(c, carry):
        r4 = c // (m4 // rc)
        j0 = (c % (m4 // rc)) * rc
        nat_rows = pl.ds(r4 + 4 * j0, rc, stride=4)
        rows = pl.ds(pl.multiple_of(c * rc, rc), rc)
        m0 = ms0[nat_rows, :]
        m1 = ms1[rows, :]
        m2 = ms2[rows, :]
        mx = jnp.maximum(jnp.maximum(m0, m1), m2)
        e0 = jnp.exp2(m0 - mx)
        e1 = jnp.exp2(m1 - mx)
        e2 = jnp.exp2(m2 - mx)
        den = e0 * ls0[nat_rows, :] + e1 * ls1[rows, :] + e2 * ls2[rows, :]
        nat_ref[nat_rows, :] = (e0 * oc0[nat_rows, :] + e1 * oc1[rows, :] + e2 * oc2[rows, :]) / den
        return carry

    lax.fori_loop(0, seq // rc, merge_body, 0)

    def gate_body(c, carry):
        rows = pl.ds(pl.multiple_of(c * rc, rc), rc)
        g = gate_ref[rows, :].astype(F32)
        o_ref[rows, :] = (nat_ref[rows, :] * _silu(g)).astype(BF16)
        return carry

    lax.fori_loop(0, seq // rc, gate_body, 0)


def _dil_call(qkv, gates, ctab, stab, batch, seq):
    rows = qkv.shape[0]
    nh = N_HEADS_B
    qscale = HEAD_DIM ** -0.5 * LOG2E
    kern = functools.partial(_dil_kernel, qscale=qscale, unroll=16)
    blk = (seq, HEAD_DIM)
    return pl.pallas_call(
        kern,
        grid=(batch, nh),
        in_specs=[pl.BlockSpec(blk, lambda b, h: (b, h)),
                  pl.BlockSpec(blk, lambda b, h: (b, nh + h)),
                  pl.BlockSpec(blk, lambda b, h: (b, 2 * nh + h)),
                  pl.BlockSpec(blk, lambda b, h: (b, nh + h)),
                  pl.BlockSpec(blk, lambda b, h: (0, 0), pipeline_mode=pl.Buffered(1)),
                  pl.BlockSpec(blk, lambda b, h: (0, 0), pipeline_mode=pl.Buffered(1))],
        out_specs=pl.BlockSpec(blk, lambda b, h: (b, h)),
        out_shape=jax.ShapeDtypeStruct((rows, WIDTH_B), BF16),
        scratch_shapes=([pltpu.VMEM(blk, F32)] * 2 + [pltpu.VMEM(blk, BF16)] * 9
                        + [pltpu.VMEM(blk, F32)] * 9),
        compiler_params=_cparams(("parallel", "arbitrary")),
        name="dilated",
    )(qkv, qkv, qkv, gates, ctab, stab)


def _out_kernel(oa_ref, ob_ref, w_ref, x_ref, mod_ref, g_ref, o_ref):
    y = jnp.dot(oa_ref[...], w_ref[0:WIDTH_A, :], preferred_element_type=F32)
    y = y + jnp.dot(ob_ref[...], w_ref[WIDTH_A:WIDTH_A + WIDTH_B, :], preferred_element_type=F32)
    yn = y * lax.rsqrt(jnp.mean(y * y, axis=-1, keepdims=True) + EPS) * g_ref[...]
    o_ref[...] = x_ref[...] + mod_ref[0, 2:3, :] * yn


def _out_call(o_a, o_b, w_out, x2, mod3, g_post, seq):
    rows, d = x2.shape
    tm = OUT_TM
    tpb = seq // tm
    return pl.pallas_call(
        _out_kernel,
        grid=(rows // tm,),
        in_specs=[pl.BlockSpec((tm, WIDTH_A), lambda i: (i, 0)),
                  pl.BlockSpec((tm, WIDTH_B), lambda i: (i, 0)),
                  pl.BlockSpec((WIDTH_A + WIDTH_B, d), lambda i: (0, 0)),
                  pl.BlockSpec((tm, d), lambda i: (i, 0)),
                  pl.BlockSpec((1, 3, d), lambda i: (i // tpb, 0, 0)),
                  pl.BlockSpec((1, d), lambda i: (0, 0))],
        out_specs=pl.BlockSpec((tm, d), lambda i: (i, 0)),
        out_shape=jax.ShapeDtypeStruct((rows, d), F32),
        compiler_params=_cparams(("parallel",)),
        name="out_proj",
    )(o_a, o_b, w_out, x2, mod3, g_post)


def _rope_tables(seq):
    pos = jnp.arange(seq, dtype=F32)

    def cs(n_rot):
        inv = ROPE_THETA ** (-jnp.arange(0, n_rot, 2, dtype=F32) / n_rot)
        ang = pos[:, None] * inv[None, :]
        return jnp.cos(ang), jnp.sin(ang)

    c32, s32 = cs(ROPE_DIM)
    c16, s16 = cs(IDX_ROPE)
    ck = jnp.concatenate([c32, c32, jnp.ones((seq, LANE - ROPE_DIM), F32)], axis=1)
    sk = jnp.concatenate([-s32, s32, jnp.zeros((seq, LANE - ROPE_DIM), F32)], axis=1)
    ci = jnp.concatenate([c16, c16, jnp.ones((seq, LANE - IDX_ROPE), F32)], axis=1)
    si = jnp.concatenate([-s16, s16, jnp.zeros((seq, LANE - IDX_ROPE), F32)], axis=1)
    gap = jnp.ones((seq, LANE // 2 - ROPE_DIM // 2), F32)
    cd = jnp.concatenate([c32, gap, c32, gap], axis=1)
    sd = jnp.concatenate([-s32, 0.0 * gap, s32, 0.0 * gap], axis=1)
    return (c32.T, s32.T, c16.T, s16.T, ck, sk, ci, si, cd, sd)


def _pair_heads(wt):
    half = ROPE_DIM // 2
    mid = ROPE_DIM + (LANE // 2 - half)
    w3 = wt.reshape(-1, HEAD_DIM, wt.shape[-1])
    w3 = jnp.concatenate([w3[:, :half], w3[:, ROPE_DIM:mid], w3[:, half:ROPE_DIM], w3[:, mid:]], axis=1)
    return w3.reshape(wt.shape)


def _layer(x, c, w_ada, b_ada, g_pre, g_post, w_in, g_q, g_kv, w_uq, w_uq_idx, w_uk, w_uv, w_out):
    batch, seq, d = x.shape
    assert seq % PROJ_TM == 0 or seq < PROJ_TM
    assert seq % KEY_CHUNK == 0 and seq // 16 >= N_BACK
    assert all(w // dl == N_BACK for w, dl in DILATED_CONFIGS)
    rows = batch * seq
    x2 = x.reshape(rows, d)

    o_cq, o_ckv, o_kr, o_ki, o_wi = 0, 512, 768, 800, 864
    o_ga, o_q, o_k, o_v, o_gb = 880, 1904, 2928, 3952, 4976
    wt = w_in.T.astype(BF16)
    zeros = lambda n: jnp.zeros((n, d), BF16)
    w_all = jnp.concatenate([
        wt[o_cq:o_ki], zeros(896 - 800), wt[o_ki:o_ga], zeros(SMALL_W - 976),
        _pair_heads(wt[o_q:o_k]), _pair_heads(wt[o_k:o_v]), wt[o_v:o_gb],
        wt[o_ga:o_q], wt[o_gb:o_gb + WIDTH_B]], axis=0)
    assert w_all.shape[0] == IN_PAD

    c_pad = jnp.zeros((PACK16, d), F32).at[:batch].set(c)
    mod = _mod_call(c_pad, w_ada, b_ada.reshape(1, -1))[:batch]
    mod3 = mod.reshape(batch, 3, d)

    small, qkv, gates = _proj_call(x2, mod3, g_pre.reshape(1, d), w_all, seq)

    tabs = _rope_tables(seq)
    wuk_t = jnp.transpose(jnp.pad(w_uk, ((0, 0), (ROPE_DIM, 0), (0, 0))), (0, 2, 1)).astype(BF16)
    qcat, qidx, w_t, kcat, kidx, ckvt = _prep_call(
        small, g_q.reshape(1, -1), g_kv.reshape(1, -1), w_uq.T.astype(BF16),
        w_uq_idx.T.astype(BF16), wuk_t, tabs[:8], seq)

    o_a = _dsa_call(kidx, kcat, ckvt, qcat, qidx, w_t, gates, w_uv.astype(BF16), batch, seq)
    o_b = _dil_call(qkv, gates, tabs[8], tabs[9], batch, seq)
    out = _out_call(o_a, o_b, w_out.astype(BF16), x2, mod3, g_post.reshape(1, d), seq)
    return out.reshape(batch, seq, d)


def kernel(x, c, w_ada, b_ada, g_pre, g_post, w_in, g_q, g_kv, w_uq, w_uq_idx, w_uk, w_uv, w_out):
    for layer in range(w_ada.shape[0]):
        x = _layer(x, c, w_ada[layer], b_ada[layer], g_pre[layer], g_post[layer], w_in[layer],
                   g_q[layer], g_kv[layer], w_uq[layer], w_uq_idx[layer], w_uk[layer],
                   w_uv[layer], w_out[layer])
    return x
```

```python
import functools
import math

import numpy as np
import jax
import jax.numpy as jnp
from jax import lax
from jax.experimental import pallas as pl
from jax.experimental.pallas import tpu as pltpu

F32 = jnp.float32
BF16 = jnp.bfloat16
I32 = jnp.int32

HEAD_DIM = 128
ROPE_DIM = HEAD_DIM // 4
ROPE_THETA = 500000.0
EPS = 1e-6
NEG = -1e30
N_HEADS_A = 8
WIDTH_A = N_HEADS_A * HEAD_DIM
Q_RANK = 512
KV_RANK = 256
IDX_HEADS = 16
IDX_DIM = 64
IDX_ROPE = IDX_DIM // 4
TOPK_MAX = 256
QUERY_BLOCK = 128
N_HEADS_B = 8
WIDTH_B = N_HEADS_B * HEAD_DIM
DILATED_CONFIGS = ((128, 1), (512, 4), (2048, 16))
N_BACK = 128
SMALL_W = 1024
IN_PAD = SMALL_W + 5 * 1024
KCAT = 288
KIDX_PAD = 64
INT_MIN = -(2 ** 31)
LOG2E = 1.4426950408889634

LANE = 128
PACK16 = 16
VMEM_LIMIT = 56 * 1024 * 1024
PROJ_TM = 1024
PROJ_TN = 1024
PREP_TQ = 512
KEY_CHUNK = 512
OUT_TM = 512
ROW_CHUNK = 512
NORM_ROWS = 128


def _cparams(sem):
    return pltpu.CompilerParams(dimension_semantics=sem, vmem_limit_bytes=VMEM_LIMIT)


def _silu(g):
    return g * jax.nn.sigmoid(g)


def _split_bf16(x):
    hi = x.astype(BF16)
    return hi, (x - hi.astype(F32)).astype(BF16)


def _mod_kernel(c_ref, w_ref, b_ref, o_ref):
    a_hi, a_lo = _split_bf16(_silu(c_ref[...]))
    w_hi, w_lo = _split_bf16(w_ref[...])
    dot = functools.partial(jnp.dot, preferred_element_type=F32)
    o_ref[...] = dot(a_hi, w_hi) + (dot(a_hi, w_lo) + dot(a_lo, w_hi)) + b_ref[...]


def _mod_call(c_pad, w_ada, b_ada):
    rows, d = c_pad.shape
    n = w_ada.shape[1]
    tn = 1024
    return pl.pallas_call(
        _mod_kernel,
        grid=(n // tn,),
        in_specs=[pl.BlockSpec((rows, d), lambda j: (0, 0)),
                  pl.BlockSpec((d, tn), lambda j: (0, j)),
                  pl.BlockSpec((1, tn), lambda j: (0, j))],
        out_specs=pl.BlockSpec((rows, tn), lambda j: (0, j)),
        out_shape=jax.ShapeDtypeStruct((rows, n), F32),
        compiler_params=_cparams(("arbitrary",)),
        name="mod",
    )(c_pad, w_ada, b_ada)


def _proj_kernel(x_ref, mod_ref, g_ref, w_ref, small_ref, qkv_ref, gates_ref, h_ref, *,
                 n_small, n_qkv):
    j = pl.program_id(1)
    tm = x_ref.shape[0]
    half = tm // 2
    assert n_small == 1 and half % NORM_ROWS == 0

    def project(rows=slice(None)):
        return lax.dot_general(h_ref[rows, :], w_ref[...], (((1,), (1,)), ((), ())),
                               preferred_element_type=F32)

    @pl.when(j == 0)
    def _():
        shift = mod_ref[0, 0:1, :]
        scale1 = 1.0 + mod_ref[0, 1:2, :]
        g = g_ref[...]

        def norm_rows(rows):
            x = x_ref[rows, :]
            ms = jnp.mean(x * x, axis=-1, keepdims=True)
            y = x * lax.rsqrt(ms + EPS) * g
            h_ref[rows, :] = (y * scale1 + shift).astype(BF16)

        def body(r, carry):
            norm_rows(pl.ds(pl.multiple_of(r * NORM_ROWS, NORM_ROWS), NORM_ROWS))
            return carry

        lax.fori_loop(0, half // NORM_ROWS, body, 0)
        small_ref[0:half, :] = project(slice(0, half))
        for r in range(half // NORM_ROWS):
            norm_rows(slice(half + r * NORM_ROWS, half + (r + 1) * NORM_ROWS))
        small_ref[half:tm, :] = project(slice(half, tm))

    @pl.when((j >= n_small) & (j < n_small + n_qkv))
    def _():
        qkv_ref[...] = project()

    @pl.when(j >= n_small + n_qkv)
    def _():
        gates_ref[...] = project().astype(BF16)


def _proj_call(x2, mod3, g_pre, w_all, seq):
    rows, d = x2.shape
    tm = min(PROJ_TM, seq)
    tn = PROJ_TN
    n_small = SMALL_W // tn
    n_qkv = 3 * WIDTH_B // tn
    n_gate = (WIDTH_A + WIDTH_B) // tn
    tiles_per_batch = seq // tm
    kern = functools.partial(_proj_kernel, n_small=n_small, n_qkv=n_qkv)
    return pl.pallas_call(
        kern,
        grid=(rows // tm, n_small + n_qkv + n_gate),
        in_specs=[pl.BlockSpec((tm, d), lambda i, j: (i, 0)),
                  pl.BlockSpec((1, 3, d), lambda i, j: (i // tiles_per_batch, 0, 0)),
                  pl.BlockSpec((1, d), lambda i, j: (0, 0)),
                  pl.BlockSpec((tn, d), lambda i, j: (j, 0))],
        out_specs=[pl.BlockSpec((tm, tn), lambda i, j: (i, jnp.minimum(j, n_small - 1))),
                   pl.BlockSpec((tm, tn), lambda i, j: (i, jnp.clip(j - n_small, 0, n_qkv - 1))),
                   pl.BlockSpec((tm, tn),
                                lambda i, j: (i, jnp.clip(j - n_small - n_qkv, 0, n_gate - 1)))],
        out_shape=[jax.ShapeDtypeStruct((rows, SMALL_W), F32),
                   jax.ShapeDtypeStruct((rows, 3 * WIDTH_B), F32),
                   jax.ShapeDtypeStruct((rows, WIDTH_A + WIDTH_B), BF16)],
        scratch_shapes=[pltpu.VMEM((tm, d), BF16)],
        compiler_params=_cparams(("parallel", "arbitrary")),
        name="in_proj",
    )(x2, mod3, g_pre, w_all)


def _prep_kernel(small_ref, gq_ref, gkv_ref, wuq_ref, wuqi_ref, wuk_ref,
                 cosq_ref, sinq_ref, cosi_ref, sini_ref, ck_ref, sk_ref, ci_ref, si_ref,
                 qcat_ref, qidx_ref, wt_ref, kcat_ref, kidx_ref, ckvt_ref, *, qscale):
    tq = small_ref.shape[0]
    nblk = tq // QUERY_BLOCK
    cq = small_ref[:, 0:Q_RANK]
    cqn = (cq * lax.rsqrt(jnp.mean(cq * cq, axis=-1, keepdims=True) + EPS)
           * gq_ref[...]).astype(BF16)
    nt = (((1,), (1,)), ((), ()))

    q_t = lax.dot_general(wuq_ref[...], cqn, nt, preferred_element_type=F32)
    cosq = cosq_ref[...]
    sinq = sinq_ref[...]
    half = ROPE_DIM // 2
    zpad = [jnp.zeros((KCAT - KV_RANK - ROPE_DIM, tq), F32)] if KCAT > KV_RANK + ROPE_DIM else []
    for h in range(N_HEADS_A):
        base = h * HEAD_DIM
        x1 = q_t[base:base + half]
        x2 = q_t[base + half:base + ROPE_DIM]
        q_rope = jnp.concatenate([x1 * cosq - x2 * sinq, x2 * cosq + x1 * sinq] + zpad, axis=0)
        q_lat = jnp.dot(wuk_ref[h], q_t[base:base + HEAD_DIM].astype(BF16),
                        preferred_element_type=F32)
        q_lat = (q_lat * qscale).astype(BF16)
        q_rope = (q_rope * qscale).astype(BF16)
        for blk in range(nblk):
            cols = slice(blk * QUERY_BLOCK, (blk + 1) * QUERY_BLOCK)
            lanes = slice(h * QUERY_BLOCK, (h + 1) * QUERY_BLOCK)
            qcat_ref[blk, 0:KV_RANK, lanes] = q_lat[:, cols]
            qcat_ref[blk, KV_RANK:KCAT, lanes] = q_rope[:, cols]

    qi_t = lax.dot_general(wuqi_ref[...], cqn, nt, preferred_element_type=F32)
    cosi = cosi_ref[...]
    sini = sini_ref[...]
    ihalf = IDX_ROPE // 2
    ipad = [jnp.zeros((KIDX_PAD - IDX_DIM, tq), F32)] if KIDX_PAD > IDX_DIM else []
    for h in range(IDX_HEADS):
        base = h * IDX_DIM
        x1 = qi_t[base:base + ihalf]
        x2 = qi_t[base + ihalf:base + IDX_ROPE]
        qi = jnp.concatenate([x1 * cosi - x2 * sini, x2 * cosi + x1 * sini,
                              qi_t[base + IDX_ROPE:base + IDX_DIM]] + ipad, axis=0).astype(BF16)
        for blk in range(nblk):
            cols = slice(blk * QUERY_BLOCK, (blk + 1) * QUERY_BLOCK)
            qidx_ref[blk, :, h * QUERY_BLOCK:(h + 1) * QUERY_BLOCK] = qi[:, cols]

    lane = lax.broadcasted_iota(I32, (tq, LANE), 1)

    slab_b = small_ref[:, 896:1024]
    swap_b = jnp.where(lane < ihalf, pltpu.roll(slab_b, LANE - ihalf, 1), pltpu.roll(slab_b, ihalf, 1))
    kidx_ref[...] = (slab_b * ci_ref[...] + swap_b * si_ref[...])[:, :KIDX_PAD].astype(BF16)
    w_t = slab_b.T[IDX_DIM:IDX_DIM + IDX_HEADS] * (IDX_HEADS ** -0.5 * IDX_DIM ** -0.5)
    for blk in range(nblk):
        wt_ref[blk] = w_t[:, blk * QUERY_BLOCK:(blk + 1) * QUERY_BLOCK]

    ckv = small_ref[:, Q_RANK:Q_RANK + KV_RANK]
    cn = ckv * lax.rsqrt(jnp.mean(ckv * ckv, axis=-1, keepdims=True) + EPS) * gkv_ref[...]
    kcat_ref[:, 0:KV_RANK] = cn.astype(BF16)
    ckvt_ref[0] = cn.T.astype(BF16)
    slab_a = small_ref[:, 768:896]
    swap_a = jnp.where(lane < half, pltpu.roll(slab_a, LANE - half, 1), pltpu.roll(slab_a, half, 1))
    kcat_ref[:, KV_RANK:KCAT] = (slab_a * ck_ref[...] + swap_a * sk_ref[...])[:, :KCAT - KV_RANK].astype(BF16)


def _prep_call(small, g_q, g_kv, wuq_t, wuqi_t, wuk_t, tabs, seq):
    rows = small.shape[0]
    tq = PREP_TQ
    nblk = tq // QUERY_BLOCK
    tpb = seq // tq
    nqb = rows // QUERY_BLOCK
    qscale = HEAD_DIM ** -0.5 * LOG2E
    kern = functools.partial(_prep_kernel, qscale=qscale)
    const = lambda t: (0, 0)
    tcol = lambda t: (0, t % tpb)
    trow = lambda t: (t % tpb, 0)
    return pl.pallas_call(
        kern,
        grid=(rows // tq,),
        in_specs=[pl.BlockSpec((tq, SMALL_W), lambda t: (t, 0)),
                  pl.BlockSpec((1, Q_RANK), const),
                  pl.BlockSpec((1, KV_RANK), const),
                  pl.BlockSpec((WIDTH_A, Q_RANK), const),
                  pl.BlockSpec((IDX_HEADS * IDX_DIM, Q_RANK), const),
                  pl.BlockSpec((N_HEADS_A, KV_RANK, HEAD_DIM), lambda t: (0, 0, 0)),
                  pl.BlockSpec((ROPE_DIM // 2, tq), tcol),
                  pl.BlockSpec((ROPE_DIM // 2, tq), tcol),
                  pl.BlockSpec((IDX_ROPE // 2, tq), tcol),
                  pl.BlockSpec((IDX_ROPE // 2, tq), tcol),
                  pl.BlockSpec((tq, LANE), trow),
                  pl.BlockSpec((tq, LANE), trow),
                  pl.BlockSpec((tq, LANE), trow),
                  pl.BlockSpec((tq, LANE), trow)],
        out_specs=[pl.BlockSpec((nblk, KCAT, N_HEADS_A * QUERY_BLOCK), lambda t: (t, 0, 0)),
                   pl.BlockSpec((nblk, KIDX_PAD, IDX_HEADS * QUERY_BLOCK), lambda t: (t, 0, 0)),
                   pl.BlockSpec((nblk, IDX_HEADS, QUERY_BLOCK), lambda t: (t, 0, 0)),
                   pl.BlockSpec((tq, KCAT), lambda t: (t, 0)),
                   pl.BlockSpec((tq, KIDX_PAD), lambda t: (t, 0)),
                   pl.BlockSpec((1, KV_RANK, tq), lambda t: (t, 0, 0))],
        out_shape=[jax.ShapeDtypeStruct((nqb, KCAT, N_HEADS_A * QUERY_BLOCK), BF16),
                   jax.ShapeDtypeStruct((nqb, KIDX_PAD, IDX_HEADS * QUERY_BLOCK), BF16),
                   jax.ShapeDtypeStruct((nqb, IDX_HEADS, QUERY_BLOCK), F32),
                   jax.ShapeDtypeStruct((rows, KCAT), BF16),
                   jax.ShapeDtypeStruct((rows, KIDX_PAD), BF16),
                   jax.ShapeDtypeStruct((rows // tq, KV_RANK, tq), BF16)],
        compiler_params=_cparams(("parallel",)),
        name="dsa_prep",
    )(small, g_q, g_kv, wuq_t, wuqi_t, wuk_t, *tabs)


def _float_of_rank(u):
    key = u ^ INT_MIN
    return lax.bitcast_convert_type(key ^ (lax.shift_right_arithmetic(key, 31) & 0x7FFFFFFF), F32)


_ABOVE_NEG = float(np.nextafter(np.float32(NEG), np.float32(0.0)))
TRIM_PASSES = 8
SELECT_BITS = 24


def _dsa_kernel(kidx_ref, kcat_ref, ckvt_ref, qcat_ref, qidx_ref, wt_ref, gate_ref, wuv_ref,
                o_ref, isc_ref, m_ref, l_ref, acc_ref, s0_ref, s1_ref, *, k_sel):
    i = pl.program_id(1)
    kc = KEY_CHUNK
    qb = QUERY_BLOCK
    per = kc // qb
    nch = (i + per) // per
    qpos = i * qb + lax.broadcasted_iota(I32, (kc, qb), 1)
    krow = lax.broadcasted_iota(I32, (kc, qb), 0)

    def chunk_rows(c):
        return pl.ds(c * kc if isinstance(c, int) else pl.multiple_of(c * kc, kc), kc)

    def fold(op, x):
        return op(x.reshape(kc // 8, 8, qb), axis=0)

    def pipelined(produce, consume, bufs):
        produce(0, bufs[0])

        def body(j, carry):
            produce(2 * j + 1, bufs[1])
            consume(2 * j, bufs[0])
            produce(2 * j + 2, bufs[0])
            consume(2 * j + 1, bufs[1])
            return carry

        lax.fori_loop(0, (nch - 1) // 2, body, 0)

        @pl.when(nch % 2 == 0)
        def _():
            produce(nch - 1, bufs[1])
            consume(nch - 2, bufs[0])
            consume(nch - 1, bufs[1])

        @pl.when(nch % 2 == 1)
        def _():
            consume(nch - 1, bufs[0])

    def index_chunk(c):
        kblk = kidx_ref[chunk_rows(c), :]
        acc = jnp.zeros((kc, qb), F32)
        group = 4
        for g in range(IDX_HEADS // group):
            lg = jnp.dot(kblk, qidx_ref[0, :, g * group * qb:(g + 1) * group * qb],
                         preferred_element_type=F32)
            for hh in range(group):
                h = g * group + hh
                acc = acc + jnp.maximum(lg[:, hh * qb:(hh + 1) * qb], 0.0) * wt_ref[0, h:h + 1, :]
        causal = (c * kc + krow) <= qpos
        isc_ref[chunk_rows(c), :] = jnp.where(causal, acc, NEG)

    def idx_body(j, carry):
        index_chunk(2 * j)
        index_chunk(2 * j + 1)
        return carry

    lax.fori_loop(0, nch // 2, idx_body, 0)

    @pl.when(nch % 2 == 1)
    def _():
        index_chunk(nch - 1)

    def count_ge(t):
        def chunk_count(c):
            return fold(jnp.sum, jnp.where(isc_ref[chunk_rows(c), :] >= t, 1, 0).astype(I32))

        def counted(n):
            def run():
                parts = [chunk_count(c) for c in range(n)]
                while len(parts) > 1:
                    parts = [a + b for a, b in zip(parts[::2], parts[1::2])] + parts[len(parts) & ~1:]
                return parts[0]
            return run

        cnt8 = lax.switch(nch - 1, [counted(n) for n in range(1, isc_ref.shape[0] // kc + 1)])
        return jnp.sum(cnt8, axis=0, keepdims=True)

    def bis_body(s, carry):
        tu, cnt_t = carry
        cand_u = tu | lax.shift_left(jnp.int32(1), 31 - s)
        cnt = count_ge(_float_of_rank(cand_u))
        ok = cnt >= k_sel
        return jnp.where(ok, cand_u, tu), jnp.where(ok, cnt, cnt_t)

    def trim(tu, cnt_t):
        thr = jnp.where(tu == 0, -jnp.inf, _float_of_rank(tu))
        few = thr < _ABOVE_NEG
        thr = jnp.where(few, _ABOVE_NEG, thr)
        cnt0 = jnp.where(few, k_sel, cnt_t)

        def trim_cond(state):
            _, cnt, n = state
            return (jnp.max(cnt) > k_sel) & (n < TRIM_PASSES)

        def trim_body(state):
            low, cnt, n = state

            def min_body(c, mn8):
                x = isc_ref[chunk_rows(c), :]
                sel = jnp.where(x >= thr, jnp.where(x > low, x, jnp.inf), jnp.inf)
                return jnp.minimum(mn8, fold(jnp.min, sel))

            mn = jnp.min(lax.fori_loop(0, nch, min_body, jnp.full((8, qb), jnp.inf, F32)),
                         axis=0, keepdims=True)

            def eq_body(c, cnt8):
                return cnt8 + fold(jnp.sum, jnp.where(isc_ref[chunk_rows(c), :] == mn, 1, 0).astype(I32))

            n_eq = jnp.sum(lax.fori_loop(0, nch, eq_body, jnp.zeros((8, qb), I32)),
                           axis=0, keepdims=True)
            can = (cnt > k_sel) & (cnt - n_eq >= k_sel)
            stalled = jnp.max(jnp.where(can, 1, 0)) == 0
            return (jnp.where(can, mn, low), jnp.where(can, cnt - n_eq, cnt),
                    jnp.where(stalled, TRIM_PASSES + 1, n + 1))

        low, cnt, n = lax.while_loop(trim_cond, trim_body,
                                     (jnp.full((1, qb), -jnp.inf, F32), cnt0, jnp.int32(0)))
        return thr, low, (n == TRIM_PASSES) & (jnp.max(cnt) > k_sel)

    start = (jnp.zeros((1, qb), I32), jnp.full((1, qb), k_sel, I32))
    tu, cnt_t = lax.fori_loop(0, SELECT_BITS, bis_body, start)
    thr, low, unresolved = trim(tu, cnt_t)
    thr, low = lax.cond(unresolved,
                        lambda: trim(*lax.fori_loop(SELECT_BITS, 32, bis_body, (tu, cnt_t)))[:2],
                        lambda: (thr, low))

    m_ref[...] = jnp.full(m_ref.shape, NEG, F32)
    l_ref[...] = jnp.zeros(l_ref.shape, F32)
    acc_ref[...] = jnp.zeros(acc_ref.shape, F32)

    def scores(c, dst):
        dst[...] = jnp.dot(kcat_ref[chunk_rows(c), :], qcat_ref[0], preferred_element_type=F32)

    def attend(c, src):
        x = isc_ref[chunk_rows(c), :]
        bias = jnp.where(x >= thr, jnp.where(x > low, 0.0, NEG), NEG).astype(F32)
        s = src[...] + jnp.concatenate([bias] * N_HEADS_A, axis=1)
        m_old = m_ref[...]
        m_new = jnp.maximum(m_old, jnp.max(s, axis=0, keepdims=True))
        alpha = jnp.exp2(m_old - m_new)
        p = jnp.exp2(s - m_new)
        l_ref[...] = alpha * l_ref[...] + jnp.sum(p, axis=0, keepdims=True)
        acc_ref[...] = alpha * acc_ref[...] + jnp.dot(ckvt_ref[c], p.astype(BF16),
                                                      preferred_element_type=F32)
        m_ref[...] = m_new

    pipelined(scores, attend, (s0_ref, s1_ref))

    inv_l = 1.0 / l_ref[...]
    outs = []
    for h in range(N_HEADS_A):
        cols = slice(h * qb, (h + 1) * qb)
        o_t = acc_ref[:, cols] * inv_l[:, cols]
        outs.append(jnp.dot(o_t.T.astype(BF16), wuv_ref[h], preferred_element_type=F32))
    o = jnp.concatenate(outs, axis=1)
    g = gate_ref[...].astype(F32)
    o_ref[...] = (o * _silu(g)).astype(BF16)


def _dsa_call(kidx, kcat, ckvt, qcat, qidx, w_t, gates, wuv, batch, seq):
    rows = kidx.shape[0]
    nb = seq // QUERY_BLOCK
    nkc = seq // KEY_CHUNK
    k_sel = min(TOPK_MAX, seq // 4)
    kern = functools.partial(_dsa_kernel, k_sel=k_sel)
    return pl.pallas_call(
        kern,
        grid=(batch, nb),
        in_specs=[pl.BlockSpec((seq, KIDX_PAD), lambda b, i: (b, 0)),
                  pl.BlockSpec((seq, KCAT), lambda b, i: (b, 0)),
                  pl.BlockSpec((nkc, KV_RANK, KEY_CHUNK), lambda b, i: (b, 0, 0)),
                  pl.BlockSpec((1, KCAT, N_HEADS_A * QUERY_BLOCK), lambda b, i: (b * nb + i, 0, 0)),
                  pl.BlockSpec((1, KIDX_PAD, IDX_HEADS * QUERY_BLOCK), lambda b, i: (b * nb + i, 0, 0)),
                  pl.BlockSpec((1, IDX_HEADS, QUERY_BLOCK), lambda b, i: (b * nb + i, 0, 0)),
                  pl.BlockSpec((QUERY_BLOCK, WIDTH_A), lambda b, i: (b * nb + i, 0)),
                  pl.BlockSpec((N_HEADS_A, KV_RANK, HEAD_DIM), lambda b, i: (0, 0, 0))],
        out_specs=pl.BlockSpec((QUERY_BLOCK, WIDTH_A), lambda b, i: (b * nb + i, 0)),
        out_shape=jax.ShapeDtypeStruct((rows, WIDTH_A), BF16),
        scratch_shapes=[pltpu.VMEM((seq, QUERY_BLOCK), F32),
                        pltpu.VMEM((1, N_HEADS_A * QUERY_BLOCK), F32),
                        pltpu.VMEM((1, N_HEADS_A * QUERY_BLOCK), F32),
                        pltpu.VMEM((KV_RANK, N_HEADS_A * QUERY_BLOCK), F32),
                        pltpu.VMEM((KEY_CHUNK, N_HEADS_A * QUERY_BLOCK), F32),
                        pltpu.VMEM((KEY_CHUNK, N_HEADS_A * QUERY_BLOCK), F32)],
        compiler_params=_cparams(("parallel", "arbitrary")),
        name="dsa_attn",
    )(kidx, kcat, ckvt, qcat, qidx, w_t, gates, wuv)


def _dil_kernel(q_ref, k_ref, v_ref, gate_ref, c_ref, s_ref, o_ref,
                nat_ref, a4_ref, qn, kn, vn, q4, k4, v4, q16, k16, v16,
                oc0, ms0, ls0, oc1, ms1, ls1, oc2, ms2, ls2, *, qscale, unroll):
    seq = q_ref.shape[0]
    blk = N_BACK
    rc = ROW_CHUNK
    m4 = seq // 4
    m16 = seq // 16

    def build(src_ref, dn, d4, d16, rope_scale):
        def nat_body(c, carry):
            rows = pl.ds(pl.multiple_of(c * rc, rc), rc)
            x = src_ref[rows, :]
            if rope_scale is not None:
                x = x * c_ref[rows, :] + pltpu.roll(x, LANE // 2, 1) * s_ref[rows, :]
                if rope_scale != 1.0:
                    x = x * rope_scale
                nat_ref[rows, :] = x
            dn[rows, :] = x.astype(BF16)
            return carry

        lax.fori_loop(0, seq // rc, nat_body, 0)
        nat = src_ref if rope_scale is None else nat_ref

        def a4_body(c, carry):
            r4 = c // (m4 // rc)
            j0 = (c % (m4 // rc)) * rc
            x = nat[pl.ds(r4 + 4 * j0, rc, stride=4), :]
            rows = pl.ds(pl.multiple_of(c * rc, rc), rc)
            a4_ref[rows, :] = x
            d4[rows, :] = x.astype(BF16)
            return carry

        lax.fori_loop(0, seq // rc, a4_body, 0)

        def a16_body(r16, carry):
            r4 = r16 % 4
            s = r16 // 4
            x = a4_ref[pl.ds(r4 * m4 + s, m16, stride=4), :]
            d16[pl.ds(pl.multiple_of(r16 * m16, m16), m16), :] = x.astype(BF16)
            return carry

        lax.fori_loop(0, 16, a16_body, 0)

    build(q_ref, qn, q4, q16, qscale)
    build(k_ref, kn, k4, k16, 1.0)
    build(v_ref, vn, v4, v16, None)

    qi = lax.broadcasted_iota(I32, (blk, 2 * blk), 0) + blk
    kj = lax.broadcasted_iota(I32, (blk, 2 * blk), 1)
    rel = qi - kj
    bias_band = jnp.where((rel >= 0) & (rel <= N_BACK), 0.0, NEG).astype(F32)
    no_prev = jnp.where(kj < blk, NEG, 0.0).astype(F32)
    qi1 = lax.broadcasted_iota(I32, (blk, blk), 0)
    kj1 = lax.broadcasted_iota(I32, (blk, blk), 1)
    bias_first = jnp.where(kj1 <= qi1, 0.0, NEG).astype(F32)
    nt = (((1,), (1,)), ((), ()))

    def run_config(qs, ks, vs, nmb, store):
        def rows_of(u, first):
            cur = pl.ds(pl.multiple_of(u * blk, blk), blk)
            if first is True:
                return (cur,)
            return (pl.ds(pl.multiple_of(jnp.maximum(u * blk - blk, 0), blk), blk), cur)

        def gather(ref, rows):
            parts = [ref[r, :] for r in rows]
            return parts[0] if len(parts) == 1 else jnp.concatenate(parts, axis=0)

        def loop_body(g, carry):
            units = []
            for t in range(unroll):
                if nmb <= unroll:
                    first = (t % nmb == 0)
                else:
                    first = jnp.where(g == 0, 1.0, 0.0).astype(F32) if t == 0 else False
                units.append((g * unroll + t, first))
            scores = []
            for u, first in units:
                rows = rows_of(u, first)
                if first is True:
                    bias = bias_first
                elif first is False:
                    bias = bias_band
                else:
                    bias = bias_band + no_prev * first
                scores.append(lax.dot_general(qs[rows[-1], :], gather(ks, rows), nt,
                                              preferred_element_type=F32) + bias)
            probs = []
            for s in scores:
                m = jnp.max(s, axis=-1, keepdims=True)
                p = jnp.exp2(s - m)
                l = jnp.sum(p, axis=-1, keepdims=True)
                probs.append((p.astype(BF16), jnp.broadcast_to(m, (blk, LANE)),
                              jnp.broadcast_to(l, (blk, LANE))))
            for (u, first), (p, m_b, l_b) in zip(units, probs):
                store(u, jnp.dot(p, gather(vs, rows_of(u, first)), preferred_element_type=F32), m_b, l_b)
            return carry

        lax.fori_loop(0, seq // blk // unroll, loop_body, 0)

    def store_contig(oc, ms, ls):
        def store(u, o, m_b, l_b):
            rows = pl.ds(pl.multiple_of(u * blk, blk), blk)
            oc[rows, :] = o
            ms[rows, :] = m_b
            ls[rows, :] = l_b
        return store

    def store_a16_to_a4(u, o, m_b, l_b):
        r16 = u // (m16 // blk)
        mb = u % (m16 // blk)
        rows = pl.ds((r16 % 4) * m4 + 4 * blk * mb + r16 // 4, blk, stride=4)
        oc2[rows, :] = o
        ms2[rows, :] = m_b
        ls2[rows, :] = l_b

    run_config(qn, kn, vn, seq // blk, store_contig(oc0, ms0, ls0))
    run_config(q4, k4, v4, m4 // blk, store_contig(oc1, ms1, ls1))
    run_config(q16, k16, v16, m16 // blk, store_a16_to_a4)

    def merge_body(c, carry):
        r4 = c // (m4 // rc)
        j0 = (c % (m4 // rc)) * rc
        nat_rows = pl.ds(r4 + 4 * j0, rc, stride=4)
        rows = pl.ds(pl.multiple_of(c * rc, rc), rc)
        m0 = ms0[nat_rows, :]
        m1 = ms1[rows, :]
        m2 = ms2[rows, :]
        mx = jnp.maximum(jnp.maximum(m0, m1), m2)
        e0 = jnp.exp2(m0 - mx)
        e1 = jnp.exp2(m1 - mx)
        e2 = jnp.exp2(m2 - mx)
        den = e0 * ls0[nat_rows, :] + e1 * ls1[rows, :] + e2 * ls2[rows, :]
        nat_ref[nat_rows, :] = (e0 * oc0[nat_rows, :] + e1 * oc1[rows, :] + e2 * oc2[rows, :]) / den
        return carry

    lax.fori_loop(0, seq // rc, merge_body, 0)

    def gate_body(c, carry):
        rows = pl.ds(pl.multiple_of(c * rc, rc), rc)
        g = gate_ref[rows, :].astype(F32)
        o_ref[rows, :] = (nat_ref[rows, :] * _silu(g)).astype(BF16)
        return carry

    lax.fori_loop(0, seq // rc, gate_body, 0)


def _dil_call(qkv, gates, ctab, stab, batch, seq):
    rows = qkv.shape[0]
    nh = N_HEADS_B
    qscale = HEAD_DIM ** -0.5 * LOG2E
    kern = functools.partial(_dil_kernel, qscale=qscale, unroll=16)
    blk = (seq, HEAD_DIM)
    return pl.pallas_call(
        kern,
        grid=(batch, nh),
        in_specs=[pl.BlockSpec(blk, lambda b, h: (b, h)),
                  pl.BlockSpec(blk, lambda b, h: (b, nh + h)),
                  pl.BlockSpec(blk, lambda b, h: (b, 2 * nh + h)),
                  pl.BlockSpec(blk, lambda b, h: (b, nh + h)),
                  pl.BlockSpec(blk, lambda b, h: (0, 0), pipeline_mode=pl.Buffered(1)),
                  pl.BlockSpec(blk, lambda b, h: (0, 0), pipeline_mode=pl.Buffered(1))],
        out_specs=pl.BlockSpec(blk, lambda b, h: (b, h)),
        out_shape=jax.ShapeDtypeStruct((rows, WIDTH_B), BF16),
        scratch_shapes=([pltpu.VMEM(blk, F32)] * 2 + [pltpu.VMEM(blk, BF16)] * 9
                        + [pltpu.VMEM(blk, F32)] * 9),
        compiler_params=_cparams(("parallel", "arbitrary")),
        name="dilated",
    )(qkv, qkv, qkv, gates, ctab, stab)


def _out_kernel(oa_ref, ob_ref, w_ref, x_ref, mod_ref, g_ref, o_ref):
    y = jnp.dot(oa_ref[...], w_ref[0:WIDTH_A, :], preferred_element_type=F32)
    y = y + jnp.dot(ob_ref[...], w_ref[WIDTH_A:WIDTH_A + WIDTH_B, :], preferred_element_type=F32)
    yn = y * lax.rsqrt(jnp.mean(y * y, axis=-1, keepdims=True) + EPS) * g_ref[...]
    o_ref[...] = x_ref[...] + mod_ref[0, 2:3, :] * yn


def _out_call(o_a, o_b, w_out, x2, mod3, g_post, seq):
    rows, d = x2.shape
    tm = OUT_TM
    tpb = seq // tm
    return pl.pallas_call(
        _out_kernel,
        grid=(rows // tm,),
        in_specs=[pl.BlockSpec((tm, WIDTH_A), lambda i: (i, 0)),
                  pl.BlockSpec((tm, WIDTH_B), lambda i: (i, 0)),
                  pl.BlockSpec((WIDTH_A + WIDTH_B, d), lambda i: (0, 0)),
                  pl.BlockSpec((tm, d), lambda i: (i, 0)),
                  pl.BlockSpec((1, 3, d), lambda i: (i // tpb, 0, 0)),
                  pl.BlockSpec((1, d), lambda i: (0, 0))],
        out_specs=pl.BlockSpec((tm, d), lambda i: (i, 0)),
        out_shape=jax.ShapeDtypeStruct((rows, d), F32),
        compiler_params=_cparams(("parallel",)),
        name="out_proj",
    )(o_a, o_b, w_out, x2, mod3, g_post)


def _rope_tables(seq):
    pos = jnp.arange(seq, dtype=F32)

    def cs(n_rot):
        inv = ROPE_THETA ** (-jnp.arange(0, n_rot, 2, dtype=F32) / n_rot)
        ang = pos[:, None] * inv[None, :]
        return jnp.cos(ang), jnp.sin(ang)

    c32, s32 = cs(ROPE_DIM)
    c16, s16 = cs(IDX_ROPE)
    ck = jnp.concatenate([c32, c32, jnp.ones((seq, LANE - ROPE_DIM), F32)], axis=1)
    sk = jnp.concatenate([-s32, s32, jnp.zeros((seq, LANE - ROPE_DIM), F32)], axis=1)
    ci = jnp.concatenate([c16, c16, jnp.ones((seq, LANE - IDX_ROPE), F32)], axis=1)
    si = jnp.concatenate([-s16, s16, jnp.zeros((seq, LANE - IDX_ROPE), F32)], axis=1)
    gap = jnp.ones((seq, LANE // 2 - ROPE_DIM // 2), F32)
    cd = jnp.concatenate([c32, gap, c32, gap], axis=1)
    sd = jnp.concatenate([-s32, 0.0 * gap, s32, 0.0 * gap], axis=1)
    return (c32.T, s32.T, c16.T, s16.T, ck, sk, ci, si, cd, sd)


def _pair_heads(wt):
    half = ROPE_DIM // 2
    mid = ROPE_DIM + (LANE // 2 - half)
    w3 = wt.reshape(-1, HEAD_DIM, wt.shape[-1])
    w3 = jnp.concatenate([w3[:, :half], w3[:, ROPE_DIM:mid], w3[:, half:ROPE_DIM], w3[:, mid:]], axis=1)
    return w3.reshape(wt.shape)


def _layer(x, c, w_ada, b_ada, g_pre, g_post, w_in, g_q, g_kv, w_uq, w_uq_idx, w_uk, w_uv, w_out):
    batch, seq, d = x.shape
    assert seq % PROJ_TM == 0 or seq < PROJ_TM
    assert seq % KEY_CHUNK == 0 and seq // 16 >= N_BACK
    assert all(w // dl == N_BACK for w, dl in DILATED_CONFIGS)
    rows = batch * seq
    x2 = x.reshape(rows, d)

    o_cq, o_ckv, o_kr, o_ki, o_wi = 0, 512, 768, 800, 864
    o_ga, o_q, o_k, o_v, o_gb = 880, 1904, 2928, 3952, 4976
    wt = w_in.T.astype(BF16)
    zeros = lambda n: jnp.zeros((n, d), BF16)
    w_all = jnp.concatenate([
        wt[o_cq:o_ki], zeros(896 - 800), wt[o_ki:o_ga], zeros(SMALL_W - 976),
        _pair_heads(wt[o_q:o_k]), _pair_heads(wt[o_k:o_v]), wt[o_v:o_gb],
        wt[o_ga:o_q], wt[o_gb:o_gb + WIDTH_B]], axis=0)
    assert w_all.shape[0] == IN_PAD

    c_pad = jnp.zeros((PACK16, d), F32).at[:batch].set(c)
    mod = _mod_call(c_pad, w_ada, b_ada.reshape(1, -1))[:batch]
    mod3 = mod.reshape(batch, 3, d)

    small, qkv, gates = _proj_call(x2, mod3, g_pre.reshape(1, d), w_all, seq)

    tabs = _rope_tables(seq)
    wuk_t = jnp.transpose(jnp.pad(w_uk, ((0, 0), (ROPE_DIM, 0), (0, 0))), (0, 2, 1)).astype(BF16)
    qcat, qidx, w_t, kcat, kidx, ckvt = _prep_call(
        small, g_q.reshape(1, -1), g_kv.reshape(1, -1), w_uq.T.astype(BF16),
        w_uq_idx.T.astype(BF16), wuk_t, tabs[:8], seq)

    o_a = _dsa_call(kidx, kcat, ckvt, qcat, qidx, w_t, gates, w_uv.astype(BF16), batch, seq)
    o_b = _dil_call(qkv, gates, tabs[8], tabs[9], batch, seq)
    out = _out_call(o_a, o_b, w_out.astype(BF16), x2, mod3, g_post.reshape(1, d), seq)
    return out.reshape(batch, seq, d)


def kernel(x, c, w_ada, b_ada, g_pre, g_post, w_in, g_q, g_kv, w_uq, w_uq_idx, w_uk, w_uv, w_out):
    for layer in range(w_ada.shape[0]):
        x = _layer(x, c, w_ada[layer], b_ada[layer], g_pre[layer], g_post[layer], w_in[layer],
                   g_q[layer], g_kv[layer], w_uq[layer], w_uq_idx[layer], w_uk[layer],
                   w_uv[layer], w_out[layer])
    return x
```

```python
import functools
import math

import numpy as np
import jax
import jax.numpy as jnp
from jax import lax
from jax.experimental import pallas as pl
from jax.experimental.pallas import tpu as pltpu

F32 = jnp.float32
BF16 = jnp.bfloat16
I32 = jnp.int32

HEAD_DIM = 128
ROPE_DIM = HEAD_DIM // 4
ROPE_THETA = 500000.0
EPS = 1e-6
NEG = -1e30
N_HEADS_A = 8
WIDTH_A = N_HEADS_A * HEAD_DIM
Q_RANK = 512
KV_RANK = 256
IDX_HEADS = 16
IDX_DIM = 64
IDX_ROPE = IDX_DIM // 4
TOPK_MAX = 256
QUERY_BLOCK = 128
N_HEADS_B = 8
WIDTH_B = N_HEADS_B * HEAD_DIM
DILATED_CONFIGS = ((128, 1), (512, 4), (2048, 16))
N_BACK = 128
SMALL_W = 1024
IN_PAD = SMALL_W + 5 * 1024
KCAT = 288
KIDX_PAD = 64
INT_MIN = -(2 ** 31)
LOG2E = 1.4426950408889634

LANE = 128
PACK16 = 16
VMEM_LIMIT = 56 * 1024 * 1024
PROJ_TM = 1024
PROJ_TN = 1024
PREP_TQ = 512
KEY_CHUNK = 512
OUT_TM = 512
ROW_CHUNK = 512
NORM_ROWS = 128


def _cparams(sem):
    return pltpu.CompilerParams(dimension_semantics=sem, vmem_limit_bytes=VMEM_LIMIT)


def _silu(g):
    return g * jax.nn.sigmoid(g)


def _split_bf16(x):
    hi = x.astype(BF16)
    return hi, (x - hi.astype(F32)).astype(BF16)


def _mod_kernel(c_ref, w_ref, b_ref, o_ref):
    a_hi, a_lo = _split_bf16(_silu(c_ref[...]))
    w_hi, w_lo = _split_bf16(w_ref[...])
    dot = functools.partial(jnp.dot, preferred_element_type=F32)
    o_ref[...] = dot(a_hi, w_hi) + (dot(a_hi, w_lo) + dot(a_lo, w_hi)) + b_ref[...]


def _mod_call(c_pad, w_ada, b_ada):
    rows, d = c_pad.shape
    n = w_ada.shape[1]
    tn = 1024
    return pl.pallas_call(
        _mod_kernel,
        grid=(n // tn,),
        in_specs=[pl.BlockSpec((rows, d), lambda j: (0, 0)),
                  pl.BlockSpec((d, tn), lambda j: (0, j)),
                  pl.BlockSpec((1, tn), lambda j: (0, j))],
        out_specs=pl.BlockSpec((rows, tn), lambda j: (0, j)),
        out_shape=jax.ShapeDtypeStruct((rows, n), F32),
        compiler_params=_cparams(("arbitrary",)),
        name="mod",
    )(c_pad, w_ada, b_ada)


def _proj_kernel(x_ref, mod_ref, g_ref, w_ref, small_ref, qkv_ref, gates_ref, h_ref, *,
                 n_small, n_qkv):
    j = pl.program_id(1)
    tm = x_ref.shape[0]
    half = tm // 2
    assert n_small == 1 and half % NORM_ROWS == 0

    def project(rows=slice(None)):
        return lax.dot_general(h_ref[rows, :], w_ref[...], (((1,), (1,)), ((), ())),
                               preferred_element_type=F32)

    @pl.when(j == 0)
    def _():
        shift = mod_ref[0, 0:1, :]
        scale1 = 1.0 + mod_ref[0, 1:2, :]
        g = g_ref[...]

        def norm_rows(rows):
            x = x_ref[rows, :]
            ms = jnp.mean(x * x, axis=-1, keepdims=True)
            y = x * lax.rsqrt(ms + EPS) * g
            h_ref[rows, :] = (y * scale1 + shift).astype(BF16)

        def body(r, carry):
            norm_rows(pl.ds(pl.multiple_of(r * NORM_ROWS, NORM_ROWS), NORM_ROWS))
            return carry

        lax.fori_loop(0, half // NORM_ROWS, body, 0)
        small_ref[0:half, :] = project(slice(0, half))
        for r in range(half // NORM_ROWS):
            norm_rows(slice(half + r * NORM_ROWS, half + (r + 1) * NORM_ROWS))
        small_ref[half:tm, :] = project(slice(half, tm))

    @pl.when((j >= n_small) & (j < n_small + n_qkv))
    def _():
        qkv_ref[...] = project()

    @pl.when(j >= n_small + n_qkv)
    def _():
        gates_ref[...] = project().astype(BF16)


def _proj_call(x2, mod3, g_pre, w_all, seq):
    rows, d = x2.shape
    tm = min(PROJ_TM, seq)
    tn = PROJ_TN
    n_small = SMALL_W // tn
    n_qkv = 3 * WIDTH_B // tn
    n_gate = (WIDTH_A + WIDTH_B) // tn
    tiles_per_batch = seq // tm
    kern = functools.partial(_proj_kernel, n_small=n_small, n_qkv=n_qkv)
    return pl.pallas_call(
        kern,
        grid=(rows // tm, n_small + n_qkv + n_gate),
        in_specs=[pl.BlockSpec((tm, d), lambda i, j: (i, 0)),
                  pl.BlockSpec((1, 3, d), lambda i, j: (i // tiles_per_batch, 0, 0)),
                  pl.BlockSpec((1, d), lambda i, j: (0, 0)),
                  pl.BlockSpec((tn, d), lambda i, j: (j, 0))],
        out_specs=[pl.BlockSpec((tm, tn), lambda i, j: (i, jnp.minimum(j, n_small - 1))),
                   pl.BlockSpec((tm, tn), lambda i, j: (i, jnp.clip(j - n_small, 0, n_qkv - 1))),
                   pl.BlockSpec((tm, tn),
                                lambda i, j: (i, jnp.clip(j - n_small - n_qkv, 0, n_gate - 1)))],
        out_shape=[jax.ShapeDtypeStruct((rows, SMALL_W), F32),
                   jax.ShapeDtypeStruct((rows, 3 * WIDTH_B), F32),
                   jax.ShapeDtypeStruct((rows, WIDTH_A + WIDTH_B), BF16)],
        scratch_shapes=[pltpu.VMEM((tm, d), BF16)],
        compiler_params=_cparams(("parallel", "arbitrary")),
        name="in_proj",
    )(x2, mod3, g_pre, w_all)


def _prep_kernel(small_ref, gq_ref, gkv_ref, wuq_ref, wuqi_ref, wuk_ref,
                 cosq_ref, sinq_ref, cosi_ref, sini_ref, ck_ref, sk_ref, ci_ref, si_ref,
                 qcat_ref, qidx_ref, wt_ref, kcat_ref, kidx_ref, ckvt_ref, *, qscale):
    tq = small_ref.shape[0]
    nblk = tq // QUERY_BLOCK
    cq = small_ref[:, 0:Q_RANK]
    cqn = (cq * lax.rsqrt(jnp.mean(cq * cq, axis=-1, keepdims=True) + EPS)
           * gq_ref[...]).astype(BF16)
    nt = (((1,), (1,)), ((), ()))

    q_t = lax.dot_general(wuq_ref[...], cqn, nt, preferred_element_type=F32)
    cosq = cosq_ref[...]
    sinq = sinq_ref[...]
    half = ROPE_DIM // 2
    zpad = [jnp.zeros((KCAT - KV_RANK - ROPE_DIM, tq), F32)] if KCAT > KV_RANK + ROPE_DIM else []
    for h in range(N_HEADS_A):
        base = h * HEAD_DIM
        x1 = q_t[base:base + half]
        x2 = q_t[base + half:base + ROPE_DIM]
        q_rope = jnp.concatenate([x1 * cosq - x2 * sinq, x2 * cosq + x1 * sinq] + zpad, axis=0)
        q_lat = jnp.dot(wuk_ref[h], q_t[base:base + HEAD_DIM].astype(BF16),
                        preferred_element_type=F32)
        q_lat = (q_lat * qscale).astype(BF16)
        q_rope = (q_rope * qscale).astype(BF16)
        for blk in range(nblk):
            cols = slice(blk * QUERY_BLOCK, (blk + 1) * QUERY_BLOCK)
            unit, hu = divmod(h, N_HEADS_A // SCORE_UNITS)
            lanes = slice(hu * QUERY_BLOCK, (hu + 1) * QUERY_BLOCK)
            qcat_ref[blk, unit, 0:KV_RANK, lanes] = q_lat[:, cols]
            qcat_ref[blk, unit, KV_RANK:KCAT, lanes] = q_rope[:, cols]

    qi_t = lax.dot_general(wuqi_ref[...], cqn, nt, preferred_element_type=F32)
    cosi = cosi_ref[...]
    sini = sini_ref[...]
    ihalf = IDX_ROPE // 2
    ipad = [jnp.zeros((KIDX_PAD - IDX_DIM, tq), F32)] if KIDX_PAD > IDX_DIM else []
    for h in range(IDX_HEADS):
        base = h * IDX_DIM
        x1 = qi_t[base:base + ihalf]
        x2 = qi_t[base + ihalf:base + IDX_ROPE]
        qi = jnp.concatenate([x1 * cosi - x2 * sini, x2 * cosi + x1 * sini,
                              qi_t[base + IDX_ROPE:base + IDX_DIM]] + ipad, axis=0).astype(BF16)
        for blk in range(nblk):
            cols = slice(blk * QUERY_BLOCK, (blk + 1) * QUERY_BLOCK)
            qidx_ref[blk, :, h * QUERY_BLOCK:(h + 1) * QUERY_BLOCK] = qi[:, cols]

    lane = lax.broadcasted_iota(I32, (tq, LANE), 1)

    slab_b = small_ref[:, 896:1024]
    swap_b = jnp.where(lane < ihalf, pltpu.roll(slab_b, LANE - ihalf, 1), pltpu.roll(slab_b, ihalf, 1))
    kidx_ref[...] = (slab_b * ci_ref[...] + swap_b * si_ref[...])[:, :KIDX_PAD].astype(BF16)
    w_t = slab_b.T[IDX_DIM:IDX_DIM + IDX_HEADS] * (IDX_HEADS ** -0.5 * IDX_DIM ** -0.5)
    for blk in range(nblk):
        wt_ref[blk] = w_t[:, blk * QUERY_BLOCK:(blk + 1) * QUERY_BLOCK]

    ckv = small_ref[:, Q_RANK:Q_RANK + KV_RANK]
    cn = ckv * lax.rsqrt(jnp.mean(ckv * ckv, axis=-1, keepdims=True) + EPS) * gkv_ref[...]
    kcat_ref[:, 0:KV_RANK] = cn.astype(BF16)
    ckvt_ref[0] = cn.T.astype(BF16)
    slab_a = small_ref[:, 768:896]
    swap_a = jnp.where(lane < half, pltpu.roll(slab_a, LANE - half, 1), pltpu.roll(slab_a, half, 1))
    kcat_ref[:, KV_RANK:KCAT] = (slab_a * ck_ref[...] + swap_a * sk_ref[...])[:, :KCAT - KV_RANK].astype(BF16)


def _prep_call(small, g_q, g_kv, wuq_t, wuqi_t, wuk_t, tabs, seq):
    rows = small.shape[0]
    tq = PREP_TQ
    nblk = tq // QUERY_BLOCK
    tpb = seq // tq
    nqb = rows // QUERY_BLOCK
    qscale = HEAD_DIM ** -0.5 * LOG2E
    kern = functools.partial(_prep_kernel, qscale=qscale)
    const = lambda t: (0, 0)
    tcol = lambda t: (0, t % tpb)
    trow = lambda t: (t % tpb, 0)
    return pl.pallas_call(
        kern,
        grid=(rows // tq,),
        in_specs=[pl.BlockSpec((tq, SMALL_W), lambda t: (t, 0)),
                  pl.BlockSpec((1, Q_RANK), const),
                  pl.BlockSpec((1, KV_RANK), const),
                  pl.BlockSpec((WIDTH_A, Q_RANK), const),
                  pl.BlockSpec((IDX_HEADS * IDX_DIM, Q_RANK), const),
                  pl.BlockSpec((N_HEADS_A, KV_RANK, HEAD_DIM), lambda t: (0, 0, 0)),
                  pl.BlockSpec((ROPE_DIM // 2, tq), tcol),
                  pl.BlockSpec((ROPE_DIM // 2, tq), tcol),
                  pl.BlockSpec((IDX_ROPE // 2, tq), tcol),
                  pl.BlockSpec((IDX_ROPE // 2, tq), tcol),
                  pl.BlockSpec((tq, LANE), trow),
                  pl.BlockSpec((tq, LANE), trow),
                  pl.BlockSpec((tq, LANE), trow),
                  pl.BlockSpec((tq, LANE), trow)],
        out_specs=[pl.BlockSpec((nblk, SCORE_UNITS, KCAT, UNIT_COLS), lambda t: (t, 0, 0, 0)),
                   pl.BlockSpec((nblk, KIDX_PAD, IDX_HEADS * QUERY_BLOCK), lambda t: (t, 0, 0)),
                   pl.BlockSpec((nblk, IDX_HEADS, QUERY_BLOCK), lambda t: (t, 0, 0)),
                   pl.BlockSpec((tq, KCAT), lambda t: (t, 0)),
                   pl.BlockSpec((tq, KIDX_PAD), lambda t: (t, 0)),
                   pl.BlockSpec((1, KV_RANK, tq), lambda t: (t, 0, 0))],
        out_shape=[jax.ShapeDtypeStruct((nqb, SCORE_UNITS, KCAT, UNIT_COLS), BF16),
                   jax.ShapeDtypeStruct((nqb, KIDX_PAD, IDX_HEADS * QUERY_BLOCK), BF16),
                   jax.ShapeDtypeStruct((nqb, IDX_HEADS, QUERY_BLOCK), F32),
                   jax.ShapeDtypeStruct((rows, KCAT), BF16),
                   jax.ShapeDtypeStruct((rows, KIDX_PAD), BF16),
                   jax.ShapeDtypeStruct((rows // tq, KV_RANK, tq), BF16)],
        compiler_params=_cparams(("parallel",)),
        name="dsa_prep",
    )(small, g_q, g_kv, wuq_t, wuqi_t, wuk_t, *tabs)


def _float_of_rank(u):
    key = u ^ INT_MIN
    return lax.bitcast_convert_type(key ^ (lax.shift_right_arithmetic(key, 31) & 0x7FFFFFFF), F32)


_ABOVE_NEG = float(np.nextafter(np.float32(NEG), np.float32(0.0)))
TRIM_PASSES = 8
SELECT_BITS = 24
SCORE_UNITS = 2
UNIT_COLS = N_HEADS_A * QUERY_BLOCK // SCORE_UNITS


def _dsa_kernel(kidx_ref, kcat_ref, ckvt_ref, qcat_ref, qidx_ref, wt_ref, gate_ref, wuv_ref,
                o_ref, isc_ref, m_ref, l_ref, acc_ref, s_ref, *, k_sel):
    i = pl.program_id(1)
    kc = KEY_CHUNK
    qb = QUERY_BLOCK
    per = kc // qb
    nch = (i + per) // per
    qpos = i * qb + lax.broadcasted_iota(I32, (kc, qb), 1)
    krow = lax.broadcasted_iota(I32, (kc, qb), 0)

    def chunk_rows(c):
        return pl.ds(c * kc if isinstance(c, int) else pl.multiple_of(c * kc, kc), kc)

    def fold(op, x):
        return op(x.reshape(kc // 8, 8, qb), axis=0)

    def chunk_pairs(step):
        def body(j, carry):
            step(2 * j)
            step(2 * j + 1)
            return carry

        lax.fori_loop(0, nch // 2, body, 0)

        @pl.when(nch % 2 == 1)
        def _():
            step(nch - 1)

    def index_chunk(c):
        kblk = kidx_ref[chunk_rows(c), :]
        acc = jnp.zeros((kc, qb), F32)
        group = 4
        for g in range(IDX_HEADS // group):
            lg = jnp.dot(kblk, qidx_ref[0, :, g * group * qb:(g + 1) * group * qb],
                         preferred_element_type=F32)
            for hh in range(group):
                h = g * group + hh
                acc = acc + jnp.maximum(lg[:, hh * qb:(hh + 1) * qb], 0.0) * wt_ref[0, h:h + 1, :]
        causal = (c * kc + krow) <= qpos
        isc_ref[chunk_rows(c), :] = jnp.where(causal, acc, NEG)

    chunk_pairs(index_chunk)

    n_max = isc_ref.shape[0] // kc

    def count_chunks(n, t):
        parts = [fold(jnp.sum, jnp.where(isc_ref[chunk_rows(c), :] >= t, 1, 0).astype(I32))
                 for c in range(n)]
        while len(parts) > 1:
            parts = [a + b for a, b in zip(parts[::2], parts[1::2])] + parts[len(parts) & ~1:]
        return jnp.sum(parts[0], axis=0, keepdims=True)

    def bit_step(carry, bit, count):
        tu, cnt_t = carry
        cand_u = tu | lax.shift_left(jnp.int32(1), bit)
        cnt = count(_float_of_rank(cand_u))
        ok = cnt >= k_sel
        return jnp.where(ok, cand_u, tu), jnp.where(ok, cnt, cnt_t)

    def bis_body(s, carry):
        return bit_step(carry, 31 - s, lambda t: lax.switch(
            nch - 1, [functools.partial(count_chunks, n, t) for n in range(1, n_max + 1)]))

    start = (jnp.zeros((1, qb), I32), jnp.full((1, qb), k_sel, I32))
    units = s_ref.shape[0]
    bits_per_unit = SELECT_BITS // units

    def select_with_scores(n):
        def run():
            def body(u, carry):
                s_ref[u, 0:n * kc, :] = jnp.dot(kcat_ref[0:n * kc, :], qcat_ref[0, u],
                                                preferred_element_type=F32)
                for k in range(bits_per_unit):
                    carry = bit_step(carry, 31 - (u * bits_per_unit + k),
                                     functools.partial(count_chunks, n))
                return carry
            return lax.fori_loop(0, units, body, start)
        return run

    tu, cnt_t = lax.switch(nch - 1, [select_with_scores(n) for n in range(1, n_max + 1)])

    def trim(tu, cnt_t):
        thr = jnp.where(tu == 0, -jnp.inf, _float_of_rank(tu))
        few = thr < _ABOVE_NEG
        thr = jnp.where(few, _ABOVE_NEG, thr)
        cnt0 = jnp.where(few, k_sel, cnt_t)

        def trim_cond(state):
            _, cnt, n = state
            return (jnp.max(cnt) > k_sel) & (n < TRIM_PASSES)

        def trim_body(state):
            low, cnt, n = state

            def min_body(c, mn8):
                x = isc_ref[chunk_rows(c), :]
                sel = jnp.where(x >= thr, jnp.where(x > low, x, jnp.inf), jnp.inf)
                return jnp.minimum(mn8, fold(jnp.min, sel))

            mn = jnp.min(lax.fori_loop(0, nch, min_body, jnp.full((8, qb), jnp.inf, F32)),
                         axis=0, keepdims=True)

            def eq_body(c, cnt8):
                return cnt8 + fold(jnp.sum, jnp.where(isc_ref[chunk_rows(c), :] == mn, 1, 0).astype(I32))

            n_eq = jnp.sum(lax.fori_loop(0, nch, eq_body, jnp.zeros((8, qb), I32)),
                           axis=0, keepdims=True)
            can = (cnt > k_sel) & (cnt - n_eq >= k_sel)
            stalled = jnp.max(jnp.where(can, 1, 0)) == 0
            return (jnp.where(can, mn, low), jnp.where(can, cnt - n_eq, cnt),
                    jnp.where(stalled, TRIM_PASSES + 1, n + 1))

        low, cnt, n = lax.while_loop(trim_cond, trim_body,
                                     (jnp.full((1, qb), -jnp.inf, F32), cnt0, jnp.int32(0)))
        return thr, low, (n == TRIM_PASSES) & (jnp.max(cnt) > k_sel)

    thr, low, unresolved = trim(tu, cnt_t)
    thr, low = lax.cond(unresolved,
                        lambda: trim(*lax.fori_loop(SELECT_BITS, 32, bis_body, (tu, cnt_t)))[:2],
                        lambda: (thr, low))

    m_ref[...] = jnp.full(m_ref.shape, NEG, F32)
    l_ref[...] = jnp.zeros(l_ref.shape, F32)
    acc_ref[...] = jnp.zeros(acc_ref.shape, F32)

    def attend(c):
        rows = chunk_rows(c)
        x = isc_ref[rows, :]
        bias = jnp.where(x >= thr, jnp.where(x > low, 0.0, NEG), NEG).astype(F32)
        s = (jnp.concatenate([s_ref[u, rows, :] for u in range(units)], axis=1)
             + jnp.concatenate([bias] * N_HEADS_A, axis=1))
        m_old = m_ref[...]
        m_new = jnp.maximum(m_old, jnp.max(s, axis=0, keepdims=True))
        alpha = jnp.exp2(m_old - m_new)
        p = jnp.exp2(s - m_new)
        l_ref[...] = alpha * l_ref[...] + jnp.sum(p, axis=0, keepdims=True)
        acc_ref[...] = alpha * acc_ref[...] + jnp.dot(ckvt_ref[c], p.astype(BF16),
                                                      preferred_element_type=F32)
        m_ref[...] = m_new

    chunk_pairs(attend)

    inv_l = 1.0 / l_ref[...]
    outs = []
    for h in range(N_HEADS_A):
        cols = slice(h * qb, (h + 1) * qb)
        o_t = acc_ref[:, cols] * inv_l[:, cols]
        outs.append(jnp.dot(o_t.T.astype(BF16), wuv_ref[h], preferred_element_type=F32))
    o = jnp.concatenate(outs, axis=1)
    g = gate_ref[...].astype(F32)
    o_ref[...] = (o * _silu(g)).astype(BF16)


def _dsa_call(kidx, kcat, ckvt, qcat, qidx, w_t, gates, wuv, batch, seq):
    rows = kidx.shape[0]
    nb = seq // QUERY_BLOCK
    nkc = seq // KEY_CHUNK
    k_sel = min(TOPK_MAX, seq // 4)
    kern = functools.partial(_dsa_kernel, k_sel=k_sel)
    return pl.pallas_call(
        kern,
        grid=(batch, nb),
        in_specs=[pl.BlockSpec((seq, KIDX_PAD), lambda b, i: (b, 0)),
                  pl.BlockSpec((seq, KCAT), lambda b, i: (b, 0)),
                  pl.BlockSpec((nkc, KV_RANK, KEY_CHUNK), lambda b, i: (b, 0, 0)),
                  pl.BlockSpec((1, SCORE_UNITS, KCAT, UNIT_COLS), lambda b, i: (b * nb + i, 0, 0, 0)),
                  pl.BlockSpec((1, KIDX_PAD, IDX_HEADS * QUERY_BLOCK), lambda b, i: (b * nb + i, 0, 0)),
                  pl.BlockSpec((1, IDX_HEADS, QUERY_BLOCK), lambda b, i: (b * nb + i, 0, 0)),
                  pl.BlockSpec((QUERY_BLOCK, WIDTH_A), lambda b, i: (b * nb + i, 0)),
                  pl.BlockSpec((N_HEADS_A, KV_RANK, HEAD_DIM), lambda b, i: (0, 0, 0))],
        out_specs=pl.BlockSpec((QUERY_BLOCK, WIDTH_A), lambda b, i: (b * nb + i, 0)),
        out_shape=jax.ShapeDtypeStruct((rows, WIDTH_A), BF16),
        scratch_shapes=[pltpu.VMEM((seq, QUERY_BLOCK), F32),
                        pltpu.VMEM((1, N_HEADS_A * QUERY_BLOCK), F32),
                        pltpu.VMEM((1, N_HEADS_A * QUERY_BLOCK), F32),
                        pltpu.VMEM((KV_RANK, N_HEADS_A * QUERY_BLOCK), F32),
                        pltpu.VMEM((SCORE_UNITS, seq, UNIT_COLS), F32)],
        compiler_params=_cparams(("parallel", "arbitrary")),
        name="dsa_attn",
    )(kidx, kcat, ckvt, qcat, qidx, w_t, gates, wuv)


def _dil_kernel(q_ref, k_ref, v_ref, gate_ref, c_ref, s_ref, o_ref,
                nat_ref, a4_ref, qn, kn, vn, q4, k4, v4, q16, k16, v16,
                oc0, ms0, ls0, oc1, ms1, ls1, oc2, ms2, ls2, *, qscale, unroll):
    seq = q_ref.shape[0]
    blk = N_BACK
    rc = ROW_CHUNK
    m4 = seq // 4
    m16 = seq // 16

    def build(src_ref, dn, d4, d16, rope_scale):
        def nat_body(c, carry):
            rows = pl.ds(pl.multiple_of(c * rc, rc), rc)
            x = src_ref[rows, :]
            if rope_scale is not None:
                x = x * c_ref[rows, :] + pltpu.roll(x, LANE // 2, 1) * s_ref[rows, :]
                if rope_scale != 1.0:
                    x = x * rope_scale
                nat_ref[rows, :] = x
            dn[rows, :] = x.astype(BF16)
            return carry

        lax.fori_loop(0, seq // rc, nat_body, 0)
        nat = src_ref if rope_scale is None else nat_ref

        def a4_body(c, carry):
            r4 = c // (m4 // rc)
            j0 = (c % (m4 // rc)) * rc
            x = nat[pl.ds(r4 + 4 * j0, rc, stride=4), :]
            rows = pl.ds(pl.multiple_of(c * rc, rc), rc)
            a4_ref[rows, :] = x
            d4[rows, :] = x.astype(BF16)
            return carry

        lax.fori_loop(0, seq // rc, a4_body, 0)

        def a16_body(r16, carry):
            r4 = r16 % 4
            s = r16 // 4
            x = a4_ref[pl.ds(r4 * m4 + s, m16, stride=4), :]
            d16[pl.ds(pl.multiple_of(r16 * m16, m16), m16), :] = x.astype(BF16)
            return carry

        lax.fori_loop(0, 16, a16_body, 0)

    build(q_ref, qn, q4, q16, qscale)
    build(k_ref, kn, k4, k16, 1.0)
    build(v_ref, vn, v4, v16, None)

    qi = lax.broadcasted_iota(I32, (blk, 2 * blk), 0) + blk
    kj = lax.broadcasted_iota(I32, (blk, 2 * blk), 1)
    rel = qi - kj
    bias_band = jnp.where((rel >= 0) & (rel <= N_BACK), 0.0, NEG).astype(F32)
    no_prev = jnp.where(kj < blk, NEG, 0.0).astype(F32)
    qi1 = lax.broadcasted_iota(I32, (blk, blk), 0)
    kj1 = lax.broadcasted_iota(I32, (blk, blk), 1)
    bias_first = jnp.where(kj1 <= qi1, 0.0, NEG).astype(F32)
    nt = (((1,), (1,)), ((), ()))

    def run_config(qs, ks, vs, nmb, store):
        def rows_of(u, first):
            cur = pl.ds(pl.multiple_of(u * blk, blk), blk)
            if first is True:
                return (cur,)
            return (pl.ds(pl.multiple_of(jnp.maximum(u * blk - blk, 0), blk), blk), cur)

        def gather(ref, rows):
            parts = [ref[r, :] for r in rows]
            return parts[0] if len(parts) == 1 else jnp.concatenate(parts, axis=0)

        def loop_body(g, carry):
            units = []
            for t in range(unroll):
                if nmb <= unroll:
                    first = (t % nmb == 0)
                else:
                    first = jnp.where(g == 0, 1.0, 0.0).astype(F32) if t == 0 else False
                units.append((g * unroll + t, first))
            scores = []
            for u, first in units:
                rows = rows_of(u, first)
                if first is True:
                    bias = bias_first
                elif first is False:
                    bias = bias_band
                else:
                    bias = bias_band + no_prev * first
                scores.append(lax.dot_general(qs[rows[-1], :], gather(ks, rows), nt,
                                              preferred_element_type=F32) + bias)
            probs = []
            for s in scores:
                m = jnp.max(s, axis=-1, keepdims=True)
                p = jnp.exp2(s - m)
                l = jnp.sum(p, axis=-1, keepdims=True)
                probs.append((p.astype(BF16), jnp.broadcast_to(m, (blk, LANE)),
                              jnp.broadcast_to(l, (blk, LANE))))
            for (u, first), (p, m_b, l_b) in zip(units, probs):
                store(u, jnp.dot(p, gather(vs, rows_of(u, first)), preferred_element_type=F32), m_b, l_b)
            return carry

        lax.fori_loop(0, seq // blk // unroll, loop_body, 0)

    def store_contig(oc, ms, ls):
        def store(u, o, m_b, l_b):
            rows = pl.ds(pl.multiple_of(u * blk, blk), blk)
            oc[rows, :] = o
            ms[rows, :] = m_b
            ls[rows, :] = l_b
        return store

    def store_a16_to_a4(u, o, m_b, l_b):
        r16 = u // (m16 // blk)
        mb = u % (m16 // blk)
        rows = pl.ds((r16 % 4) * m4 + 4 * blk * mb + r16 // 4, blk, stride=4)
        oc2[rows, :] = o
        ms2[rows, :] = m_b
        ls2[rows, :] = l_b

    run_config(qn, kn, vn, seq // blk, store_contig(oc0, ms0, ls0))
    run_config(q4, k4, v4, m4 // blk, store_contig(oc1, ms1, ls1))
    run_config(q16, k16, v16, m16 // blk, store_a16_to_a4)

    def merge_body(c, carry):
        r4 = c // (m4 // rc)
        j0 = (c % (m4 // rc)) * rc
        nat_rows = pl.ds(r4 + 4 * j0, rc, stride=4)
        rows = pl.ds(pl.multiple_of(c * rc, rc), rc)
        m0 = ms0[nat_rows, :]
        m1 = ms1[rows, :]
        m2 = ms2[rows, :]
        mx = jnp.maximum(jnp.maximum(m0, m1), m2)
        e0 = jnp.exp2(m0 - mx)
        e1 = jnp.exp2(m1 - mx)
        e2 = jnp.exp2(m2 - mx)
        den = e0 * ls0[nat_rows, :] + e1 * ls1[rows, :] + e2 * ls2[rows, :]
        nat_ref[nat_rows, :] = (e0 * oc0[nat_rows, :] + e1 * oc1[rows, :] + e2 * oc2[rows, :]) / den
        return carry

    lax.fori_loop(0, seq // rc, merge_body, 0)

    def gate_body(c, carry):
        rows = pl.ds(pl.multiple_of(c * rc, rc), rc)
        g = gate_ref[rows, :].astype(F32)
        o_ref[rows, :] = (nat_ref[rows, :] * _silu(g)).astype(BF16)
        return carry

    lax.fori_loop(0, seq // rc, gate_body, 0)


def _dil_call(qkv, gates, ctab, stab, batch, seq):
    rows = qkv.shape[0]
    nh = N_HEADS_B
    qscale = HEAD_DIM ** -0.5 * LOG2E
    kern = functools.partial(_dil_kernel, qscale=qscale, unroll=16)
    blk = (seq, HEAD_DIM)
    return pl.pallas_call(
        kern,
        grid=(batch, nh),
        in_specs=[pl.BlockSpec(blk, lambda b, h: (b, h)),
                  pl.BlockSpec(blk, lambda b, h: (b, nh + h)),
                  pl.BlockSpec(blk, lambda b, h: (b, 2 * nh + h)),
                  pl.BlockSpec(blk, lambda b, h: (b, nh + h)),
                  pl.BlockSpec(blk, lambda b, h: (0, 0), pipeline_mode=pl.Buffered(1)),
                  pl.BlockSpec(blk, lambda b, h: (0, 0), pipeline_mode=pl.Buffered(1))],
        out_specs=pl.BlockSpec(blk, lambda b, h: (b, h)),
        out_shape=jax.ShapeDtypeStruct((rows, WIDTH_B), BF16),
        scratch_shapes=([pltpu.VMEM(blk, F32)] * 2 + [pltpu.VMEM(blk, BF16)] * 9
                        + [pltpu.VMEM(blk, F32)] * 9),
        compiler_params=_cparams(("parallel", "arbitrary")),
        name="dilated",
    )(qkv, qkv, qkv, gates, ctab, stab)


def _out_kernel(oa_ref, ob_ref, w_ref, x_ref, mod_ref, g_ref, o_ref):
    y = jnp.dot(oa_ref[...], w_ref[0:WIDTH_A, :], preferred_element_type=F32)
    y = y + jnp.dot(ob_ref[...], w_ref[WIDTH_A:WIDTH_A + WIDTH_B, :], preferred_element_type=F32)
    yn = y * lax.rsqrt(jnp.mean(y * y, axis=-1, keepdims=True) + EPS) * g_ref[...]
    o_ref[...] = x_ref[...] + mod_ref[0, 2:3, :] * yn


def _out_call(o_a, o_b, w_out, x2, mod3, g_post, seq):
    rows, d = x2.shape
    tm = OUT_TM
    tpb = seq // tm
    return pl.pallas_call(
        _out_kernel,
        grid=(rows // tm,),
        in_specs=[pl.BlockSpec((tm, WIDTH_A), lambda i: (i, 0)),
                  pl.BlockSpec((tm, WIDTH_B), lambda i: (i, 0)),
                  pl.BlockSpec((WIDTH_A + WIDTH_B, d), lambda i: (0, 0)),
                  pl.BlockSpec((tm, d), lambda i: (i, 0)),
                  pl.BlockSpec((1, 3, d), lambda i: (i // tpb, 0, 0)),
                  pl.BlockSpec((1, d), lambda i: (0, 0))],
        out_specs=pl.BlockSpec((tm, d), lambda i: (i, 0)),
        out_shape=jax.ShapeDtypeStruct((rows, d), F32),
        compiler_params=_cparams(("parallel",)),
        name="out_proj",
    )(o_a, o_b, w_out, x2, mod3, g_post)


def _rope_tables(seq):
    pos = jnp.arange(seq, dtype=F32)

    def cs(n_rot):
        inv = ROPE_THETA ** (-jnp.arange(0, n_rot, 2, dtype=F32) / n_rot)
        ang = pos[:, None] * inv[None, :]
        return jnp.cos(ang), jnp.sin(ang)

    c32, s32 = cs(ROPE_DIM)
    c16, s16 = cs(IDX_ROPE)
    ck = jnp.concatenate([c32, c32, jnp.ones((seq, LANE - ROPE_DIM), F32)], axis=1)
    sk = jnp.concatenate([-s32, s32, jnp.zeros((seq, LANE - ROPE_DIM), F32)], axis=1)
    ci = jnp.concatenate([c16, c16, jnp.ones((seq, LANE - IDX_ROPE), F32)], axis=1)
    si = jnp.concatenate([-s16, s16, jnp.zeros((seq, LANE - IDX_ROPE), F32)], axis=1)
    gap = jnp.ones((seq, LANE // 2 - ROPE_DIM // 2), F32)
    cd = jnp.concatenate([c32, gap, c32, gap], axis=1)
    sd = jnp.concatenate([-s32, 0.0 * gap, s32, 0.0 * gap], axis=1)
    return (c32.T, s32.T, c16.T, s16.T, ck, sk, ci, si, cd, sd)


def _pair_heads(wt):
    half = ROPE_DIM // 2
    mid = ROPE_DIM + (LANE // 2 - half)
    w3 = wt.reshape(-1, HEAD_DIM, wt.shape[-1])
    w3 = jnp.concatenate([w3[:, :half], w3[:, ROPE_DIM:mid], w3[:, half:ROPE_DIM], w3[:, mid:]], axis=1)
    return w3.reshape(wt.shape)


def _layer(x, c, w_ada, b_ada, g_pre, g_post, w_in, g_q, g_kv, w_uq, w_uq_idx, w_uk, w_uv, w_out):
    batch, seq, d = x.shape
    assert seq % PROJ_TM == 0 or seq < PROJ_TM
    assert seq % KEY_CHUNK == 0 and seq // 16 >= N_BACK
    assert all(w // dl == N_BACK for w, dl in DILATED_CONFIGS)
    rows = batch * seq
    x2 = x.reshape(rows, d)

    o_cq, o_ckv, o_kr, o_ki, o_wi = 0, 512, 768, 800, 864
    o_ga, o_q, o_k, o_v, o_gb = 880, 1904, 2928, 3952, 4976
    wt = w_in.T.astype(BF16)
    zeros = lambda n: jnp.zeros((n, d), BF16)
    w_all = jnp.concatenate([
        wt[o_cq:o_ki], zeros(896 - 800), wt[o_ki:o_ga], zeros(SMALL_W - 976),
        _pair_heads(wt[o_q:o_k]), _pair_heads(wt[o_k:o_v]), wt[o_v:o_gb],
        wt[o_ga:o_q], wt[o_gb:o_gb + WIDTH_B]], axis=0)
    assert w_all.shape[0] == IN_PAD

    c_pad = jnp.zeros((PACK16, d), F32).at[:batch].set(c)
    mod = _mod_call(c_pad, w_ada, b_ada.reshape(1, -1))[:batch]
    mod3 = mod.reshape(batch, 3, d)

    small, qkv, gates = _proj_call(x2, mod3, g_pre.reshape(1, d), w_all, seq)

    tabs = _rope_tables(seq)
    wuk_t = jnp.transpose(jnp.pad(w_uk, ((0, 0), (ROPE_DIM, 0), (0, 0))), (0, 2, 1)).astype(BF16)
    qcat, qidx, w_t, kcat, kidx, ckvt = _prep_call(
        small, g_q.reshape(1, -1), g_kv.reshape(1, -1), w_uq.T.astype(BF16),
        w_uq_idx.T.astype(BF16), wuk_t, tabs[:8], seq)

    o_a = _dsa_call(kidx, kcat, ckvt, qcat, qidx, w_t, gates, w_uv.astype(BF16), batch, seq)
    o_b = _dil_call(qkv, gates, tabs[8], tabs[9], batch, seq)
    out = _out_call(o_a, o_b, w_out.astype(BF16), x2, mod3, g_post.reshape(1, d), seq)
    return out.reshape(batch, seq, d)


def kernel(x, c, w_ada, b_ada, g_pre, g_post, w_in, g_q, g_kv, w_uq, w_uq_idx, w_uk, w_uv, w_out):
    for layer in range(w_ada.shape[0]):
        x = _layer(x, c, w_ada[layer], b_ada[layer], g_pre[layer], g_post[layer], w_in[layer],
                   g_q[layer], g_kv[layer], w_uq[layer], w_uq_idx[layer], w_uk[layer],
                   w_uv[layer], w_out[layer])
    return x
```

```python
import functools
import math

import numpy as np
import jax
import jax.numpy as jnp
from jax import lax
from jax.experimental import pallas as pl
from jax.experimental.pallas import tpu as pltpu

F32 = jnp.float32
BF16 = jnp.bfloat16
I32 = jnp.int32

HEAD_DIM = 128
ROPE_DIM = HEAD_DIM // 4
ROPE_THETA = 500000.0
EPS = 1e-6
NEG = -1e30
N_HEADS_A = 8
WIDTH_A = N_HEADS_A * HEAD_DIM
Q_RANK = 512
KV_RANK = 256
IDX_HEADS = 16
IDX_DIM = 64
IDX_ROPE = IDX_DIM // 4
TOPK_MAX = 256
QUERY_BLOCK = 128
N_HEADS_B = 8
WIDTH_B = N_HEADS_B * HEAD_DIM
DILATED_CONFIGS = ((128, 1), (512, 4), (2048, 16))
N_BACK = 128
SMALL_W = 1024
IN_PAD = SMALL_W + 5 * 1024
KCAT = 288
KIDX_PAD = 64
INT_MIN = -(2 ** 31)
LOG2E = 1.4426950408889634

LANE = 128
PACK16 = 16
VMEM_LIMIT = 56 * 1024 * 1024
PROJ_TM = 1024
PROJ_TN = 1024
PREP_TQ = 512
KEY_CHUNK = 512
OUT_TM = 512
ROW_CHUNK = 512
NORM_ROWS = 128


def _cparams(sem):
    return pltpu.CompilerParams(dimension_semantics=sem, vmem_limit_bytes=VMEM_LIMIT)


def _silu(g):
    return g * jax.nn.sigmoid(g)


def _split_bf16(x):
    hi = x.astype(BF16)
    return hi, (x - hi.astype(F32)).astype(BF16)


def _mod_kernel(c_ref, w_ref, b_ref, o_ref):
    a_hi, a_lo = _split_bf16(_silu(c_ref[...]))
    w_hi, w_lo = _split_bf16(w_ref[...])
    dot = functools.partial(jnp.dot, preferred_element_type=F32)
    o_ref[...] = dot(a_hi, w_hi) + (dot(a_hi, w_lo) + dot(a_lo, w_hi)) + b_ref[...]


def _mod_call(c_pad, w_ada, b_ada):
    rows, d = c_pad.shape
    n = w_ada.shape[1]
    tn = 1024
    return pl.pallas_call(
        _mod_kernel,
        grid=(n // tn,),
        in_specs=[pl.BlockSpec((rows, d), lambda j: (0, 0)),
                  pl.BlockSpec((d, tn), lambda j: (0, j)),
                  pl.BlockSpec((1, tn), lambda j: (0, j))],
        out_specs=pl.BlockSpec((rows, tn), lambda j: (0, j)),
        out_shape=jax.ShapeDtypeStruct((rows, n), F32),
        compiler_params=_cparams(("arbitrary",)),
        name="mod",
    )(c_pad, w_ada, b_ada)


def _proj_kernel(x_ref, mod_ref, g_ref, w_ref, small_ref, qkv_ref, gates_ref, h_ref, *,
                 n_small, n_qkv):
    j = pl.program_id(1)
    tm = x_ref.shape[0]
    half = tm // 2
    assert n_small == 1 and half % NORM_ROWS == 0

    def project(rows=slice(None)):
        return lax.dot_general(h_ref[rows, :], w_ref[...], (((1,), (1,)), ((), ())),
                               preferred_element_type=F32)

    @pl.when(j == 0)
    def _():
        shift = mod_ref[0, 0:1, :]
        scale1 = 1.0 + mod_ref[0, 1:2, :]
        g = g_ref[...]

        def norm_rows(rows):
            x = x_ref[rows, :]
            ms = jnp.mean(x * x, axis=-1, keepdims=True)
            y = x * lax.rsqrt(ms + EPS) * g
            h_ref[rows, :] = (y * scale1 + shift).astype(BF16)

        def body(r, carry):
            norm_rows(pl.ds(pl.multiple_of(r * NORM_ROWS, NORM_ROWS), NORM_ROWS))
            return carry

        lax.fori_loop(0, half // NORM_ROWS, body, 0)
        small_ref[0:half, :] = project(slice(0, half))
        for r in range(half // NORM_ROWS):
            norm_rows(slice(half + r * NORM_ROWS, half + (r + 1) * NORM_ROWS))
        small_ref[half:tm, :] = project(slice(half, tm))

    @pl.when((j >= n_small) & (j < n_small + n_qkv))
    def _():
        qkv_ref[...] = project()

    @pl.when(j >= n_small + n_qkv)
    def _():
        gates_ref[...] = project().astype(BF16)


def _proj_call(x2, mod3, g_pre, w_all, seq):
    rows, d = x2.shape
    tm = min(PROJ_TM, seq)
    tn = PROJ_TN
    n_small = SMALL_W // tn
    n_qkv = 3 * WIDTH_B // tn
    n_gate = (WIDTH_A + WIDTH_B) // tn
    tiles_per_batch = seq // tm
    kern = functools.partial(_proj_kernel, n_small=n_small, n_qkv=n_qkv)
    return pl.pallas_call(
        kern,
        grid=(rows // tm, n_small + n_qkv + n_gate),
        in_specs=[pl.BlockSpec((tm, d), lambda i, j: (i, 0)),
                  pl.BlockSpec((1, 3, d), lambda i, j: (i // tiles_per_batch, 0, 0)),
                  pl.BlockSpec((1, d), lambda i, j: (0, 0)),
                  pl.BlockSpec((tn, d), lambda i, j: (j, 0))],
        out_specs=[pl.BlockSpec((tm, tn), lambda i, j: (i, jnp.minimum(j, n_small - 1))),
                   pl.BlockSpec((tm, tn), lambda i, j: (i, jnp.clip(j - n_small, 0, n_qkv - 1))),
                   pl.BlockSpec((tm, tn),
                                lambda i, j: (i, jnp.clip(j - n_small - n_qkv, 0, n_gate - 1)))],
        out_shape=[jax.ShapeDtypeStruct((rows, SMALL_W), F32),
                   jax.ShapeDtypeStruct((rows, 3 * WIDTH_B), F32),
                   jax.ShapeDtypeStruct((rows, WIDTH_A + WIDTH_B), BF16)],
        scratch_shapes=[pltpu.VMEM((tm, d), BF16)],
        compiler_params=_cparams(("parallel", "arbitrary")),
        name="in_proj",
    )(x2, mod3, g_pre, w_all)


def _prep_kernel(small_ref, gq_ref, gkv_ref, wuq_ref, wuqi_ref, wuk_ref,
                 cosq_ref, sinq_ref, cosi_ref, sini_ref, ck_ref, sk_ref, ci_ref, si_ref,
                 qcat_ref, qidx_ref, wt_ref, kcat_ref, kidx_ref, ckvt_ref, *, qscale):
    tq = small_ref.shape[0]
    nblk = tq // QUERY_BLOCK
    cq = small_ref[:, 0:Q_RANK]
    cqn = (cq * lax.rsqrt(jnp.mean(cq * cq, axis=-1, keepdims=True) + EPS)
           * gq_ref[...]).astype(BF16)
    nt = (((1,), (1,)), ((), ()))

    q_t = lax.dot_general(wuq_ref[...], cqn, nt, preferred_element_type=F32)
    cosq = cosq_ref[...]
    sinq = sinq_ref[...]
    half = ROPE_DIM // 2
    zpad = [jnp.zeros((KCAT - KV_RANK - ROPE_DIM, tq), F32)] if KCAT > KV_RANK + ROPE_DIM else []
    for h in range(N_HEADS_A):
        base = h * HEAD_DIM
        x1 = q_t[base:base + half]
        x2 = q_t[base + half:base + ROPE_DIM]
        q_rope = jnp.concatenate([x1 * cosq - x2 * sinq, x2 * cosq + x1 * sinq] + zpad, axis=0)
        q_lat = jnp.dot(wuk_ref[h], q_t[base:base + HEAD_DIM].astype(BF16),
                        preferred_element_type=F32)
        q_lat = (q_lat * qscale).astype(BF16)
        q_rope = (q_rope * qscale).astype(BF16)
        for blk in range(nblk):
            cols = slice(blk * QUERY_BLOCK, (blk + 1) * QUERY_BLOCK)
            unit, hu = divmod(h, N_HEADS_A // SCORE_UNITS)
            lanes = slice(hu * QUERY_BLOCK, (hu + 1) * QUERY_BLOCK)
            qcat_ref[blk, unit, 0:KV_RANK, lanes] = q_lat[:, cols]
            qcat_ref[blk, unit, KV_RANK:KCAT, lanes] = q_rope[:, cols]

    qi_t = lax.dot_general(wuqi_ref[...], cqn, nt, preferred_element_type=F32)
    cosi = cosi_ref[...]
    sini = sini_ref[...]
    ihalf = IDX_ROPE // 2
    ipad = [jnp.zeros((KIDX_PAD - IDX_DIM, tq), F32)] if KIDX_PAD > IDX_DIM else []
    for h in range(IDX_HEADS):
        base = h * IDX_DIM
        x1 = qi_t[base:base + ihalf]
        x2 = qi_t[base + ihalf:base + IDX_ROPE]
        qi = jnp.concatenate([x1 * cosi - x2 * sini, x2 * cosi + x1 * sini,
                              qi_t[base + IDX_ROPE:base + IDX_DIM]] + ipad, axis=0).astype(BF16)
        for blk in range(nblk):
            cols = slice(blk * QUERY_BLOCK, (blk + 1) * QUERY_BLOCK)
            qidx_ref[blk, :, h * QUERY_BLOCK:(h + 1) * QUERY_BLOCK] = qi[:, cols]

    lane = lax.broadcasted_iota(I32, (tq, LANE), 1)

    slab_b = small_ref[:, 896:1024]
    swap_b = jnp.where(lane < ihalf, pltpu.roll(slab_b, LANE - ihalf, 1), pltpu.roll(slab_b, ihalf, 1))
    kidx_ref[...] = (slab_b * ci_ref[...] + swap_b * si_ref[...])[:, :KIDX_PAD].astype(BF16)
    w_t = slab_b.T[IDX_DIM:IDX_DIM + IDX_HEADS] * (IDX_HEADS ** -0.5 * IDX_DIM ** -0.5)
    for blk in range(nblk):
        wt_ref[blk] = w_t[:, blk * QUERY_BLOCK:(blk + 1) * QUERY_BLOCK]

    ckv = small_ref[:, Q_RANK:Q_RANK + KV_RANK]
    cn = ckv * lax.rsqrt(jnp.mean(ckv * ckv, axis=-1, keepdims=True) + EPS) * gkv_ref[...]
    kcat_ref[:, 0:KV_RANK] = cn.astype(BF16)
    ckvt_ref[0] = cn.T.astype(BF16)
    slab_a = small_ref[:, 768:896]
    swap_a = jnp.where(lane < half, pltpu.roll(slab_a, LANE - half, 1), pltpu.roll(slab_a, half, 1))
    kcat_ref[:, KV_RANK:KCAT] = (slab_a * ck_ref[...] + swap_a * sk_ref[...])[:, :KCAT - KV_RANK].astype(BF16)


def _prep_call(small, g_q, g_kv, wuq_t, wuqi_t, wuk_t, tabs, seq):
    rows = small.shape[0]
    tq = PREP_TQ
    nblk = tq // QUERY_BLOCK
    tpb = seq // tq
    nqb = rows // QUERY_BLOCK
    qscale = HEAD_DIM ** -0.5 * LOG2E
    kern = functools.partial(_prep_kernel, qscale=qscale)
    const = lambda t: (0, 0)
    tcol = lambda t: (0, t % tpb)
    trow = lambda t: (t % tpb, 0)
    return pl.pallas_call(
        kern,
        grid=(rows // tq,),
        in_specs=[pl.BlockSpec((tq, SMALL_W), lambda t: (t, 0)),
                  pl.BlockSpec((1, Q_RANK), const),
                  pl.BlockSpec((1, KV_RANK), const),
                  pl.BlockSpec((WIDTH_A, Q_RANK), const),
                  pl.BlockSpec((IDX_HEADS * IDX_DIM, Q_RANK), const),
                  pl.BlockSpec((N_HEADS_A, KV_RANK, HEAD_DIM), lambda t: (0, 0, 0)),
                  pl.BlockSpec((ROPE_DIM // 2, tq), tcol),
                  pl.BlockSpec((ROPE_DIM // 2, tq), tcol),
                  pl.BlockSpec((IDX_ROPE // 2, tq), tcol),
                  pl.BlockSpec((IDX_ROPE // 2, tq), tcol),
                  pl.BlockSpec((tq, LANE), trow),
                  pl.BlockSpec((tq, LANE), trow),
                  pl.BlockSpec((tq, LANE), trow),
                  pl.BlockSpec((tq, LANE), trow)],
        out_specs=[pl.BlockSpec((nblk, SCORE_UNITS, KCAT, UNIT_COLS), lambda t: (t, 0, 0, 0)),
                   pl.BlockSpec((nblk, KIDX_PAD, IDX_HEADS * QUERY_BLOCK), lambda t: (t, 0, 0)),
                   pl.BlockSpec((nblk, IDX_HEADS, QUERY_BLOCK), lambda t: (t, 0, 0)),
                   pl.BlockSpec((tq, KCAT), lambda t: (t, 0)),
                   pl.BlockSpec((tq, KIDX_PAD), lambda t: (t, 0)),
                   pl.BlockSpec((1, KV_RANK, tq), lambda t: (t, 0, 0))],
        out_shape=[jax.ShapeDtypeStruct((nqb, SCORE_UNITS, KCAT, UNIT_COLS), BF16),
                   jax.ShapeDtypeStruct((nqb, KIDX_PAD, IDX_HEADS * QUERY_BLOCK), BF16),
                   jax.ShapeDtypeStruct((nqb, IDX_HEADS, QUERY_BLOCK), F32),
                   jax.ShapeDtypeStruct((rows, KCAT), BF16),
                   jax.ShapeDtypeStruct((rows, KIDX_PAD), BF16),
                   jax.ShapeDtypeStruct((rows // tq, KV_RANK, tq), BF16)],
        compiler_params=_cparams(("parallel",)),
        name="dsa_prep",
    )(small, g_q, g_kv, wuq_t, wuqi_t, wuk_t, *tabs)


def _float_of_rank(u):
    key = u ^ INT_MIN
    return lax.bitcast_convert_type(key ^ (lax.shift_right_arithmetic(key, 31) & 0x7FFFFFFF), F32)


_ABOVE_NEG = float(np.nextafter(np.float32(NEG), np.float32(0.0)))
TRIM_PASSES = 8
SELECT_BITS = 24
SCORE_UNITS = 2
UNIT_COLS = N_HEADS_A * QUERY_BLOCK // SCORE_UNITS


def _dsa_kernel(kidx_ref, kcat_ref, ckvt_ref, qcat_ref, qidx_ref, wt_ref, qidx_nx_ref, wt_nx_ref,
                gate_ref, wuv_ref, o_ref, isc_ref, m_ref, l_ref, acc_ref, s_ref, *, k_sel):
    i = pl.program_id(1)
    kc = KEY_CHUNK
    qb = QUERY_BLOCK
    per = kc // qb
    nch = (i + per) // per
    has_next = i + 1 < pl.num_programs(1)
    nch_next = (i + 1 + per) // per
    lane_q = lax.broadcasted_iota(I32, (kc, qb), 1)
    krow = lax.broadcasted_iota(I32, (kc, qb), 0)

    def chunk_rows(c):
        return pl.ds(c * kc if isinstance(c, int) else pl.multiple_of(c * kc, kc), kc)

    def fold(op, x):
        return op(x.reshape(kc // 8, 8, qb), axis=0)

    def chunk_pairs(step):
        def body(j, carry):
            step(2 * j)
            step(2 * j + 1)
            return carry

        lax.fori_loop(0, nch // 2, body, 0)

        @pl.when(nch % 2 == 1)
        def _():
            step(nch - 1)

    def index_chunk(c, q_ref, w_ref, block):
        kblk = kidx_ref[chunk_rows(c), :]
        acc = jnp.zeros((kc, qb), F32)
        group = 4
        for g in range(IDX_HEADS // group):
            lg = jnp.dot(kblk, q_ref[0, :, g * group * qb:(g + 1) * group * qb],
                         preferred_element_type=F32)
            for hh in range(group):
                h = g * group + hh
                acc = acc + jnp.maximum(lg[:, hh * qb:(hh + 1) * qb], 0.0) * w_ref[0, h:h + 1, :]
        causal = (c * kc + krow) <= (block * qb + lane_q)
        isc_ref[chunk_rows(c), :] = jnp.where(causal, acc, NEG)

    @pl.when(i == 0)
    def _():
        chunk_pairs(lambda c: index_chunk(c, qidx_ref, wt_ref, i))

    n_max = isc_ref.shape[0] // kc

    def count_chunks(n, t):
        parts = [fold(jnp.sum, jnp.where(isc_ref[chunk_rows(c), :] >= t, 1, 0).astype(I32))
                 for c in range(n)]
        while len(parts) > 1:
            parts = [a + b for a, b in zip(parts[::2], parts[1::2])] + parts[len(parts) & ~1:]
        return jnp.sum(parts[0], axis=0, keepdims=True)

    def bit_step(carry, bit, count):
        tu, cnt_t = carry
        cand_u = tu | lax.shift_left(jnp.int32(1), bit)
        cnt = count(_float_of_rank(cand_u))
        ok = cnt >= k_sel
        return jnp.where(ok, cand_u, tu), jnp.where(ok, cnt, cnt_t)

    def bis_body(s, carry):
        return bit_step(carry, 31 - s, lambda t: lax.switch(
            nch - 1, [functools.partial(count_chunks, n, t) for n in range(1, n_max + 1)]))

    start = (jnp.zeros((1, qb), I32), jnp.full((1, qb), k_sel, I32))
    units = s_ref.shape[0]
    bits_per_unit = SELECT_BITS // units

    def select_with_scores(n):
        def run():
            def body(u, carry):
                s_ref[u, 0:n * kc, :] = jnp.dot(kcat_ref[0:n * kc, :], qcat_ref[0, u],
                                                preferred_element_type=F32)
                for k in range(bits_per_unit):
                    carry = bit_step(carry, 31 - (u * bits_per_unit + k),
                                     functools.partial(count_chunks, n))
                return carry
            return lax.fori_loop(0, units, body, start)
        return run

    tu, cnt_t = lax.switch(nch - 1, [select_with_scores(n) for n in range(1, n_max + 1)])

    def trim(tu, cnt_t):
        thr = jnp.where(tu == 0, -jnp.inf, _float_of_rank(tu))
        few = thr < _ABOVE_NEG
        thr = jnp.where(few, _ABOVE_NEG, thr)
        cnt0 = jnp.where(few, k_sel, cnt_t)

        def trim_cond(state):
            _, cnt, n = state
            return (jnp.max(cnt) > k_sel) & (n < TRIM_PASSES)

        def trim_body(state):
            low, cnt, n = state

            def min_body(c, mn8):
                x = isc_ref[chunk_rows(c), :]
                sel = jnp.where(x >= thr, jnp.where(x > low, x, jnp.inf), jnp.inf)
                return jnp.minimum(mn8, fold(jnp.min, sel))

            mn = jnp.min(lax.fori_loop(0, nch, min_body, jnp.full((8, qb), jnp.inf, F32)),
                         axis=0, keepdims=True)

            def eq_body(c, cnt8):
                return cnt8 + fold(jnp.sum, jnp.where(isc_ref[chunk_rows(c), :] == mn, 1, 0).astype(I32))

            n_eq = jnp.sum(lax.fori_loop(0, nch, eq_body, jnp.zeros((8, qb), I32)),
                           axis=0, keepdims=True)
            can = (cnt > k_sel) & (cnt - n_eq >= k_sel)
            stalled = jnp.max(jnp.where(can, 1, 0)) == 0
            return (jnp.where(can, mn, low), jnp.where(can, cnt - n_eq, cnt),
                    jnp.where(stalled, TRIM_PASSES + 1, n + 1))

        low, cnt, n = lax.while_loop(trim_cond, trim_body,
                                     (jnp.full((1, qb), -jnp.inf, F32), cnt0, jnp.int32(0)))
        return thr, low, (n == TRIM_PASSES) & (jnp.max(cnt) > k_sel)

    thr, low, unresolved = trim(tu, cnt_t)
    thr, low = lax.cond(unresolved,
                        lambda: trim(*lax.fori_loop(SELECT_BITS, 32, bis_body, (tu, cnt_t)))[:2],
                        lambda: (thr, low))

    m_ref[...] = jnp.full(m_ref.shape, NEG, F32)
    l_ref[...] = jnp.zeros(l_ref.shape, F32)
    acc_ref[...] = jnp.zeros(acc_ref.shape, F32)

    def attend(c, index_next):
        rows = chunk_rows(c)
        x = isc_ref[rows, :]
        if index_next:
            index_chunk(c, qidx_nx_ref, wt_nx_ref, i + 1)
        bias = jnp.where(x >= thr, jnp.where(x > low, 0.0, NEG), NEG).astype(F32)
        s = (jnp.concatenate([s_ref[u, rows, :] for u in range(units)], axis=1)
             + jnp.concatenate([bias] * N_HEADS_A, axis=1))
        m_old = m_ref[...]
        m_new = jnp.maximum(m_old, jnp.max(s, axis=0, keepdims=True))
        alpha = jnp.exp2(m_old - m_new)
        p = jnp.exp2(s - m_new)
        l_ref[...] = alpha * l_ref[...] + jnp.sum(p, axis=0, keepdims=True)
        acc_ref[...] = alpha * acc_ref[...] + jnp.dot(ckvt_ref[c], p.astype(BF16),
                                                      preferred_element_type=F32)
        m_ref[...] = m_new

    @pl.when(has_next)
    def _():
        chunk_pairs(functools.partial(attend, index_next=True))

        @pl.when(nch_next > nch)
        def _():
            index_chunk(nch, qidx_nx_ref, wt_nx_ref, i + 1)

    @pl.when(jnp.logical_not(has_next))
    def _():
        chunk_pairs(functools.partial(attend, index_next=False))

    inv_l = 1.0 / l_ref[...]
    outs = []
    for h in range(N_HEADS_A):
        cols = slice(h * qb, (h + 1) * qb)
        o_t = acc_ref[:, cols] * inv_l[:, cols]
        outs.append(jnp.dot(o_t.T.astype(BF16), wuv_ref[h], preferred_element_type=F32))
    o = jnp.concatenate(outs, axis=1)
    g = gate_ref[...].astype(F32)
    o_ref[...] = (o * _silu(g)).astype(BF16)


def _dsa_call(kidx, kcat, ckvt, qcat, qidx, w_t, gates, wuv, batch, seq):
    rows = kidx.shape[0]
    nb = seq // QUERY_BLOCK
    nkc = seq // KEY_CHUNK
    k_sel = min(TOPK_MAX, seq // 4)
    kern = functools.partial(_dsa_kernel, k_sel=k_sel)
    nxt = lambda b, i: (b * nb + jnp.minimum(i + 1, nb - 1), 0, 0)
    return pl.pallas_call(
        kern,
        grid=(batch, nb),
        in_specs=[pl.BlockSpec((seq, KIDX_PAD), lambda b, i: (b, 0)),
                  pl.BlockSpec((seq, KCAT), lambda b, i: (b, 0)),
                  pl.BlockSpec((nkc, KV_RANK, KEY_CHUNK), lambda b, i: (b, 0, 0)),
                  pl.BlockSpec((1, SCORE_UNITS, KCAT, UNIT_COLS), lambda b, i: (b * nb + i, 0, 0, 0)),
                  pl.BlockSpec((1, KIDX_PAD, IDX_HEADS * QUERY_BLOCK), lambda b, i: (b * nb + i, 0, 0)),
                  pl.BlockSpec((1, IDX_HEADS, QUERY_BLOCK), lambda b, i: (b * nb + i, 0, 0)),
                  pl.BlockSpec((1, KIDX_PAD, IDX_HEADS * QUERY_BLOCK), nxt),
                  pl.BlockSpec((1, IDX_HEADS, QUERY_BLOCK), nxt),
                  pl.BlockSpec((QUERY_BLOCK, WIDTH_A), lambda b, i: (b * nb + i, 0)),
                  pl.BlockSpec((N_HEADS_A, KV_RANK, HEAD_DIM), lambda b, i: (0, 0, 0))],
        out_specs=pl.BlockSpec((QUERY_BLOCK, WIDTH_A), lambda b, i: (b * nb + i, 0)),
        out_shape=jax.ShapeDtypeStruct((rows, WIDTH_A), BF16),
        scratch_shapes=[pltpu.VMEM((seq, QUERY_BLOCK), F32),
                        pltpu.VMEM((1, N_HEADS_A * QUERY_BLOCK), F32),
                        pltpu.VMEM((1, N_HEADS_A * QUERY_BLOCK), F32),
                        pltpu.VMEM((KV_RANK, N_HEADS_A * QUERY_BLOCK), F32),
                        pltpu.VMEM((SCORE_UNITS, seq, UNIT_COLS), F32)],
        compiler_params=_cparams(("arbitrary", "arbitrary")),
        name="dsa_attn",
    )(kidx, kcat, ckvt, qcat, qidx, w_t, qidx, w_t, gates, wuv)


def _dil_kernel(q_ref, k_ref, v_ref, gate_ref, c_ref, s_ref, o_ref,
                nat_ref, a4_ref, qn, kn, vn, q4, k4, v4, q16, k16, v16,
                oc0, ms0, ls0, oc1, ms1, ls1, oc2, ms2, ls2, *, qscale, unroll):
    seq = q_ref.shape[0]
    blk = N_BACK
    rc = ROW_CHUNK
    m4 = seq // 4
    m16 = seq // 16

    def build(src_ref, dn, d4, d16, rope_scale):
        def nat_body(c, carry):
            rows = pl.ds(pl.multiple_of(c * rc, rc), rc)
            x = src_ref[rows, :]
            if rope_scale is not None:
                x = x * c_ref[rows, :] + pltpu.roll(x, LANE // 2, 1) * s_ref[rows, :]
                if rope_scale != 1.0:
                    x = x * rope_scale
                nat_ref[rows, :] = x
            dn[rows, :] = x.astype(BF16)
            return carry

        lax.fori_loop(0, seq // rc, nat_body, 0)
        nat = src_ref if rope_scale is None else nat_ref

        def a4_body(c, carry):
            r4 = c // (m4 // rc)
            j0 = (c % (m4 // rc)) * rc
            x = nat[pl.ds(r4 + 4 * j0, rc, stride=4), :]
            rows = pl.ds(pl.multiple_of(c * rc, rc), rc)
            a4_ref[rows, :] = x
            d4[rows, :] = x.astype(BF16)
            return carry

        lax.fori_loop(0, seq // rc, a4_body, 0)

        def a16_body(r16, carry):
            r4 = r16 % 4
            s = r16 // 4
            x = a4_ref[pl.ds(r4 * m4 + s, m16, stride=4), :]
            d16[pl.ds(pl.multiple_of(r16 * m16, m16), m16), :] = x.astype(BF16)
            return carry

        lax.fori_loop(0, 16, a16_body, 0)

    build(q_ref, qn, q4, q16, qscale)
    build(k_ref, kn, k4, k16, 1.0)
    build(v_ref, vn, v4, v16, None)

    qi = lax.broadcasted_iota(I32, (blk, 2 * blk), 0) + blk
    kj = lax.broadcasted_iota(I32, (blk, 2 * blk), 1)
    rel = qi - kj
    bias_band = jnp.where((rel >= 0) & (rel <= N_BACK), 0.0, NEG).astype(F32)
    no_prev = jnp.where(kj < blk, NEG, 0.0).astype(F32)
    qi1 = lax.broadcasted_iota(I32, (blk, blk), 0)
    kj1 = lax.broadcasted_iota(I32, (blk, blk), 1)
    bias_first = jnp.where(kj1 <= qi1, 0.0, NEG).astype(F32)
    nt = (((1,), (1,)), ((), ()))

    def run_config(qs, ks, vs, nmb, store):
        def rows_of(u, first):
            cur = pl.ds(pl.multiple_of(u * blk, blk), blk)
            if first is True:
                return (cur,)
            return (pl.ds(pl.multiple_of(jnp.maximum(u * blk - blk, 0), blk), blk), cur)

        def gather(ref, rows):
            parts = [ref[r, :] for r in rows]
            return parts[0] if len(parts) == 1 else jnp.concatenate(parts, axis=0)

        def loop_body(g, carry):
            units = []
            for t in range(unroll):
                if nmb <= unroll:
                    first = (t % nmb == 0)
                else:
                    first = jnp.where(g == 0, 1.0, 0.0).astype(F32) if t == 0 else False
                units.append((g * unroll + t, first))
            scores = []
            for u, first in units:
                rows = rows_of(u, first)
                if first is True:
                    bias = bias_first
                elif first is False:
                    bias = bias_band
                else:
                    bias = bias_band + no_prev * first
                scores.append(lax.dot_general(qs[rows[-1], :], gather(ks, rows), nt,
                                              preferred_element_type=F32) + bias)
            probs = []
            for s in scores:
                m = jnp.max(s, axis=-1, keepdims=True)
                p = jnp.exp2(s - m)
                l = jnp.sum(p, axis=-1, keepdims=True)
                probs.append((p.astype(BF16), jnp.broadcast_to(m, (blk, LANE)),
                              jnp.broadcast_to(l, (blk, LANE))))
            for (u, first), (p, m_b, l_b) in zip(units, probs):
                store(u, jnp.dot(p, gather(vs, rows_of(u, first)), preferred_element_type=F32), m_b, l_b)
            return carry

        lax.fori_loop(0, seq // blk // unroll, loop_body, 0)

    def store_contig(oc, ms, ls):
        def store(u, o, m_b, l_b):
            rows = pl.ds(pl.multiple_of(u * blk, blk), blk)
            oc[rows, :] = o
            ms[rows, :] = m_b
            ls[rows, :] = l_b
        return store

    def store_a16_to_a4(u, o, m_b, l_b):
        r16 = u // (m16 // blk)
        mb = u % (m16 // blk)
        rows = pl.ds((r16 % 4) * m4 + 4 * blk * mb + r16 // 4, blk, stride=4)
        oc2[rows, :] = o
        ms2[rows, :] = m_b
        ls2[rows, :] = l_b

    run_config(qn, kn, vn, seq // blk, store_contig(oc0, ms0, ls0))
    run_config(q4, k4, v4, m4 // blk, store_contig(oc1, ms1, ls1))
    run_config(q16, k16, v16, m16 // blk, store_a16_to_a4)

    def merge_body(c, carry):
        r4 = c // (m4 // rc)
        j0 = (c % (m4 // rc)) * rc
        nat_rows = pl.ds(r4 + 4 * j0, rc, stride=4)
        rows = pl.ds(pl.multiple_of(c * rc, rc), rc)
        m0 = ms0[nat_rows, :]
        m1 = ms1[rows, :]
        m2 = ms2[rows, :]
        mx = jnp.maximum(jnp.maximum(m0, m1), m2)
        e0 = jnp.exp2(m0 - mx)
        e1 = jnp.exp2(m1 - mx)
        e2 = jnp.exp2(m2 - mx)
        den = e0 * ls0[nat_rows, :] + e1 * ls1[rows, :] + e2 * ls2[rows, :]
        nat_ref[nat_rows, :] = (e0 * oc0[nat_rows, :] + e1 * oc1[rows, :] + e2 * oc2[rows, :]) / den
        return carry

    lax.fori_loop(0, seq // rc, merge_body, 0)

    def gate_body(c, carry):
        rows = pl.ds(pl.multiple_of(c * rc, rc), rc)
        g = gate_ref[rows, :].astype(F32)
        o_ref[rows, :] = (nat_ref[rows, :] * _silu(g)).astype(BF16)
        return carry

    lax.fori_loop(0, seq // rc, gate_body, 0)


def _dil_call(qkv, gates, ctab, stab, batch, seq):
    rows = qkv.shape[0]
    nh = N_HEADS_B
    qscale = HEAD_DIM ** -0.5 * LOG2E
    kern = functools.partial(_dil_kernel, qscale=qscale, unroll=16)
    blk = (seq, HEAD_DIM)
    return pl.pallas_call(
        kern,
        grid=(batch, nh),
        in_specs=[pl.BlockSpec(blk, lambda b, h: (b, h)),
                  pl.BlockSpec(blk, lambda b, h: (b, nh + h)),
                  pl.BlockSpec(blk, lambda b, h: (b, 2 * nh + h)),
                  pl.BlockSpec(blk, lambda b, h: (b, nh + h)),
                  pl.BlockSpec(blk, lambda b, h: (0, 0), pipeline_mode=pl.Buffered(1)),
                  pl.BlockSpec(blk, lambda b, h: (0, 0), pipeline_mode=pl.Buffered(1))],
        out_specs=pl.BlockSpec(blk, lambda b, h: (b, h)),
        out_shape=jax.ShapeDtypeStruct((rows, WIDTH_B), BF16),
        scratch_shapes=([pltpu.VMEM(blk, F32)] * 2 + [pltpu.VMEM(blk, BF16)] * 9
                        + [pltpu.VMEM(blk, F32)] * 9),
        compiler_params=_cparams(("parallel", "arbitrary")),
        name="dilated",
    )(qkv, qkv, qkv, gates, ctab, stab)


def _out_kernel(oa_ref, ob_ref, w_ref, x_ref, mod_ref, g_ref, o_ref):
    y = jnp.dot(oa_ref[...], w_ref[0:WIDTH_A, :], preferred_element_type=F32)
    y = y + jnp.dot(ob_ref[...], w_ref[WIDTH_A:WIDTH_A + WIDTH_B, :], preferred_element_type=F32)
    yn = y * lax.rsqrt(jnp.mean(y * y, axis=-1, keepdims=True) + EPS) * g_ref[...]
    o_ref[...] = x_ref[...] + mod_ref[0, 2:3, :] * yn


def _out_call(o_a, o_b, w_out, x2, mod3, g_post, seq):
    rows, d = x2.shape
    tm = OUT_TM
    tpb = seq // tm
    return pl.pallas_call(
        _out_kernel,
        grid=(rows // tm,),
        in_specs=[pl.BlockSpec((tm, WIDTH_A), lambda i: (i, 0)),
                  pl.BlockSpec((tm, WIDTH_B), lambda i: (i, 0)),
                  pl.BlockSpec((WIDTH_A + WIDTH_B, d), lambda i: (0, 0)),
                  pl.BlockSpec((tm, d), lambda i: (i, 0)),
                  pl.BlockSpec((1, 3, d), lambda i: (i // tpb, 0, 0)),
                  pl.BlockSpec((1, d), lambda i: (0, 0))],
        out_specs=pl.BlockSpec((tm, d), lambda i: (i, 0)),
        out_shape=jax.ShapeDtypeStruct((rows, d), F32),
        compiler_params=_cparams(("parallel",)),
        name="out_proj",
    )(o_a, o_b, w_out, x2, mod3, g_post)


def _rope_tables(seq):
    pos = jnp.arange(seq, dtype=F32)

    def cs(n_rot):
        inv = ROPE_THETA ** (-jnp.arange(0, n_rot, 2, dtype=F32) / n_rot)
        ang = pos[:, None] * inv[None, :]
        return jnp.cos(ang), jnp.sin(ang)

    c32, s32 = cs(ROPE_DIM)
    c16, s16 = cs(IDX_ROPE)
    ck = jnp.concatenate([c32, c32, jnp.ones((seq, LANE - ROPE_DIM), F32)], axis=1)
    sk = jnp.concatenate([-s32, s32, jnp.zeros((seq, LANE - ROPE_DIM), F32)], axis=1)
    ci = jnp.concatenate([c16, c16, jnp.ones((seq, LANE - IDX_ROPE), F32)], axis=1)
    si = jnp.concatenate([-s16, s16, jnp.zeros((seq, LANE - IDX_ROPE), F32)], axis=1)
    gap = jnp.ones((seq, LANE // 2 - ROPE_DIM // 2), F32)
    cd = jnp.concatenate([c32, gap, c32, gap], axis=1)
    sd = jnp.concatenate([-s32, 0.0 * gap, s32, 0.0 * gap], axis=1)
    return (c32.T, s32.T, c16.T, s16.T, ck, sk, ci, si, cd, sd)


def _pair_heads(wt):
    half = ROPE_DIM // 2
    mid = ROPE_DIM + (LANE // 2 - half)
    w3 = wt.reshape(-1, HEAD_DIM, wt.shape[-1])
    w3 = jnp.concatenate([w3[:, :half], w3[:, ROPE_DIM:mid], w3[:, half:ROPE_DIM], w3[:, mid:]], axis=1)
    return w3.reshape(wt.shape)


def _layer(x, c, w_ada, b_ada, g_pre, g_post, w_in, g_q, g_kv, w_uq, w_uq_idx, w_uk, w_uv, w_out):
    batch, seq, d = x.shape
    assert seq % PROJ_TM == 0 or seq < PROJ_TM
    assert seq % KEY_CHUNK == 0 and seq // 16 >= N_BACK
    assert all(w // dl == N_BACK for w, dl in DILATED_CONFIGS)
    rows = batch * seq
    x2 = x.reshape(rows, d)

    o_cq, o_ckv, o_kr, o_ki, o_wi = 0, 512, 768, 800, 864
    o_ga, o_q, o_k, o_v, o_gb = 880, 1904, 2928, 3952, 4976
    wt = w_in.T.astype(BF16)
    zeros = lambda n: jnp.zeros((n, d), BF16)
    w_all = jnp.concatenate([
        wt[o_cq:o_ki], zeros(896 - 800), wt[o_ki:o_ga], zeros(SMALL_W - 976),
        _pair_heads(wt[o_q:o_k]), _pair_heads(wt[o_k:o_v]), wt[o_v:o_gb],
        wt[o_ga:o_q], wt[o_gb:o_gb + WIDTH_B]], axis=0)
    assert w_all.shape[0] == IN_PAD

    c_pad = jnp.zeros((PACK16, d), F32).at[:batch].set(c)
    mod = _mod_call(c_pad, w_ada, b_ada.reshape(1, -1))[:batch]
    mod3 = mod.reshape(batch, 3, d)

    small, qkv, gates = _proj_call(x2, mod3, g_pre.reshape(1, d), w_all, seq)

    tabs = _rope_tables(seq)
    wuk_t = jnp.transpose(jnp.pad(w_uk, ((0, 0), (ROPE_DIM, 0), (0, 0))), (0, 2, 1)).astype(BF16)
    qcat, qidx, w_t, kcat, kidx, ckvt = _prep_call(
        small, g_q.reshape(1, -1), g_kv.reshape(1, -1), w_uq.T.astype(BF16),
        w_uq_idx.T.astype(BF16), wuk_t, tabs[:8], seq)

    o_a = _dsa_call(kidx, kcat, ckvt, qcat, qidx, w_t, gates, w_uv.astype(BF16), batch, seq)
    o_b = _dil_call(qkv, gates, tabs[8], tabs[9], batch, seq)
    out = _out_call(o_a, o_b, w_out.astype(BF16), x2, mod3, g_post.reshape(1, d), seq)
    return out.reshape(batch, seq, d)


def kernel(x, c, w_ada, b_ada, g_pre, g_post, w_in, g_q, g_kv, w_uq, w_uq_idx, w_uk, w_uv, w_out):
    for layer in range(w_ada.shape[0]):
        x = _layer(x, c, w_ada[layer], b_ada[layer], g_pre[layer], g_post[layer], w_in[layer],
                   g_q[layer], g_kv[layer], w_uq[layer], w_uq_idx[layer], w_uk[layer],
                   w_uv[layer], w_out[layer])
    return x
```

```python
import functools

import numpy as np
import jax
import jax.numpy as jnp
from jax import lax
from jax.experimental import pallas as pl
from jax.experimental.pallas import tpu as pltpu

F32 = jnp.float32
BF16 = jnp.bfloat16
I32 = jnp.int32

HEAD_DIM = 128
ROPE_DIM = HEAD_DIM // 4
ROPE_THETA = 500000.0
EPS = 1e-6
NEG = -1e30
N_HEADS_A = 8
WIDTH_A = N_HEADS_A * HEAD_DIM
Q_RANK = 512
KV_RANK = 256
IDX_HEADS = 16
IDX_DIM = 64
IDX_ROPE = IDX_DIM // 4
TOPK_MAX = 256
QUERY_BLOCK = 128
N_HEADS_B = 8
WIDTH_B = N_HEADS_B * HEAD_DIM
DILATED_CONFIGS = ((128, 1), (512, 4), (2048, 16))
N_BACK = 128
SMALL_W = 1024
IN_PAD = SMALL_W + 5 * 1024
KCAT = 288
KIDX_PAD = 64
INT_MIN = -(2 ** 31)
LOG2E = 1.4426950408889634

LANE = 128
PACK16 = 16
SLAB_KROPE = Q_RANK + KV_RANK
SLAB_KIDX = SLAB_KROPE + LANE
assert SLAB_KIDX + LANE == SMALL_W
DIL_UNROLL = 16
VMEM_LIMIT = 56 * 1024 * 1024
PROJ_TM = 1024
PROJ_TN = 1024
PREP_TQ = 512
KEY_CHUNK = 512
OUT_TM = 512
ROW_CHUNK = 512
NORM_ROWS = 128


def _cparams(sem):
    return pltpu.CompilerParams(dimension_semantics=sem, vmem_limit_bytes=VMEM_LIMIT)


def _silu(g):
    return g * jax.nn.sigmoid(g)


def _split_bf16(x):
    hi = x.astype(BF16)
    return hi, (x - hi.astype(F32)).astype(BF16)


def _mod_kernel(c_ref, w_ref, b_ref, o_ref):
    a_hi, a_lo = _split_bf16(_silu(c_ref[...]))
    w_hi, w_lo = _split_bf16(w_ref[...])
    dot = functools.partial(jnp.dot, preferred_element_type=F32)
    o_ref[...] = dot(a_hi, w_hi) + (dot(a_hi, w_lo) + dot(a_lo, w_hi)) + b_ref[...]


def _mod_call(c_pad, w_ada, b_ada):
    rows, d = c_pad.shape
    n = w_ada.shape[1]
    tn = 1024
    return pl.pallas_call(
        _mod_kernel,
        grid=(n // tn,),
        in_specs=[pl.BlockSpec((rows, d), lambda j: (0, 0)),
                  pl.BlockSpec((d, tn), lambda j: (0, j)),
                  pl.BlockSpec((1, tn), lambda j: (0, j))],
        out_specs=pl.BlockSpec((rows, tn), lambda j: (0, j)),
        out_shape=jax.ShapeDtypeStruct((rows, n), F32),
        compiler_params=_cparams(("arbitrary",)),
        name="mod",
    )(c_pad, w_ada, b_ada)


def _proj_kernel(x_ref, mod_ref, g_ref, w_ref, small_ref, qkv_ref, gates_ref, h_ref, *,
                 n_small, n_qkv):
    j = pl.program_id(1)
    tm = x_ref.shape[0]
    half = tm // 2
    assert n_small == 1 and half % NORM_ROWS == 0

    def project(rows=slice(None)):
        return lax.dot_general(h_ref[rows, :], w_ref[...], (((1,), (1,)), ((), ())),
                               preferred_element_type=F32)

    @pl.when(j == 0)
    def _():
        shift = mod_ref[0, 0:1, :]
        scale1 = 1.0 + mod_ref[0, 1:2, :]
        g = g_ref[...]

        def norm_rows(rows):
            x = x_ref[rows, :]
            ms = jnp.mean(x * x, axis=-1, keepdims=True)
            y = x * lax.rsqrt(ms + EPS) * g
            h_ref[rows, :] = (y * scale1 + shift).astype(BF16)

        def body(r, carry):
            norm_rows(pl.ds(pl.multiple_of(r * NORM_ROWS, NORM_ROWS), NORM_ROWS))
            return carry

        lax.fori_loop(0, half // NORM_ROWS, body, 0)
        small_ref[0:half, :] = project(slice(0, half))
        for r in range(half // NORM_ROWS):
            norm_rows(slice(half + r * NORM_ROWS, half + (r + 1) * NORM_ROWS))
        small_ref[half:tm, :] = project(slice(half, tm))

    @pl.when((j >= n_small) & (j < n_small + n_qkv))
    def _():
        qkv_ref[...] = project()

    @pl.when(j >= n_small + n_qkv)
    def _():
        gates_ref[...] = project().astype(BF16)


def _proj_call(x2, mod3, g_pre, w_all, seq):
    rows, d = x2.shape
    tm = min(PROJ_TM, seq)
    tn = PROJ_TN
    n_small = SMALL_W // tn
    n_qkv = 3 * WIDTH_B // tn
    n_gate = (WIDTH_A + WIDTH_B) // tn
    tiles_per_batch = seq // tm
    kern = functools.partial(_proj_kernel, n_small=n_small, n_qkv=n_qkv)
    return pl.pallas_call(
        kern,
        grid=(rows // tm, n_small + n_qkv + n_gate),
        in_specs=[pl.BlockSpec((tm, d), lambda i, j: (i, 0)),
                  pl.BlockSpec((1, 3, d), lambda i, j: (i // tiles_per_batch, 0, 0)),
                  pl.BlockSpec((1, d), lambda i, j: (0, 0)),
                  pl.BlockSpec((tn, d), lambda i, j: (j, 0))],
        out_specs=[pl.BlockSpec((tm, tn), lambda i, j: (i, jnp.minimum(j, n_small - 1))),
                   pl.BlockSpec((tm, tn), lambda i, j: (i, jnp.clip(j - n_small, 0, n_qkv - 1))),
                   pl.BlockSpec((tm, tn),
                                lambda i, j: (i, jnp.clip(j - n_small - n_qkv, 0, n_gate - 1)))],
        out_shape=[jax.ShapeDtypeStruct((rows, SMALL_W), F32),
                   jax.ShapeDtypeStruct((rows, 3 * WIDTH_B), F32),
                   jax.ShapeDtypeStruct((rows, WIDTH_A + WIDTH_B), BF16)],
        scratch_shapes=[pltpu.VMEM((tm, d), BF16)],
        compiler_params=_cparams(("parallel", "arbitrary")),
        name="in_proj",
    )(x2, mod3, g_pre, w_all)


def _prep_kernel(small_ref, gq_ref, gkv_ref, wuq_ref, wuqi_ref, wuk_ref,
                 cosq_ref, sinq_ref, cosi_ref, sini_ref, ck_ref, sk_ref, ci_ref, si_ref,
                 qcat_ref, qidx_ref, wt_ref, kcat_ref, kidx_ref, ckvt_ref, *, qscale):
    tq = small_ref.shape[0]
    nblk = tq // QUERY_BLOCK
    cq = small_ref[:, 0:Q_RANK]
    cqn = (cq * lax.rsqrt(jnp.mean(cq * cq, axis=-1, keepdims=True) + EPS)
           * gq_ref[...]).astype(BF16)
    nt = (((1,), (1,)), ((), ()))

    q_t = lax.dot_general(wuq_ref[...], cqn, nt, preferred_element_type=F32)
    cosq = cosq_ref[...]
    sinq = sinq_ref[...]
    half = ROPE_DIM // 2
    zpad = [jnp.zeros((KCAT - KV_RANK - ROPE_DIM, tq), F32)] if KCAT > KV_RANK + ROPE_DIM else []
    for h in range(N_HEADS_A):
        base = h * HEAD_DIM
        x1 = q_t[base:base + half]
        x2 = q_t[base + half:base + ROPE_DIM]
        q_rope = jnp.concatenate([x1 * cosq - x2 * sinq, x2 * cosq + x1 * sinq] + zpad, axis=0)
        q_lat = jnp.dot(wuk_ref[h], q_t[base:base + HEAD_DIM].astype(BF16),
                        preferred_element_type=F32)
        q_lat = (q_lat * qscale).astype(BF16)
        q_rope = (q_rope * qscale).astype(BF16)
        for blk in range(nblk):
            cols = slice(blk * QUERY_BLOCK, (blk + 1) * QUERY_BLOCK)
            unit, hu = divmod(h, N_HEADS_A // SCORE_UNITS)
            lanes = slice(hu * QUERY_BLOCK, (hu + 1) * QUERY_BLOCK)
            qcat_ref[blk, unit, 0:KV_RANK, lanes] = q_lat[:, cols]
            qcat_ref[blk, unit, KV_RANK:KCAT, lanes] = q_rope[:, cols]

    qi_t = lax.dot_general(wuqi_ref[...], cqn, nt, preferred_element_type=F32)
    cosi = cosi_ref[...]
    sini = sini_ref[...]
    ihalf = IDX_ROPE // 2
    ipad = [jnp.zeros((KIDX_PAD - IDX_DIM, tq), F32)] if KIDX_PAD > IDX_DIM else []
    for h in range(IDX_HEADS):
        base = h * IDX_DIM
        x1 = qi_t[base:base + ihalf]
        x2 = qi_t[base + ihalf:base + IDX_ROPE]
        qi = jnp.concatenate([x1 * cosi - x2 * sini, x2 * cosi + x1 * sini,
                              qi_t[base + IDX_ROPE:base + IDX_DIM]] + ipad, axis=0).astype(BF16)
        for blk in range(nblk):
            cols = slice(blk * QUERY_BLOCK, (blk + 1) * QUERY_BLOCK)
            qidx_ref[blk, :, h * QUERY_BLOCK:(h + 1) * QUERY_BLOCK] = qi[:, cols]

    lane = lax.broadcasted_iota(I32, (tq, LANE), 1)

    slab_b = small_ref[:, SLAB_KIDX:SLAB_KIDX + LANE]
    swap_b = jnp.where(lane < ihalf, pltpu.roll(slab_b, LANE - ihalf, 1), pltpu.roll(slab_b, ihalf, 1))
    kidx_ref[...] = (slab_b * ci_ref[...] + swap_b * si_ref[...])[:, :KIDX_PAD].astype(BF16)
    w_t = slab_b.T[IDX_DIM:IDX_DIM + IDX_HEADS] * (IDX_HEADS ** -0.5 * IDX_DIM ** -0.5)
    for blk in range(nblk):
        wt_ref[blk] = w_t[:, blk * QUERY_BLOCK:(blk + 1) * QUERY_BLOCK]

    ckv = small_ref[:, Q_RANK:Q_RANK + KV_RANK]
    cn = ckv * lax.rsqrt(jnp.mean(ckv * ckv, axis=-1, keepdims=True) + EPS) * gkv_ref[...]
    kcat_ref[:, 0:KV_RANK] = cn.astype(BF16)
    ckvt_ref[0] = cn.T.astype(BF16)
    slab_a = small_ref[:, SLAB_KROPE:SLAB_KROPE + LANE]
    swap_a = jnp.where(lane < half, pltpu.roll(slab_a, LANE - half, 1), pltpu.roll(slab_a, half, 1))
    kcat_ref[:, KV_RANK:KCAT] = (slab_a * ck_ref[...] + swap_a * sk_ref[...])[:, :KCAT - KV_RANK].astype(BF16)


def _prep_call(small, g_q, g_kv, wuq_t, wuqi_t, wuk_t, tabs, seq):
    rows = small.shape[0]
    tq = PREP_TQ
    nblk = tq // QUERY_BLOCK
    tpb = seq // tq
    nqb = rows // QUERY_BLOCK
    qscale = HEAD_DIM ** -0.5 * LOG2E
    kern = functools.partial(_prep_kernel, qscale=qscale)
    const = lambda t: (0, 0)
    tcol = lambda t: (0, t % tpb)
    trow = lambda t: (t % tpb, 0)
    return pl.pallas_call(
        kern,
        grid=(rows // tq,),
        in_specs=[pl.BlockSpec((tq, SMALL_W), lambda t: (t, 0)),
                  pl.BlockSpec((1, Q_RANK), const),
                  pl.BlockSpec((1, KV_RANK), const),
                  pl.BlockSpec((WIDTH_A, Q_RANK), const),
                  pl.BlockSpec((IDX_HEADS * IDX_DIM, Q_RANK), const),
                  pl.BlockSpec((N_HEADS_A, KV_RANK, HEAD_DIM), lambda t: (0, 0, 0)),
                  pl.BlockSpec((ROPE_DIM // 2, tq), tcol),
                  pl.BlockSpec((ROPE_DIM // 2, tq), tcol),
                  pl.BlockSpec((IDX_ROPE // 2, tq), tcol),
                  pl.BlockSpec((IDX_ROPE // 2, tq), tcol),
                  pl.BlockSpec((tq, LANE), trow),
                  pl.BlockSpec((tq, LANE), trow),
                  pl.BlockSpec((tq, LANE), trow),
                  pl.BlockSpec((tq, LANE), trow)],
        out_specs=[pl.BlockSpec((nblk, SCORE_UNITS, KCAT, UNIT_COLS), lambda t: (t, 0, 0, 0)),
                   pl.BlockSpec((nblk, KIDX_PAD, IDX_HEADS * QUERY_BLOCK), lambda t: (t, 0, 0)),
                   pl.BlockSpec((nblk, IDX_HEADS, QUERY_BLOCK), lambda t: (t, 0, 0)),
                   pl.BlockSpec((tq, KCAT), lambda t: (t, 0)),
                   pl.BlockSpec((tq, KIDX_PAD), lambda t: (t, 0)),
                   pl.BlockSpec((1, KV_RANK, tq), lambda t: (t, 0, 0))],
        out_shape=[jax.ShapeDtypeStruct((nqb, SCORE_UNITS, KCAT, UNIT_COLS), BF16),
                   jax.ShapeDtypeStruct((nqb, KIDX_PAD, IDX_HEADS * QUERY_BLOCK), BF16),
                   jax.ShapeDtypeStruct((nqb, IDX_HEADS, QUERY_BLOCK), F32),
                   jax.ShapeDtypeStruct((rows, KCAT), BF16),
                   jax.ShapeDtypeStruct((rows, KIDX_PAD), BF16),
                   jax.ShapeDtypeStruct((rows // tq, KV_RANK, tq), BF16)],
        compiler_params=_cparams(("parallel",)),
        name="dsa_prep",
    )(small, g_q, g_kv, wuq_t, wuqi_t, wuk_t, *tabs)


def _float_of_rank(u):
    key = u ^ INT_MIN
    return lax.bitcast_convert_type(key ^ (lax.shift_right_arithmetic(key, 31) & 0x7FFFFFFF), F32)


_ABOVE_NEG = float(np.nextafter(np.float32(NEG), np.float32(0.0)))
TRIM_PASSES = 8
SELECT_BITS = 24
SCORE_UNITS = 2
UNIT_COLS = N_HEADS_A * QUERY_BLOCK // SCORE_UNITS


def _dsa_kernel(kidx_ref, kcat_ref, ckvt_ref, qcat_ref, qidx_ref, wt_ref, gate_ref, wuv_ref,
                o_ref, isc_ref, m_ref, l_ref, acc_ref, s_ref, *, k_sel):
    i = pl.program_id(1)
    kc = KEY_CHUNK
    qb = QUERY_BLOCK
    per = kc // qb
    nch = (i + per) // per
    qpos = i * qb + lax.broadcasted_iota(I32, (kc, qb), 1)
    krow = lax.broadcasted_iota(I32, (kc, qb), 0)

    def chunk_rows(c):
        return pl.ds(c * kc if isinstance(c, int) else pl.multiple_of(c * kc, kc), kc)

    def fold(op, x):
        return op(x.reshape(kc // 8, 8, qb), axis=0)

    def chunk_pairs(step):
        def body(j, carry):
            step(2 * j)
            step(2 * j + 1)
            return carry

        lax.fori_loop(0, nch // 2, body, 0)

        @pl.when(nch % 2 == 1)
        def _():
            step(nch - 1)

    def index_chunk(c):
        kblk = kidx_ref[chunk_rows(c), :]
        acc = jnp.zeros((kc, qb), F32)
        group = 4
        for g in range(IDX_HEADS // group):
            lg = jnp.dot(kblk, qidx_ref[0, :, g * group * qb:(g + 1) * group * qb],
                         preferred_element_type=F32)
            for hh in range(group):
                h = g * group + hh
                acc = acc + jnp.maximum(lg[:, hh * qb:(hh + 1) * qb], 0.0) * wt_ref[0, h:h + 1, :]
        causal = (c * kc + krow) <= qpos
        isc_ref[chunk_rows(c), :] = jnp.where(causal, acc, NEG)

    chunk_pairs(index_chunk)

    n_max = isc_ref.shape[0] // kc

    def count_chunks(n, t):
        parts = [fold(jnp.sum, jnp.where(isc_ref[chunk_rows(c), :] >= t, 1, 0).astype(I32))
                 for c in range(n)]
        while len(parts) > 1:
            parts = [a + b for a, b in zip(parts[::2], parts[1::2])] + parts[len(parts) & ~1:]
        return jnp.sum(parts[0], axis=0, keepdims=True)

    def bit_step(carry, bit, count):
        tu, cnt_t = carry
        cand_u = tu | lax.shift_left(jnp.int32(1), bit)
        cnt = count(_float_of_rank(cand_u))
        ok = cnt >= k_sel
        return jnp.where(ok, cand_u, tu), jnp.where(ok, cnt, cnt_t)

    def bis_body(s, carry):
        return bit_step(carry, 31 - s, lambda t: lax.switch(
            nch - 1, [functools.partial(count_chunks, n, t) for n in range(1, n_max + 1)]))

    start = (jnp.zeros((1, qb), I32), jnp.full((1, qb), k_sel, I32))
    units = s_ref.shape[0]
    bits_per_unit = SELECT_BITS // units

    def select_with_scores(n):
        def run():
            def body(u, carry):
                s_ref[u, 0:n * kc, :] = jnp.dot(kcat_ref[0:n * kc, :], qcat_ref[0, u],
                                                preferred_element_type=F32)
                for k in range(bits_per_unit):
                    carry = bit_step(carry, 31 - (u * bits_per_unit + k),
                                     functools.partial(count_chunks, n))
                return carry
            return lax.fori_loop(0, units, body, start)
        return run

    tu, cnt_t = lax.switch(nch - 1, [select_with_scores(n) for n in range(1, n_max + 1)])

    def trim(tu, cnt_t):
        thr = jnp.where(tu == 0, -jnp.inf, _float_of_rank(tu))
        few = thr < _ABOVE_NEG
        thr = jnp.where(few, _ABOVE_NEG, thr)
        cnt0 = jnp.where(few, k_sel, cnt_t)

        def trim_cond(state):
            _, cnt, n = state
            return (jnp.max(cnt) > k_sel) & (n < TRIM_PASSES)

        def trim_body(state):
            low, cnt, n = state

            def min_body(c, mn8):
                x = isc_ref[chunk_rows(c), :]
                sel = jnp.where(x >= thr, jnp.where(x > low, x, jnp.inf), jnp.inf)
                return jnp.minimum(mn8, fold(jnp.min, sel))

            mn = jnp.min(lax.fori_loop(0, nch, min_body, jnp.full((8, qb), jnp.inf, F32)),
                         axis=0, keepdims=True)

            def eq_body(c, cnt8):
                return cnt8 + fold(jnp.sum, jnp.where(isc_ref[chunk_rows(c), :] == mn, 1, 0).astype(I32))

            n_eq = jnp.sum(lax.fori_loop(0, nch, eq_body, jnp.zeros((8, qb), I32)),
                           axis=0, keepdims=True)
            can = (cnt > k_sel) & (cnt - n_eq >= k_sel)
            stalled = jnp.max(jnp.where(can, 1, 0)) == 0
            return (jnp.where(can, mn, low), jnp.where(can, cnt - n_eq, cnt),
                    jnp.where(stalled, TRIM_PASSES + 1, n + 1))

        low, cnt, n = lax.while_loop(trim_cond, trim_body,
                                     (jnp.full((1, qb), -jnp.inf, F32), cnt0, jnp.int32(0)))
        return thr, low, (n == TRIM_PASSES) & (jnp.max(cnt) > k_sel)

    thr, low, unresolved = trim(tu, cnt_t)
    thr, low = lax.cond(unresolved,
                        lambda: trim(*lax.fori_loop(SELECT_BITS, 32, bis_body, (tu, cnt_t)))[:2],
                        lambda: (thr, low))

    m_ref[...] = jnp.full(m_ref.shape, NEG, F32)
    l_ref[...] = jnp.zeros(l_ref.shape, F32)
    acc_ref[...] = jnp.zeros(acc_ref.shape, F32)

    def attend(c):
        rows = chunk_rows(c)
        x = isc_ref[rows, :]
        bias = jnp.where(x >= thr, jnp.where(x > low, 0.0, NEG), NEG).astype(F32)
        s = (jnp.concatenate([s_ref[u, rows, :] for u in range(units)], axis=1)
             + jnp.concatenate([bias] * N_HEADS_A, axis=1))
        m_old = m_ref[...]
        m_new = jnp.maximum(m_old, jnp.max(s, axis=0, keepdims=True))
        alpha = jnp.exp2(m_old - m_new)
        p = jnp.exp2(s - m_new)
        l_ref[...] = alpha * l_ref[...] + jnp.sum(p, axis=0, keepdims=True)
        acc_ref[...] = alpha * acc_ref[...] + jnp.dot(ckvt_ref[c], p.astype(BF16),
                                                      preferred_element_type=F32)
        m_ref[...] = m_new

    chunk_pairs(attend)

    inv_l = 1.0 / l_ref[...]
    outs = []
    for h in range(N_HEADS_A):
        cols = slice(h * qb, (h + 1) * qb)
        o_t = acc_ref[:, cols] * inv_l[:, cols]
        outs.append(jnp.dot(o_t.T.astype(BF16), wuv_ref[h], preferred_element_type=F32))
    o = jnp.concatenate(outs, axis=1)
    g = gate_ref[...].astype(F32)
    o_ref[...] = (o * _silu(g)).astype(BF16)


def _dsa_call(kidx, kcat, ckvt, qcat, qidx, w_t, gates, wuv, batch, seq):
    rows = kidx.shape[0]
    nb = seq // QUERY_BLOCK
    nkc = seq // KEY_CHUNK
    k_sel = min(TOPK_MAX, seq // 4)
    kern = functools.partial(_dsa_kernel, k_sel=k_sel)
    return pl.pallas_call(
        kern,
        grid=(batch, nb),
        in_specs=[pl.BlockSpec((seq, KIDX_PAD), lambda b, i: (b, 0)),
                  pl.BlockSpec((seq, KCAT), lambda b, i: (b, 0)),
                  pl.BlockSpec((nkc, KV_RANK, KEY_CHUNK), lambda b, i: (b, 0, 0)),
                  pl.BlockSpec((1, SCORE_UNITS, KCAT, UNIT_COLS), lambda b, i: (b * nb + i, 0, 0, 0)),
                  pl.BlockSpec((1, KIDX_PAD, IDX_HEADS * QUERY_BLOCK), lambda b, i: (b * nb + i, 0, 0)),
                  pl.BlockSpec((1, IDX_HEADS, QUERY_BLOCK), lambda b, i: (b * nb + i, 0, 0)),
                  pl.BlockSpec((QUERY_BLOCK, WIDTH_A), lambda b, i: (b * nb + i, 0)),
                  pl.BlockSpec((N_HEADS_A, KV_RANK, HEAD_DIM), lambda b, i: (0, 0, 0))],
        out_specs=pl.BlockSpec((QUERY_BLOCK, WIDTH_A), lambda b, i: (b * nb + i, 0)),
        out_shape=jax.ShapeDtypeStruct((rows, WIDTH_A), BF16),
        scratch_shapes=[pltpu.VMEM((seq, QUERY_BLOCK), F32),
                        pltpu.VMEM((1, N_HEADS_A * QUERY_BLOCK), F32),
                        pltpu.VMEM((1, N_HEADS_A * QUERY_BLOCK), F32),
                        pltpu.VMEM((KV_RANK, N_HEADS_A * QUERY_BLOCK), F32),
                        pltpu.VMEM((SCORE_UNITS, seq, UNIT_COLS), F32)],
        compiler_params=_cparams(("parallel", "arbitrary")),
        name="dsa_attn",
    )(kidx, kcat, ckvt, qcat, qidx, w_t, gates, wuv)


def _dil_kernel(q_ref, k_ref, v_ref, gate_ref, c_ref, s_ref, o_ref,
                nat_ref, a4_ref, qn, kn, vn, q4, k4, v4, q16, k16, v16,
                oc0, ms0, ls0, oc1, ms1, ls1, oc2, ms2, ls2, *, qscale, unroll):
    seq = q_ref.shape[0]
    blk = N_BACK
    rc = ROW_CHUNK
    m4 = seq // 4
    m16 = seq // 16

    def build(src_ref, dn, d4, d16, rope_scale):
        def nat_body(c, carry):
            rows = pl.ds(pl.multiple_of(c * rc, rc), rc)
            x = src_ref[rows, :]
            if rope_scale is not None:
                x = x * c_ref[rows, :] + pltpu.roll(x, LANE // 2, 1) * s_ref[rows, :]
                if rope_scale != 1.0:
                    x = x * rope_scale
                nat_ref[rows, :] = x
            dn[rows, :] = x.astype(BF16)
            return carry

        lax.fori_loop(0, seq // rc, nat_body, 0)
        nat = src_ref if rope_scale is None else nat_ref

        def a4_body(c, carry):
            r4 = c // (m4 // rc)
            j0 = (c % (m4 // rc)) * rc
            x = nat[pl.ds(r4 + 4 * j0, rc, stride=4), :]
            rows = pl.ds(pl.multiple_of(c * rc, rc), rc)
            a4_ref[rows, :] = x
            d4[rows, :] = x.astype(BF16)
            return carry

        lax.fori_loop(0, seq // rc, a4_body, 0)

        def a16_body(s, carry):
            for r4 in range(4):
                x = a4_ref[pl.ds(r4 * m4 + s, m16, stride=4), :]
                d16[pl.ds(pl.multiple_of((4 * s + r4) * m16, m16), m16), :] = x.astype(BF16)
            return carry

        lax.fori_loop(0, 4, a16_body, 0)

    build(q_ref, qn, q4, q16, qscale)
    build(k_ref, kn, k4, k16, 1.0)
    build(v_ref, vn, v4, v16, None)

    qi = lax.broadcasted_iota(I32, (blk, 2 * blk), 0) + blk
    kj = lax.broadcasted_iota(I32, (blk, 2 * blk), 1)
    rel = qi - kj
    bias_band = jnp.where((rel >= 0) & (rel <= N_BACK), 0.0, NEG).astype(F32)
    no_prev = jnp.where(kj < blk, NEG, 0.0).astype(F32)
    qi1 = lax.broadcasted_iota(I32, (blk, blk), 0)
    kj1 = lax.broadcasted_iota(I32, (blk, blk), 1)
    bias_first = jnp.where(kj1 <= qi1, 0.0, NEG).astype(F32)
    nt = (((1,), (1,)), ((), ()))

    def run_config(qs, ks, vs, nmb, store):
        def rows_of(u, first):
            cur = pl.ds(pl.multiple_of(u * blk, blk), blk)
            if first is True:
                return (cur,)
            return (pl.ds(pl.multiple_of(jnp.maximum(u * blk - blk, 0), blk), blk), cur)

        def gather(ref, rows):
            parts = [ref[r, :] for r in rows]
            return parts[0] if len(parts) == 1 else jnp.concatenate(parts, axis=0)

        def loop_body(g, carry):
            units = []
            for t in range(unroll):
                if nmb <= unroll:
                    first = (t % nmb == 0)
                else:
                    first = jnp.where(g == 0, 1.0, 0.0).astype(F32) if t == 0 else False
                units.append((g * unroll + t, first))
            scores = []
            for u, first in units:
                rows = rows_of(u, first)
                if first is True:
                    bias = bias_first
                elif first is False:
                    bias = bias_band
                else:
                    bias = bias_band + no_prev * first
                scores.append(lax.dot_general(qs[rows[-1], :], gather(ks, rows), nt,
                                              preferred_element_type=F32) + bias)
            probs = []
            for s in scores:
                m = jnp.max(s, axis=-1, keepdims=True)
                p = jnp.exp2(s - m)
                l = jnp.sum(p, axis=-1, keepdims=True)
                probs.append((p.astype(BF16), jnp.broadcast_to(m, (blk, LANE)),
                              jnp.broadcast_to(l, (blk, LANE))))
            for (u, first), (p, m_b, l_b) in zip(units, probs):
                store(u, jnp.dot(p, gather(vs, rows_of(u, first)), preferred_element_type=F32), m_b, l_b)
            return carry

        lax.fori_loop(0, seq // blk // unroll, loop_body, 0)

    def store_contig(oc, ms, ls):
        def store(u, o, m_b, l_b):
            rows = pl.ds(pl.multiple_of(u * blk, blk), blk)
            oc[rows, :] = o
            ms[rows, :] = m_b
            ls[rows, :] = l_b
        return store

    def store_a16_to_a4(u, o, m_b, l_b):
        r16 = u // (m16 // blk)
        mb = u % (m16 // blk)
        rows = pl.ds((r16 % 4) * m4 + 4 * blk * mb + r16 // 4, blk, stride=4)
        oc2[rows, :] = o
        ms2[rows, :] = m_b
        ls2[rows, :] = l_b

    run_config(qn, kn, vn, seq // blk, store_contig(oc0, ms0, ls0))
    run_config(q4, k4, v4, m4 // blk, store_contig(oc1, ms1, ls1))
    run_config(q16, k16, v16, m16 // blk, store_a16_to_a4)

    def merge_body(c, carry):
        r4 = c // (m4 // rc)
        j0 = (c % (m4 // rc)) * rc
        nat_rows = pl.ds(r4 + 4 * j0, rc, stride=4)
        rows = pl.ds(pl.multiple_of(c * rc, rc), rc)
        m0 = ms0[nat_rows, :]
        m1 = ms1[rows, :]
        m2 = ms2[rows, :]
        mx = jnp.maximum(jnp.maximum(m0, m1), m2)
        e0 = jnp.exp2(m0 - mx)
        e1 = jnp.exp2(m1 - mx)
        e2 = jnp.exp2(m2 - mx)
        den = e0 * ls0[nat_rows, :] + e1 * ls1[rows, :] + e2 * ls2[rows, :]
        nat_ref[nat_rows, :] = (e0 * oc0[nat_rows, :] + e1 * oc1[rows, :] + e2 * oc2[rows, :]) / den
        return carry

    lax.fori_loop(0, seq // rc, merge_body, 0)

    def gate_body(c, carry):
        rows = pl.ds(pl.multiple_of(c * rc, rc), rc)
        g = gate_ref[rows, :].astype(F32)
        o_ref[rows, :] = (nat_ref[rows, :] * _silu(g)).astype(BF16)
        return carry

    lax.fori_loop(0, seq // rc, gate_body, 0)


def _dil_call(qkv, gates, ctab, stab, batch, seq):
    rows = qkv.shape[0]
    nh = N_HEADS_B
    qscale = HEAD_DIM ** -0.5 * LOG2E
    kern = functools.partial(_dil_kernel, qscale=qscale, unroll=DIL_UNROLL)
    blk = (seq, HEAD_DIM)
    return pl.pallas_call(
        kern,
        grid=(batch, nh),
        in_specs=[pl.BlockSpec(blk, lambda b, h: (b, h)),
                  pl.BlockSpec(blk, lambda b, h: (b, nh + h)),
                  pl.BlockSpec(blk, lambda b, h: (b, 2 * nh + h)),
                  pl.BlockSpec(blk, lambda b, h: (b, nh + h)),
                  pl.BlockSpec(blk, lambda b, h: (0, 0), pipeline_mode=pl.Buffered(1)),
                  pl.BlockSpec(blk, lambda b, h: (0, 0), pipeline_mode=pl.Buffered(1))],
        out_specs=pl.BlockSpec(blk, lambda b, h: (b, h)),
        out_shape=jax.ShapeDtypeStruct((rows, WIDTH_B), BF16),
        scratch_shapes=([pltpu.VMEM(blk, F32)] * 2 + [pltpu.VMEM(blk, BF16)] * 9
                        + [pltpu.VMEM(blk, F32)] * 9),
        compiler_params=_cparams(("parallel", "arbitrary")),
        name="dilated",
    )(qkv, qkv, qkv, gates, ctab, stab)


def _out_kernel(oa_ref, ob_ref, w_ref, x_ref, mod_ref, g_ref, o_ref):
    y = jnp.dot(oa_ref[...], w_ref[0:WIDTH_A, :], preferred_element_type=F32)
    y = y + jnp.dot(ob_ref[...], w_ref[WIDTH_A:WIDTH_A + WIDTH_B, :], preferred_element_type=F32)
    yn = y * lax.rsqrt(jnp.mean(y * y, axis=-1, keepdims=True) + EPS) * g_ref[...]
    o_ref[...] = x_ref[...] + mod_ref[0, 2:3, :] * yn


def _out_call(o_a, o_b, w_out, x2, mod3, g_post, seq):
    rows, d = x2.shape
    tm = OUT_TM
    tpb = seq // tm
    return pl.pallas_call(
        _out_kernel,
        grid=(rows // tm,),
        in_specs=[pl.BlockSpec((tm, WIDTH_A), lambda i: (i, 0)),
                  pl.BlockSpec((tm, WIDTH_B), lambda i: (i, 0)),
                  pl.BlockSpec((WIDTH_A + WIDTH_B, d), lambda i: (0, 0)),
                  pl.BlockSpec((tm, d), lambda i: (i, 0)),
                  pl.BlockSpec((1, 3, d), lambda i: (i // tpb, 0, 0)),
                  pl.BlockSpec((1, d), lambda i: (0, 0))],
        out_specs=pl.BlockSpec((tm, d), lambda i: (i, 0)),
        out_shape=jax.ShapeDtypeStruct((rows, d), F32),
        compiler_params=_cparams(("parallel",)),
        name="out_proj",
    )(o_a, o_b, w_out, x2, mod3, g_post)


def _rope_tables(seq):
    pos = jnp.arange(seq, dtype=F32)

    def cs(n_rot):
        inv = ROPE_THETA ** (-jnp.arange(0, n_rot, 2, dtype=F32) / n_rot)
        ang = pos[:, None] * inv[None, :]
        return jnp.cos(ang), jnp.sin(ang)

    c32, s32 = cs(ROPE_DIM)
    c16, s16 = cs(IDX_ROPE)
    ck = jnp.concatenate([c32, c32, jnp.ones((seq, LANE - ROPE_DIM), F32)], axis=1)
    sk = jnp.concatenate([-s32, s32, jnp.zeros((seq, LANE - ROPE_DIM), F32)], axis=1)
    ci = jnp.concatenate([c16, c16, jnp.ones((seq, LANE - IDX_ROPE), F32)], axis=1)
    si = jnp.concatenate([-s16, s16, jnp.zeros((seq, LANE - IDX_ROPE), F32)], axis=1)
    gap = jnp.ones((seq, LANE // 2 - ROPE_DIM // 2), F32)
    cd = jnp.concatenate([c32, gap, c32, gap], axis=1)
    sd = jnp.concatenate([-s32, 0.0 * gap, s32, 0.0 * gap], axis=1)
    return (c32.T, s32.T, c16.T, s16.T, ck, sk, ci, si, cd, sd)


def _pair_heads(wt):
    half = ROPE_DIM // 2
    mid = ROPE_DIM + (LANE // 2 - half)
    w3 = wt.reshape(-1, HEAD_DIM, wt.shape[-1])
    w3 = jnp.concatenate([w3[:, :half], w3[:, ROPE_DIM:mid], w3[:, half:ROPE_DIM], w3[:, mid:]], axis=1)
    return w3.reshape(wt.shape)


def _layer(x, c, w_ada, b_ada, g_pre, g_post, w_in, g_q, g_kv, w_uq, w_uq_idx, w_uk, w_uv, w_out):
    batch, seq, d = x.shape
    assert seq % PROJ_TM == 0 or seq < PROJ_TM
    assert seq % KEY_CHUNK == 0 and seq // 16 >= N_BACK
    assert all(w // dl == N_BACK for w, dl in DILATED_CONFIGS)
    rows = batch * seq
    x2 = x.reshape(rows, d)

    o_ki = Q_RANK + KV_RANK + ROPE_DIM
    o_ga = o_ki + IDX_DIM + IDX_HEADS
    o_q = o_ga + WIDTH_A
    o_k, o_v, o_gb = o_q + WIDTH_B, o_q + 2 * WIDTH_B, o_q + 3 * WIDTH_B
    assert o_gb + WIDTH_B == w_in.shape[1]
    wt = w_in.T.astype(BF16)
    zeros = lambda n: jnp.zeros((n, d), BF16)
    w_all = jnp.concatenate([
        wt[:o_ki], zeros(SLAB_KIDX - o_ki), wt[o_ki:o_ga], zeros(SMALL_W - SLAB_KIDX - (o_ga - o_ki)),
        _pair_heads(wt[o_q:o_k]), _pair_heads(wt[o_k:o_v]), wt[o_v:o_gb],
        wt[o_ga:o_q], wt[o_gb:o_gb + WIDTH_B]], axis=0)
    assert w_all.shape[0] == IN_PAD

    c_pad = jnp.zeros((PACK16, d), F32).at[:batch].set(c)
    mod = _mod_call(c_pad, w_ada, b_ada.reshape(1, -1))[:batch]
    mod3 = mod.reshape(batch, 3, d)

    small, qkv, gates = _proj_call(x2, mod3, g_pre.reshape(1, d), w_all, seq)

    tabs = _rope_tables(seq)
    wuk_t = jnp.transpose(jnp.pad(w_uk, ((0, 0), (ROPE_DIM, 0), (0, 0))), (0, 2, 1)).astype(BF16)
    qcat, qidx, w_t, kcat, kidx, ckvt = _prep_call(
        small, g_q.reshape(1, -1), g_kv.reshape(1, -1), w_uq.T.astype(BF16),
        w_uq_idx.T.astype(BF16), wuk_t, tabs[:8], seq)

    o_a = _dsa_call(kidx, kcat, ckvt, qcat, qidx, w_t, gates, w_uv.astype(BF16), batch, seq)
    o_b = _dil_call(qkv, gates, tabs[8], tabs[9], batch, seq)
    out = _out_call(o_a, o_b, w_out.astype(BF16), x2, mod3, g_post.reshape(1, d), seq)
    return out.reshape(batch, seq, d)


def kernel(x, c, w_ada, b_ada, g_pre, g_post, w_in, g_q, g_kv, w_uq, w_uq_idx, w_uk, w_uv, w_out):
    for layer in range(w_ada.shape[0]):
        x = _layer(x, c, w_ada[layer], b_ada[layer], g_pre[layer], g_post[layer], w_in[layer],
                   g_q[layer], g_kv[layer], w_uq[layer], w_uq_idx[layer], w_uk[layer],
                   w_uv[layer], w_out[layer])
    return x
```

```python
import functools

import numpy as np
import jax
import jax.numpy as jnp
from jax import lax
from jax.experimental import pallas as pl
from jax.experimental.pallas import tpu as pltpu

F32 = jnp.float32
BF16 = jnp.bfloat16
I32 = jnp.int32

HEAD_DIM = 128
ROPE_DIM = HEAD_DIM // 4
ROPE_THETA = 500000.0
EPS = 1e-6
NEG = -1e30
N_HEADS_A = 8
WIDTH_A = N_HEADS_A * HEAD_DIM
Q_RANK = 512
KV_RANK = 256
IDX_HEADS = 16
IDX_DIM = 64
IDX_ROPE = IDX_DIM // 4
TOPK_MAX = 256
QUERY_BLOCK = 128
N_HEADS_B = 8
WIDTH_B = N_HEADS_B * HEAD_DIM
DILATED_CONFIGS = ((128, 1), (512, 4), (2048, 16))
N_BACK = 128
SMALL_W = 1024
IN_PAD = SMALL_W + 5 * 1024
KCAT = 288
KIDX_PAD = 64
INT_MIN = -(2 ** 31)
LOG2E = 1.4426950408889634

LANE = 128
PACK16 = 16
SLAB_KROPE = Q_RANK + KV_RANK
SLAB_KIDX = SLAB_KROPE + LANE
assert SLAB_KIDX + LANE == SMALL_W
DIL_UNROLL = 16
VMEM_LIMIT = 56 * 1024 * 1024
PROJ_TM = 1024
PROJ_TN = 1024
PREP_TQ = 512
KEY_CHUNK = 512
OUT_TM = 512
ROW_CHUNK = 512
NORM_ROWS = 128


def _cparams(sem):
    return pltpu.CompilerParams(dimension_semantics=sem, vmem_limit_bytes=VMEM_LIMIT)


def _silu(g):
    return g * jax.nn.sigmoid(g)


def _split_bf16(x):
    hi = x.astype(BF16)
    return hi, (x - hi.astype(F32)).astype(BF16)


def _mod_kernel(c_ref, w_ref, b_ref, o_ref):
    a_hi, a_lo = _split_bf16(_silu(c_ref[...]))
    w_hi, w_lo = _split_bf16(w_ref[...])
    dot = functools.partial(jnp.dot, preferred_element_type=F32)
    o_ref[...] = dot(a_hi, w_hi) + (dot(a_hi, w_lo) + dot(a_lo, w_hi)) + b_ref[...]


def _mod_call(c_pad, w_ada, b_ada):
    rows, d = c_pad.shape
    n = w_ada.shape[1]
    tn = 1024
    return pl.pallas_call(
        _mod_kernel,
        grid=(n // tn,),
        in_specs=[pl.BlockSpec((rows, d), lambda j: (0, 0)),
                  pl.BlockSpec((d, tn), lambda j: (0, j)),
                  pl.BlockSpec((1, tn), lambda j: (0, j))],
        out_specs=pl.BlockSpec((rows, tn), lambda j: (0, j)),
        out_shape=jax.ShapeDtypeStruct((rows, n), F32),
        compiler_params=_cparams(("arbitrary",)),
        name="mod",
    )(c_pad, w_ada, b_ada)


def _proj_kernel(x_ref, mod_ref, g_ref, w_ref, small_ref, qkv_ref, gates_ref, h_ref, *,
                 n_small, n_qkv):
    j = pl.program_id(1)
    tm = x_ref.shape[0]
    half = tm // 2
    assert n_small == 1 and half % NORM_ROWS == 0

    def project(rows=slice(None)):
        return lax.dot_general(h_ref[rows, :], w_ref[...], (((1,), (1,)), ((), ())),
                               preferred_element_type=F32)

    @pl.when(j == 0)
    def _():
        shift = mod_ref[0, 0:1, :]
        scale1 = 1.0 + mod_ref[0, 1:2, :]
        g = g_ref[...]

        def norm_rows(rows):
            x = x_ref[rows, :]
            ms = jnp.mean(x * x, axis=-1, keepdims=True)
            y = x * lax.rsqrt(ms + EPS) * g
            h_ref[rows, :] = (y * scale1 + shift).astype(BF16)

        def body(r, carry):
            norm_rows(pl.ds(pl.multiple_of(r * NORM_ROWS, NORM_ROWS), NORM_ROWS))
            return carry

        lax.fori_loop(0, half // NORM_ROWS, body, 0)
        small_ref[0:half, :] = project(slice(0, half))
        for r in range(half // NORM_ROWS):
            norm_rows(slice(half + r * NORM_ROWS, half + (r + 1) * NORM_ROWS))
        small_ref[half:tm, :] = project(slice(half, tm))

    @pl.when((j >= n_small) & (j < n_small + n_qkv))
    def _():
        qkv_ref[...] = project()

    @pl.when(j >= n_small + n_qkv)
    def _():
        gates_ref[...] = project().astype(BF16)


def _proj_call(x2, mod3, g_pre, w_all, seq):
    rows, d = x2.shape
    tm = min(PROJ_TM, seq)
    tn = PROJ_TN
    n_small = SMALL_W // tn
    n_qkv = 3 * WIDTH_B // tn
    n_gate = (WIDTH_A + WIDTH_B) // tn
    tiles_per_batch = seq // tm
    kern = functools.partial(_proj_kernel, n_small=n_small, n_qkv=n_qkv)
    return pl.pallas_call(
        kern,
        grid=(rows // tm, n_small + n_qkv + n_gate),
        in_specs=[pl.BlockSpec((tm, d), lambda i, j: (i, 0)),
                  pl.BlockSpec((1, 3, d), lambda i, j: (i // tiles_per_batch, 0, 0)),
                  pl.BlockSpec((1, d), lambda i, j: (0, 0)),
                  pl.BlockSpec((tn, d), lambda i, j: (j, 0))],
        out_specs=[pl.BlockSpec((tm, tn), lambda i, j: (i, jnp.minimum(j, n_small - 1))),
                   pl.BlockSpec((tm, tn), lambda i, j: (i, jnp.clip(j - n_small, 0, n_qkv - 1))),
                   pl.BlockSpec((tm, tn),
                                lambda i, j: (i, jnp.clip(j - n_small - n_qkv, 0, n_gate - 1)))],
        out_shape=[jax.ShapeDtypeStruct((rows, SMALL_W), F32),
                   jax.ShapeDtypeStruct((rows, 3 * WIDTH_B), F32),
                   jax.ShapeDtypeStruct((rows, WIDTH_A + WIDTH_B), BF16)],
        scratch_shapes=[pltpu.VMEM((tm, d), BF16)],
        compiler_params=_cparams(("parallel", "arbitrary")),
        name="in_proj",
    )(x2, mod3, g_pre, w_all)


def _prep_kernel(small_ref, gq_ref, gkv_ref, wuq_ref, wuqi_ref, wuk_ref,
                 cosq_ref, sinq_ref, cosi_ref, sini_ref, ck_ref, sk_ref, ci_ref, si_ref,
                 qcat_ref, qidx_ref, wt_ref, kcat_ref, kidx_ref, ckvt_ref, *, qscale):
    tq = small_ref.shape[0]
    nblk = tq // QUERY_BLOCK
    cq = small_ref[:, 0:Q_RANK]
    cqn = (cq * lax.rsqrt(jnp.mean(cq * cq, axis=-1, keepdims=True) + EPS)
           * gq_ref[...]).astype(BF16)
    nt = (((1,), (1,)), ((), ()))

    q_t = lax.dot_general(wuq_ref[...], cqn, nt, preferred_element_type=F32)
    cosq = cosq_ref[...]
    sinq = sinq_ref[...]
    half = ROPE_DIM // 2
    zpad = [jnp.zeros((KCAT - KV_RANK - ROPE_DIM, tq), F32)] if KCAT > KV_RANK + ROPE_DIM else []
    for h in range(N_HEADS_A):
        base = h * HEAD_DIM
        x1 = q_t[base:base + half]
        x2 = q_t[base + half:base + ROPE_DIM]
        q_rope = jnp.concatenate([x1 * cosq - x2 * sinq, x2 * cosq + x1 * sinq] + zpad, axis=0)
        q_lat = jnp.dot(wuk_ref[h], q_t[base:base + HEAD_DIM].astype(BF16),
                        preferred_element_type=F32)
        q_lat = (q_lat * qscale).astype(BF16)
        q_rope = (q_rope * qscale).astype(BF16)
        for blk in range(nblk):
            cols = slice(blk * QUERY_BLOCK, (blk + 1) * QUERY_BLOCK)
            unit, hu = divmod(h, N_HEADS_A // SCORE_UNITS)
            lanes = slice(hu * QUERY_BLOCK, (hu + 1) * QUERY_BLOCK)
            qcat_ref[blk, unit, 0:KV_RANK, lanes] = q_lat[:, cols]
            qcat_ref[blk, unit, KV_RANK:KCAT, lanes] = q_rope[:, cols]

    qi_t = lax.dot_general(wuqi_ref[...], cqn, nt, preferred_element_type=F32)
    cosi = cosi_ref[...]
    sini = sini_ref[...]
    ihalf = IDX_ROPE // 2
    ipad = [jnp.zeros((KIDX_PAD - IDX_DIM, tq), F32)] if KIDX_PAD > IDX_DIM else []
    for h in range(IDX_HEADS):
        base = h * IDX_DIM
        x1 = qi_t[base:base + ihalf]
        x2 = qi_t[base + ihalf:base + IDX_ROPE]
        qi = jnp.concatenate([x1 * cosi - x2 * sini, x2 * cosi + x1 * sini,
                              qi_t[base + IDX_ROPE:base + IDX_DIM]] + ipad, axis=0).astype(BF16)
        for blk in range(nblk):
            cols = slice(blk * QUERY_BLOCK, (blk + 1) * QUERY_BLOCK)
            qidx_ref[blk, :, h * QUERY_BLOCK:(h + 1) * QUERY_BLOCK] = qi[:, cols]

    lane = lax.broadcasted_iota(I32, (tq, LANE), 1)

    slab_b = small_ref[:, SLAB_KIDX:SLAB_KIDX + LANE]
    swap_b = jnp.where(lane < ihalf, pltpu.roll(slab_b, LANE - ihalf, 1), pltpu.roll(slab_b, ihalf, 1))
    kidx_ref[...] = (slab_b * ci_ref[...] + swap_b * si_ref[...])[:, :KIDX_PAD].astype(BF16)
    w_t = slab_b.T[IDX_DIM:IDX_DIM + IDX_HEADS] * (IDX_HEADS ** -0.5 * IDX_DIM ** -0.5)
    for blk in range(nblk):
        wt_ref[blk] = w_t[:, blk * QUERY_BLOCK:(blk + 1) * QUERY_BLOCK]

    ckv = small_ref[:, Q_RANK:Q_RANK + KV_RANK]
    cn = ckv * lax.rsqrt(jnp.mean(ckv * ckv, axis=-1, keepdims=True) + EPS) * gkv_ref[...]
    kcat_ref[:, 0:KV_RANK] = cn.astype(BF16)
    ckvt_ref[0] = cn.T.astype(BF16)
    slab_a = small_ref[:, SLAB_KROPE:SLAB_KROPE + LANE]
    swap_a = jnp.where(lane < half, pltpu.roll(slab_a, LANE - half, 1), pltpu.roll(slab_a, half, 1))
    kcat_ref[:, KV_RANK:KCAT] = (slab_a * ck_ref[...] + swap_a * sk_ref[...])[:, :KCAT - KV_RANK].astype(BF16)


def _prep_call(small, g_q, g_kv, wuq_t, wuqi_t, wuk_t, tabs, seq):
    rows = small.shape[0]
    tq = PREP_TQ
    nblk = tq // QUERY_BLOCK
    tpb = seq // tq
    nqb = rows // QUERY_BLOCK
    qscale = HEAD_DIM ** -0.5 * LOG2E
    kern = functools.partial(_prep_kernel, qscale=qscale)
    const = lambda t: (0, 0)
    tcol = lambda t: (0, t % tpb)
    trow = lambda t: (t % tpb, 0)
    return pl.pallas_call(
        kern,
        grid=(rows // tq,),
        in_specs=[pl.BlockSpec((tq, SMALL_W), lambda t: (t, 0)),
                  pl.BlockSpec((1, Q_RANK), const),
                  pl.BlockSpec((1, KV_RANK), const),
                  pl.BlockSpec((WIDTH_A, Q_RANK), const),
                  pl.BlockSpec((IDX_HEADS * IDX_DIM, Q_RANK), const),
                  pl.BlockSpec((N_HEADS_A, KV_RANK, HEAD_DIM), lambda t: (0, 0, 0)),
                  pl.BlockSpec((ROPE_DIM // 2, tq), tcol),
                  pl.BlockSpec((ROPE_DIM // 2, tq), tcol),
                  pl.BlockSpec((IDX_ROPE // 2, tq), tcol),
                  pl.BlockSpec((IDX_ROPE // 2, tq), tcol),
                  pl.BlockSpec((tq, LANE), trow),
                  pl.BlockSpec((tq, LANE), trow),
                  pl.BlockSpec((tq, LANE), trow),
                  pl.BlockSpec((tq, LANE), trow)],
        out_specs=[pl.BlockSpec((nblk, SCORE_UNITS, KCAT, UNIT_COLS), lambda t: (t, 0, 0, 0)),
                   pl.BlockSpec((nblk, KIDX_PAD, IDX_HEADS * QUERY_BLOCK), lambda t: (t, 0, 0)),
                   pl.BlockSpec((nblk, IDX_HEADS, QUERY_BLOCK), lambda t: (t, 0, 0)),
                   pl.BlockSpec((tq, KCAT), lambda t: (t, 0)),
                   pl.BlockSpec((tq, KIDX_PAD), lambda t: (t, 0)),
                   pl.BlockSpec((1, KV_RANK, tq), lambda t: (t, 0, 0))],
        out_shape=[jax.ShapeDtypeStruct((nqb, SCORE_UNITS, KCAT, UNIT_COLS), BF16),
                   jax.ShapeDtypeStruct((nqb, KIDX_PAD, IDX_HEADS * QUERY_BLOCK), BF16),
                   jax.ShapeDtypeStruct((nqb, IDX_HEADS, QUERY_BLOCK), F32),
                   jax.ShapeDtypeStruct((rows, KCAT), BF16),
                   jax.ShapeDtypeStruct((rows, KIDX_PAD), BF16),
                   jax.ShapeDtypeStruct((rows // tq, KV_RANK, tq), BF16)],
        compiler_params=_cparams(("parallel",)),
        name="dsa_prep",
    )(small, g_q, g_kv, wuq_t, wuqi_t, wuk_t, *tabs)


def _float_of_rank(u):
    key = u ^ INT_MIN
    return lax.bitcast_convert_type(key ^ (lax.shift_right_arithmetic(key, 31) & 0x7FFFFFFF), F32)


_ABOVE_NEG = float(np.nextafter(np.float32(NEG), np.float32(0.0)))
TRIM_PASSES = 8
SELECT_BITS = 24
SCORE_UNITS = 2
UNIT_COLS = N_HEADS_A * QUERY_BLOCK // SCORE_UNITS


def _dsa_kernel(kidx_ref, kcat_ref, ckvt_ref, qcat_ref, qidx_ref, wt_ref, gate_ref, wuv_ref,
                o_ref, isc_ref, acc_ref, s_ref, *, k_sel):
    i = pl.program_id(1)
    kc = KEY_CHUNK
    qb = QUERY_BLOCK
    per = kc // qb
    nch = (i + per) // per
    qpos = i * qb + lax.broadcasted_iota(I32, (kc, qb), 1)
    krow = lax.broadcasted_iota(I32, (kc, qb), 0)

    def chunk_rows(c):
        return pl.ds(c * kc if isinstance(c, int) else pl.multiple_of(c * kc, kc), kc)

    def fold(op, x):
        return op(x.reshape(kc // 8, 8, qb), axis=0)

    def chunk_pairs(step):
        def body(j, carry):
            step(2 * j)
            step(2 * j + 1)
            return carry

        lax.fori_loop(0, nch // 2, body, 0)

        @pl.when(nch % 2 == 1)
        def _():
            step(nch - 1)

    def index_chunk(c):
        kblk = kidx_ref[chunk_rows(c), :]
        acc = jnp.zeros((kc, qb), F32)
        group = 4
        for g in range(IDX_HEADS // group):
            lg = jnp.dot(kblk, qidx_ref[0, :, g * group * qb:(g + 1) * group * qb],
                         preferred_element_type=F32)
            for hh in range(group):
                h = g * group + hh
                acc = acc + jnp.maximum(lg[:, hh * qb:(hh + 1) * qb], 0.0) * wt_ref[0, h:h + 1, :]
        causal = (c * kc + krow) <= qpos
        isc_ref[chunk_rows(c), :] = jnp.where(causal, acc, NEG)

    chunk_pairs(index_chunk)

    n_max = isc_ref.shape[0] // kc

    def count_chunks(n, t):
        parts = [fold(jnp.sum, jnp.where(isc_ref[chunk_rows(c), :] >= t, 1, 0).astype(I32))
                 for c in range(n)]
        while len(parts) > 1:
            parts = [a + b for a, b in zip(parts[::2], parts[1::2])] + parts[len(parts) & ~1:]
        return jnp.sum(parts[0], axis=0, keepdims=True)

    def bit_step(carry, bit, count):
        tu, cnt_t = carry
        cand_u = tu | lax.shift_left(jnp.int32(1), bit)
        cnt = count(_float_of_rank(cand_u))
        ok = cnt >= k_sel
        return jnp.where(ok, cand_u, tu), jnp.where(ok, cnt, cnt_t)

    def bis_body(s, carry):
        return bit_step(carry, 31 - s, lambda t: lax.switch(
            nch - 1, [functools.partial(count_chunks, n, t) for n in range(1, n_max + 1)]))

    start = (jnp.zeros((1, qb), I32), jnp.full((1, qb), k_sel, I32))
    units = s_ref.shape[0]
    bits_per_unit = SELECT_BITS // units

    def select_with_scores(n):
        def run():
            def body(u, carry):
                s_ref[u, 0:n * kc, :] = jnp.dot(kcat_ref[0:n * kc, :], qcat_ref[0, u],
                                                preferred_element_type=F32)
                for k in range(bits_per_unit):
                    carry = bit_step(carry, 31 - (u * bits_per_unit + k),
                                     functools.partial(count_chunks, n))
                return carry
            return lax.fori_loop(0, units, body, start)
        return run

    tu, cnt_t = lax.switch(nch - 1, [select_with_scores(n) for n in range(1, n_max + 1)])

    def trim(tu, cnt_t):
        thr = jnp.where(tu == 0, -jnp.inf, _float_of_rank(tu))
        few = thr < _ABOVE_NEG
        thr = jnp.where(few, _ABOVE_NEG, thr)
        cnt0 = jnp.where(few, k_sel, cnt_t)

        def trim_cond(state):
            _, cnt, n = state
            return (jnp.max(cnt) > k_sel) & (n < TRIM_PASSES)

        def trim_body(state):
            low, cnt, n = state

            def min_body(c, mn8):
                x = isc_ref[chunk_rows(c), :]
                sel = jnp.where(x >= thr, jnp.where(x > low, x, jnp.inf), jnp.inf)
                return jnp.minimum(mn8, fold(jnp.min, sel))

            mn = jnp.min(lax.fori_loop(0, nch, min_body, jnp.full((8, qb), jnp.inf, F32)),
                         axis=0, keepdims=True)

            def eq_body(c, cnt8):
                return cnt8 + fold(jnp.sum, jnp.where(isc_ref[chunk_rows(c), :] == mn, 1, 0).astype(I32))

            n_eq = jnp.sum(lax.fori_loop(0, nch, eq_body, jnp.zeros((8, qb), I32)),
                           axis=0, keepdims=True)
            can = (cnt > k_sel) & (cnt - n_eq >= k_sel)
            stalled = jnp.max(jnp.where(can, 1, 0)) == 0
            return (jnp.where(can, mn, low), jnp.where(can, cnt - n_eq, cnt),
                    jnp.where(stalled, TRIM_PASSES + 1, n + 1))

        low, cnt, n = lax.while_loop(trim_cond, trim_body,
                                     (jnp.full((1, qb), -jnp.inf, F32), cnt0, jnp.int32(0)))
        return thr, low, (n == TRIM_PASSES) & (jnp.max(cnt) > k_sel)

    thr, low, unresolved = trim(tu, cnt_t)
    thr, low = lax.cond(unresolved,
                        lambda: trim(*lax.fori_loop(SELECT_BITS, 32, bis_body, (tu, cnt_t)))[:2],
                        lambda: (thr, low))

    def fold_chunks(step, init):
        def body(j, carry):
            return step(2 * j + 1, step(2 * j, carry))

        carry = lax.fori_loop(0, nch // 2, body, init)
        return lax.cond(nch % 2 == 1, lambda cr: step(nch - 1, cr), lambda cr: cr, carry)

    def load_scores(rows):
        return jnp.concatenate([s_ref[u, rows, :] for u in range(units)], axis=1)

    def mask_and_max(c, m):
        rows = chunk_rows(c)
        x = isc_ref[rows, :]
        bias = jnp.where(x >= thr, jnp.where(x > low, 0.0, NEG), NEG).astype(F32)
        s = load_scores(rows) + jnp.concatenate([bias] * N_HEADS_A, axis=1)
        for u in range(units):
            s_ref[u, rows, :] = s[:, u * UNIT_COLS:(u + 1) * UNIT_COLS]
        return jnp.maximum(m, jnp.max(s, axis=0, keepdims=True))

    m = fold_chunks(mask_and_max, jnp.full((1, N_HEADS_A * qb), NEG, F32))
    acc_ref[...] = jnp.zeros(acc_ref.shape, F32)

    def exp_sum_pv(c, l):
        p = jnp.exp2(load_scores(chunk_rows(c)) - m)
        acc_ref[...] += jnp.dot(ckvt_ref[c], p.astype(BF16), preferred_element_type=F32)
        return l + jnp.sum(p, axis=0, keepdims=True)

    l = fold_chunks(exp_sum_pv, jnp.zeros((1, N_HEADS_A * qb), F32))

    inv_l = 1.0 / l
    outs = []
    for h in range(N_HEADS_A):
        cols = slice(h * qb, (h + 1) * qb)
        o_t = acc_ref[:, cols] * inv_l[:, cols]
        outs.append(jnp.dot(o_t.T.astype(BF16), wuv_ref[h], preferred_element_type=F32))
    o = jnp.concatenate(outs, axis=1)
    g = gate_ref[...].astype(F32)
    o_ref[...] = (o * _silu(g)).astype(BF16)


def _dsa_call(kidx, kcat, ckvt, qcat, qidx, w_t, gates, wuv, batch, seq):
    rows = kidx.shape[0]
    nb = seq // QUERY_BLOCK
    nkc = seq // KEY_CHUNK
    k_sel = min(TOPK_MAX, seq // 4)
    kern = functools.partial(_dsa_kernel, k_sel=k_sel)
    return pl.pallas_call(
        kern,
        grid=(batch, nb),
        in_specs=[pl.BlockSpec((seq, KIDX_PAD), lambda b, i: (b, 0)),
                  pl.BlockSpec((seq, KCAT), lambda b, i: (b, 0)),
                  pl.BlockSpec((nkc, KV_RANK, KEY_CHUNK), lambda b, i: (b, 0, 0)),
                  pl.BlockSpec((1, SCORE_UNITS, KCAT, UNIT_COLS), lambda b, i: (b * nb + i, 0, 0, 0)),
                  pl.BlockSpec((1, KIDX_PAD, IDX_HEADS * QUERY_BLOCK), lambda b, i: (b * nb + i, 0, 0)),
                  pl.BlockSpec((1, IDX_HEADS, QUERY_BLOCK), lambda b, i: (b * nb + i, 0, 0)),
                  pl.BlockSpec((QUERY_BLOCK, WIDTH_A), lambda b, i: (b * nb + i, 0)),
                  pl.BlockSpec((N_HEADS_A, KV_RANK, HEAD_DIM), lambda b, i: (0, 0, 0))],
        out_specs=pl.BlockSpec((QUERY_BLOCK, WIDTH_A), lambda b, i: (b * nb + i, 0)),
        out_shape=jax.ShapeDtypeStruct((rows, WIDTH_A), BF16),
        scratch_shapes=[pltpu.VMEM((seq, QUERY_BLOCK), F32),
                        pltpu.VMEM((KV_RANK, N_HEADS_A * QUERY_BLOCK), F32),
                        pltpu.VMEM((SCORE_UNITS, seq, UNIT_COLS), F32)],
        compiler_params=_cparams(("parallel", "arbitrary")),
        name="dsa_attn",
    )(kidx, kcat, ckvt, qcat, qidx, w_t, gates, wuv)


def _dil_kernel(q_ref, k_ref, v_ref, gate_ref, c_ref, s_ref, o_ref,
                nat_ref, a4_ref, qn, kn, vn, q4, k4, v4, q16, k16, v16,
                oc0, ms0, ls0, oc1, ms1, ls1, oc2, ms2, ls2, *, qscale, unroll):
    seq = q_ref.shape[0]
    blk = N_BACK
    rc = ROW_CHUNK
    m4 = seq // 4
    m16 = seq // 16

    def build(src_ref, dn, d4, d16, rope_scale):
        def nat_body(c, carry):
            rows = pl.ds(pl.multiple_of(c * rc, rc), rc)
            x = src_ref[rows, :]
            if rope_scale is not None:
                x = x * c_ref[rows, :] + pltpu.roll(x, LANE // 2, 1) * s_ref[rows, :]
                if rope_scale != 1.0:
                    x = x * rope_scale
                nat_ref[rows, :] = x
            dn[rows, :] = x.astype(BF16)
            return carry

        lax.fori_loop(0, seq // rc, nat_body, 0)
        nat = src_ref if rope_scale is None else nat_ref

        def a4_body(c, carry):
            r4 = c // (m4 // rc)
            j0 = (c % (m4 // rc)) * rc
            x = nat[pl.ds(r4 + 4 * j0, rc, stride=4), :]
            rows = pl.ds(pl.multiple_of(c * rc, rc), rc)
            a4_ref[rows, :] = x
            d4[rows, :] = x.astype(BF16)
            return carry

        lax.fori_loop(0, seq // rc, a4_body, 0)

        def a16_body(s, carry):
            for r4 in range(4):
                x = a4_ref[pl.ds(r4 * m4 + s, m16, stride=4), :]
                d16[pl.ds(pl.multiple_of((4 * s + r4) * m16, m16), m16), :] = x.astype(BF16)
            return carry

        lax.fori_loop(0, 4, a16_body, 0)

    build(q_ref, qn, q4, q16, qscale)
    build(k_ref, kn, k4, k16, 1.0)
    build(v_ref, vn, v4, v16, None)

    qi = lax.broadcasted_iota(I32, (blk, 2 * blk), 0) + blk
    kj = lax.broadcasted_iota(I32, (blk, 2 * blk), 1)
    rel = qi - kj
    bias_band = jnp.where((rel >= 0) & (rel <= N_BACK), 0.0, NEG).astype(F32)
    no_prev = jnp.where(kj < blk, NEG, 0.0).astype(F32)
    qi1 = lax.broadcasted_iota(I32, (blk, blk), 0)
    kj1 = lax.broadcasted_iota(I32, (blk, blk), 1)
    bias_first = jnp.where(kj1 <= qi1, 0.0, NEG).astype(F32)
    nt = (((1,), (1,)), ((), ()))

    def run_config(qs, ks, vs, nmb, store):
        def rows_of(u, first):
            cur = pl.ds(pl.multiple_of(u * blk, blk), blk)
            if first is True:
                return (cur,)
            return (pl.ds(pl.multiple_of(jnp.maximum(u * blk - blk, 0), blk), blk), cur)

        def gather(ref, rows):
            parts = [ref[r, :] for r in rows]
            return parts[0] if len(parts) == 1 else jnp.concatenate(parts, axis=0)

        def loop_body(g, carry):
            units = []
            for t in range(unroll):
                if nmb <= unroll:
                    first = (t % nmb == 0)
                else:
                    first = jnp.where(g == 0, 1.0, 0.0).astype(F32) if t == 0 else False
                units.append((g * unroll + t, first))
            scores = []
            for u, first in units:
                rows = rows_of(u, first)
                if first is True:
                    bias = bias_first
                elif first is False:
                    bias = bias_band
                else:
                    bias = bias_band + no_prev * first
                scores.append(lax.dot_general(qs[rows[-1], :], gather(ks, rows), nt,
                                              preferred_element_type=F32) + bias)
            probs = []
            for s in scores:
                m = jnp.max(s, axis=-1, keepdims=True)
                p = jnp.exp2(s - m)
                l = jnp.sum(p, axis=-1, keepdims=True)
                probs.append((p.astype(BF16), jnp.broadcast_to(m, (blk, LANE)),
                              jnp.broadcast_to(l, (blk, LANE))))
            for (u, first), (p, m_b, l_b) in zip(units, probs):
                store(u, jnp.dot(p, gather(vs, rows_of(u, first)), preferred_element_type=F32), m_b, l_b)
            return carry

        lax.fori_loop(0, seq // blk // unroll, loop_body, 0)

    def store_contig(oc, ms, ls):
        def store(u, o, m_b, l_b):
            rows = pl.ds(pl.multiple_of(u * blk, blk), blk)
            oc[rows, :] = o
            ms[rows, :] = m_b
            ls[rows, :] = l_b
        return store

    def store_a16_to_a4(u, o, m_b, l_b):
        r16 = u // (m16 // blk)
        mb = u % (m16 // blk)
        rows = pl.ds((r16 % 4) * m4 + 4 * blk * mb + r16 // 4, blk, stride=4)
        oc2[rows, :] = o
        ms2[rows, :] = m_b
        ls2[rows, :] = l_b

    run_config(qn, kn, vn, seq // blk, store_contig(oc0, ms0, ls0))
    run_config(q4, k4, v4, m4 // blk, store_contig(oc1, ms1, ls1))
    run_config(q16, k16, v16, m16 // blk, store_a16_to_a4)

    def merge_body(c, carry):
        r4 = c // (m4 // rc)
        j0 = (c % (m4 // rc)) * rc
        nat_rows = pl.ds(r4 + 4 * j0, rc, stride=4)
        rows = pl.ds(pl.multiple_of(c * rc, rc), rc)
        m0 = ms0[nat_rows, :]
        m1 = ms1[rows, :]
        m2 = ms2[rows, :]
        mx = jnp.maximum(jnp.maximum(m0, m1), m2)
        e0 = jnp.exp2(m0 - mx)
        e1 = jnp.exp2(m1 - mx)
        e2 = jnp.exp2(m2 - mx)
        den = e0 * ls0[nat_rows, :] + e1 * ls1[rows, :] + e2 * ls2[rows, :]
        nat_ref[nat_rows, :] = (e0 * oc0[nat_rows, :] + e1 * oc1[rows, :] + e2 * oc2[rows, :]) / den
        return carry

    lax.fori_loop(0, seq // rc, merge_body, 0)

    def gate_body(c, carry):
        rows = pl.ds(pl.multiple_of(c * rc, rc), rc)
        g = gate_ref[rows, :].astype(F32)
        o_ref[rows, :] = (nat_ref[rows, :] * _silu(g)).astype(BF16)
        return carry

    lax.fori_loop(0, seq // rc, gate_body, 0)


def _dil_call(qkv, gates, ctab, stab, batch, seq):
    rows = qkv.shape[0]
    nh = N_HEADS_B
    qscale = HEAD_DIM ** -0.5 * LOG2E
    kern = functools.partial(_dil_kernel, qscale=qscale, unroll=DIL_UNROLL)
    blk = (seq, HEAD_DIM)
    return pl.pallas_call(
        kern,
        grid=(batch, nh),
        in_specs=[pl.BlockSpec(blk, lambda b, h: (b, h)),
                  pl.BlockSpec(blk, lambda b, h: (b, nh + h)),
                  pl.BlockSpec(blk, lambda b, h: (b, 2 * nh + h)),
                  pl.BlockSpec(blk, lambda b, h: (b, nh + h)),
                  pl.BlockSpec(blk, lambda b, h: (0, 0), pipeline_mode=pl.Buffered(1)),
                  pl.BlockSpec(blk, lambda b, h: (0, 0), pipeline_mode=pl.Buffered(1))],
        out_specs=pl.BlockSpec(blk, lambda b, h: (b, h)),
        out_shape=jax.ShapeDtypeStruct((rows, WIDTH_B), BF16),
        scratch_shapes=([pltpu.VMEM(blk, F32)] * 2 + [pltpu.VMEM(blk, BF16)] * 9
                        + [pltpu.VMEM(blk, F32)] * 9),
        compiler_params=_cparams(("parallel", "arbitrary")),
        name="dilated",
    )(qkv, qkv, qkv, gates, ctab, stab)


def _out_kernel(oa_ref, ob_ref, w_ref, x_ref, mod_ref, g_ref, o_ref):
    y = jnp.dot(oa_ref[...], w_ref[0:WIDTH_A, :], preferred_element_type=F32)
    y = y + jnp.dot(ob_ref[...], w_ref[WIDTH_A:WIDTH_A + WIDTH_B, :], preferred_element_type=F32)
    yn = y * lax.rsqrt(jnp.mean(y * y, axis=-1, keepdims=True) + EPS) * g_ref[...]
    o_ref[...] = x_ref[...] + mod_ref[0, 2:3, :] * yn


def _out_call(o_a, o_b, w_out, x2, mod3, g_post, seq):
    rows, d = x2.shape
    tm = OUT_TM
    tpb = seq // tm
    return pl.pallas_call(
        _out_kernel,
        grid=(rows // tm,),
        in_specs=[pl.BlockSpec((tm, WIDTH_A), lambda i: (i, 0)),
                  pl.BlockSpec((tm, WIDTH_B), lambda i: (i, 0)),
                  pl.BlockSpec((WIDTH_A + WIDTH_B, d), lambda i: (0, 0)),
                  pl.BlockSpec((tm, d), lambda i: (i, 0)),
                  pl.BlockSpec((1, 3, d), lambda i: (i // tpb, 0, 0)),
                  pl.BlockSpec((1, d), lambda i: (0, 0))],
        out_specs=pl.BlockSpec((tm, d), lambda i: (i, 0)),
        out_shape=jax.ShapeDtypeStruct((rows, d), F32),
        compiler_params=_cparams(("parallel",)),
        name="out_proj",
    )(o_a, o_b, w_out, x2, mod3, g_post)


def _rope_tables(seq):
    pos = jnp.arange(seq, dtype=F32)

    def cs(n_rot):
        inv = ROPE_THETA ** (-jnp.arange(0, n_rot, 2, dtype=F32) / n_rot)
        ang = pos[:, None] * inv[None, :]
        return jnp.cos(ang), jnp.sin(ang)

    c32, s32 = cs(ROPE_DIM)
    c16, s16 = cs(IDX_ROPE)
    ck = jnp.concatenate([c32, c32, jnp.ones((seq, LANE - ROPE_DIM), F32)], axis=1)
    sk = jnp.concatenate([-s32, s32, jnp.zeros((seq, LANE - ROPE_DIM), F32)], axis=1)
    ci = jnp.concatenate([c16, c16, jnp.ones((seq, LANE - IDX_ROPE), F32)], axis=1)
    si = jnp.concatenate([-s16, s16, jnp.zeros((seq, LANE - IDX_ROPE), F32)], axis=1)
    gap = jnp.ones((seq, LANE // 2 - ROPE_DIM // 2), F32)
    cd = jnp.concatenate([c32, gap, c32, gap], axis=1)
    sd = jnp.concatenate([-s32, 0.0 * gap, s32, 0.0 * gap], axis=1)
    return (c32.T, s32.T, c16.T, s16.T, ck, sk, ci, si, cd, sd)


def _pair_heads(wt):
    half = ROPE_DIM // 2
    mid = ROPE_DIM + (LANE // 2 - half)
    w3 = wt.reshape(-1, HEAD_DIM, wt.shape[-1])
    w3 = jnp.concatenate([w3[:, :half], w3[:, ROPE_DIM:mid], w3[:, half:ROPE_DIM], w3[:, mid:]], axis=1)
    return w3.reshape(wt.shape)


def _layer(x, c, w_ada, b_ada, g_pre, g_post, w_in, g_q, g_kv, w_uq, w_uq_idx, w_uk, w_uv, w_out):
    batch, seq, d = x.shape
    assert seq % PROJ_TM == 0 or seq < PROJ_TM
    assert seq % KEY_CHUNK == 0 and seq // 16 >= N_BACK
    assert all(w // dl == N_BACK for w, dl in DILATED_CONFIGS)
    rows = batch * seq
    x2 = x.reshape(rows, d)

    o_ki = Q_RANK + KV_RANK + ROPE_DIM
    o_ga = o_ki + IDX_DIM + IDX_HEADS
    o_q = o_ga + WIDTH_A
    o_k, o_v, o_gb = o_q + WIDTH_B, o_q + 2 * WIDTH_B, o_q + 3 * WIDTH_B
    assert o_gb + WIDTH_B == w_in.shape[1]
    wt = w_in.T.astype(BF16)
    zeros = lambda n: jnp.zeros((n, d), BF16)
    w_all = jnp.concatenate([
        wt[:o_ki], zeros(SLAB_KIDX - o_ki), wt[o_ki:o_ga], zeros(SMALL_W - SLAB_KIDX - (o_ga - o_ki)),
        _pair_heads(wt[o_q:o_k]), _pair_heads(wt[o_k:o_v]), wt[o_v:o_gb],
        wt[o_ga:o_q], wt[o_gb:o_gb + WIDTH_B]], axis=0)
    assert w_all.shape[0] == IN_PAD

    c_pad = jnp.zeros((PACK16, d), F32).at[:batch].set(c)
    mod = _mod_call(c_pad, w_ada, b_ada.reshape(1, -1))[:batch]
    mod3 = mod.reshape(batch, 3, d)

    small, qkv, gates = _proj_call(x2, mod3, g_pre.reshape(1, d), w_all, seq)

    tabs = _rope_tables(seq)
    wuk_t = jnp.transpose(jnp.pad(w_uk, ((0, 0), (ROPE_DIM, 0), (0, 0))), (0, 2, 1)).astype(BF16)
    qcat, qidx, w_t, kcat, kidx, ckvt = _prep_call(
        small, g_q.reshape(1, -1), g_kv.reshape(1, -1), w_uq.T.astype(BF16),
        w_uq_idx.T.astype(BF16), wuk_t, tabs[:8], seq)

    o_a = _dsa_call(kidx, kcat, ckvt, qcat, qidx, w_t, gates, w_uv.astype(BF16), batch, seq)
    o_b = _dil_call(qkv, gates, tabs[8], tabs[9], batch, seq)
    out = _out_call(o_a, o_b, w_out.astype(BF16), x2, mod3, g_post.reshape(1, d), seq)
    return out.reshape(batch, seq, d)


def kernel(x, c, w_ada, b_ada, g_pre, g_post, w_in, g_q, g_kv, w_uq, w_uq_idx, w_uk, w_uv, w_out):
    for layer in range(w_ada.shape[0]):
        x = _layer(x, c, w_ada[layer], b_ada[layer], g_pre[layer], g_post[layer], w_in[layer],
                   g_q[layer], g_kv[layer], w_uq[layer], w_uq_idx[layer], w_uk[layer],
                   w_uv[layer], w_out[layer])
    return x
```

```python
import functools

import numpy as np
import jax
import jax.numpy as jnp
from jax import lax
from jax.experimental import pallas as pl
from jax.experimental.pallas import tpu as pltpu

F32 = jnp.float32
BF16 = jnp.bfloat16
I32 = jnp.int32

HEAD_DIM = 128
ROPE_DIM = HEAD_DIM // 4
ROPE_THETA = 500000.0
EPS = 1e-6
NEG = -1e30
N_HEADS_A = 8
WIDTH_A = N_HEADS_A * HEAD_DIM
Q_RANK = 512
KV_RANK = 256
IDX_HEADS = 16
IDX_DIM = 64
IDX_ROPE = IDX_DIM // 4
TOPK_MAX = 256
QUERY_BLOCK = 128
N_HEADS_B = 8
WIDTH_B = N_HEADS_B * HEAD_DIM
DILATED_CONFIGS = ((128, 1), (512, 4), (2048, 16))
N_BACK = 128
SMALL_W = 1024
IN_PAD = SMALL_W + 5 * 1024
KCAT = 288
KIDX_PAD = 64
INT_MIN = -(2 ** 31)
LOG2E = 1.4426950408889634

LANE = 128
PACK16 = 16
SLAB_KROPE = Q_RANK + KV_RANK
SLAB_KIDX = SLAB_KROPE + LANE
assert SLAB_KIDX + LANE == SMALL_W
DIL_UNROLL = 16
VMEM_LIMIT = 56 * 1024 * 1024
PROJ_TM = 1024
PROJ_TN = 1024
PREP_TQ = 512
KEY_CHUNK = 512
OUT_TM = 512
ROW_CHUNK = 512
NORM_ROWS = 128


def _cparams(sem):
    return pltpu.CompilerParams(dimension_semantics=sem, vmem_limit_bytes=VMEM_LIMIT)


def _silu(g):
    return g * jax.nn.sigmoid(g)


def _split_bf16(x):
    hi = x.astype(BF16)
    return hi, (x - hi.astype(F32)).astype(BF16)


def _mod_kernel(c_ref, w_ref, b_ref, o_ref):
    a_hi, a_lo = _split_bf16(_silu(c_ref[...]))
    w_hi, w_lo = _split_bf16(w_ref[...])
    dot = functools.partial(jnp.dot, preferred_element_type=F32)
    o_ref[...] = dot(a_hi, w_hi) + (dot(a_hi, w_lo) + dot(a_lo, w_hi)) + b_ref[...]


def _mod_call(c_pad, w_ada, b_ada):
    rows, d = c_pad.shape
    n = w_ada.shape[1]
    tn = 1024
    return pl.pallas_call(
        _mod_kernel,
        grid=(n // tn,),
        in_specs=[pl.BlockSpec((rows, d), lambda j: (0, 0)),
                  pl.BlockSpec((d, tn), lambda j: (0, j)),
                  pl.BlockSpec((1, tn), lambda j: (0, j))],
        out_specs=pl.BlockSpec((rows, tn), lambda j: (0, j)),
        out_shape=jax.ShapeDtypeStruct((rows, n), F32),
        compiler_params=_cparams(("arbitrary",)),
        name="mod",
    )(c_pad, w_ada, b_ada)


def _proj_kernel(x_ref, mod_ref, g_ref, w_ref, small_ref, qkv_ref, gates_ref, h_ref, *,
                 n_small, n_qkv):
    j = pl.program_id(1)
    tm = x_ref.shape[0]
    half = tm // 2
    assert n_small == 1 and half % NORM_ROWS == 0

    def project(rows=slice(None)):
        return lax.dot_general(h_ref[rows, :], w_ref[...], (((1,), (1,)), ((), ())),
                               preferred_element_type=F32)

    @pl.when(j == 0)
    def _():
        shift = mod_ref[0, 0:1, :]
        scale1 = 1.0 + mod_ref[0, 1:2, :]
        g = g_ref[...]

        def norm_rows(rows):
            x = x_ref[rows, :]
            ms = jnp.mean(x * x, axis=-1, keepdims=True)
            y = x * lax.rsqrt(ms + EPS) * g
            h_ref[rows, :] = (y * scale1 + shift).astype(BF16)

        def body(r, carry):
            norm_rows(pl.ds(pl.multiple_of(r * NORM_ROWS, NORM_ROWS), NORM_ROWS))
            return carry

        lax.fori_loop(0, half // NORM_ROWS, body, 0)
        small_ref[0:half, :] = project(slice(0, half))
        for r in range(half // NORM_ROWS):
            norm_rows(slice(half + r * NORM_ROWS, half + (r + 1) * NORM_ROWS))
        small_ref[half:tm, :] = project(slice(half, tm))

    @pl.when((j >= n_small) & (j < n_small + n_qkv))
    def _():
        qkv_ref[...] = project()

    @pl.when(j >= n_small + n_qkv)
    def _():
        gates_ref[...] = project().astype(BF16)


def _proj_call(x2, mod3, g_pre, w_all, seq):
    rows, d = x2.shape
    tm = min(PROJ_TM, seq)
    tn = PROJ_TN
    n_small = SMALL_W // tn
    n_qkv = 3 * WIDTH_B // tn
    n_gate = (WIDTH_A + WIDTH_B) // tn
    tiles_per_batch = seq // tm
    kern = functools.partial(_proj_kernel, n_small=n_small, n_qkv=n_qkv)
    return pl.pallas_call(
        kern,
        grid=(rows // tm, n_small + n_qkv + n_gate),
        in_specs=[pl.BlockSpec((tm, d), lambda i, j: (i, 0)),
                  pl.BlockSpec((1, 3, d), lambda i, j: (i // tiles_per_batch, 0, 0)),
                  pl.BlockSpec((1, d), lambda i, j: (0, 0)),
                  pl.BlockSpec((tn, d), lambda i, j: (j, 0))],
        out_specs=[pl.BlockSpec((tm, tn), lambda i, j: (i, jnp.minimum(j, n_small - 1))),
                   pl.BlockSpec((tm, tn), lambda i, j: (i, jnp.clip(j - n_small, 0, n_qkv - 1))),
                   pl.BlockSpec((tm, tn),
                                lambda i, j: (i, jnp.clip(j - n_small - n_qkv, 0, n_gate - 1)))],
        out_shape=[jax.ShapeDtypeStruct((rows, SMALL_W), F32),
                   jax.ShapeDtypeStruct((rows, 3 * WIDTH_B), F32),
                   jax.ShapeDtypeStruct((rows, WIDTH_A + WIDTH_B), BF16)],
        scratch_shapes=[pltpu.VMEM((tm, d), BF16)],
        compiler_params=_cparams(("parallel", "arbitrary")),
        name="in_proj",
    )(x2, mod3, g_pre, w_all)


def _prep_kernel(small_ref, gq_ref, gkv_ref, wuq_ref, wuqi_ref, wuk_ref,
                 cosq_ref, sinq_ref, cosi_ref, sini_ref, ck_ref, sk_ref, ci_ref, si_ref,
                 qcat_ref, qidx_ref, wt_ref, kcat_ref, kidx_ref, ckvt_ref, *, qscale):
    tq = small_ref.shape[0]
    nblk = tq // QUERY_BLOCK
    cq = small_ref[:, 0:Q_RANK]
    cqn = (cq * lax.rsqrt(jnp.mean(cq * cq, axis=-1, keepdims=True) + EPS)
           * gq_ref[...]).astype(BF16)
    nt = (((1,), (1,)), ((), ()))

    q_t = lax.dot_general(wuq_ref[...], cqn, nt, preferred_element_type=F32)
    cosq = cosq_ref[...]
    sinq = sinq_ref[...]
    half = ROPE_DIM // 2
    zpad = [jnp.zeros((KCAT - KV_RANK - ROPE_DIM, tq), F32)] if KCAT > KV_RANK + ROPE_DIM else []
    for h in range(N_HEADS_A):
        base = h * HEAD_DIM
        x1 = q_t[base:base + half]
        x2 = q_t[base + half:base + ROPE_DIM]
        q_rope = jnp.concatenate([x1 * cosq - x2 * sinq, x2 * cosq + x1 * sinq] + zpad, axis=0)
        q_lat = jnp.dot(wuk_ref[h], q_t[base:base + HEAD_DIM].astype(BF16),
                        preferred_element_type=F32)
        q_lat = (q_lat * qscale).astype(BF16)
        q_rope = (q_rope * qscale).astype(BF16)
        for blk in range(nblk):
            cols = slice(blk * QUERY_BLOCK, (blk + 1) * QUERY_BLOCK)
            unit, hu = divmod(h, N_HEADS_A // SCORE_UNITS)
            lanes = slice(hu * QUERY_BLOCK, (hu + 1) * QUERY_BLOCK)
            qcat_ref[blk, unit, 0:KV_RANK, lanes] = q_lat[:, cols]
            qcat_ref[blk, unit, KV_RANK:KCAT, lanes] = q_rope[:, cols]

    qi_t = lax.dot_general(wuqi_ref[...], cqn, nt, preferred_element_type=F32)
    cosi = cosi_ref[...]
    sini = sini_ref[...]
    ihalf = IDX_ROPE // 2
    ipad = [jnp.zeros((KIDX_PAD - IDX_DIM, tq), F32)] if KIDX_PAD > IDX_DIM else []
    for h in range(IDX_HEADS):
        base = h * IDX_DIM
        x1 = qi_t[base:base + ihalf]
        x2 = qi_t[base + ihalf:base + IDX_ROPE]
        qi = jnp.concatenate([x1 * cosi - x2 * sini, x2 * cosi + x1 * sini,
                              qi_t[base + IDX_ROPE:base + IDX_DIM]] + ipad, axis=0).astype(BF16)
        for blk in range(nblk):
            cols = slice(blk * QUERY_BLOCK, (blk + 1) * QUERY_BLOCK)
            qidx_ref[blk, :, h * QUERY_BLOCK:(h + 1) * QUERY_BLOCK] = qi[:, cols]

    lane = lax.broadcasted_iota(I32, (tq, LANE), 1)

    slab_b = small_ref[:, SLAB_KIDX:SLAB_KIDX + LANE]
    swap_b = jnp.where(lane < ihalf, pltpu.roll(slab_b, LANE - ihalf, 1), pltpu.roll(slab_b, ihalf, 1))
    kidx_ref[...] = (slab_b * ci_ref[...] + swap_b * si_ref[...])[:, :KIDX_PAD].astype(BF16)
    w_t = slab_b.T[IDX_DIM:IDX_DIM + IDX_HEADS] * (IDX_HEADS ** -0.5 * IDX_DIM ** -0.5)
    for blk in range(nblk):
        wt_ref[blk] = w_t[:, blk * QUERY_BLOCK:(blk + 1) * QUERY_BLOCK]

    ckv = small_ref[:, Q_RANK:Q_RANK + KV_RANK]
    cn = ckv * lax.rsqrt(jnp.mean(ckv * ckv, axis=-1, keepdims=True) + EPS) * gkv_ref[...]
    kcat_ref[:, 0:KV_RANK] = cn.astype(BF16)
    ckvt_ref[0] = cn.T.astype(BF16)
    slab_a = small_ref[:, SLAB_KROPE:SLAB_KROPE + LANE]
    swap_a = jnp.where(lane < half, pltpu.roll(slab_a, LANE - half, 1), pltpu.roll(slab_a, half, 1))
    kcat_ref[:, KV_RANK:KCAT] = (slab_a * ck_ref[...] + swap_a * sk_ref[...])[:, :KCAT - KV_RANK].astype(BF16)


def _prep_call(small, g_q, g_kv, wuq_t, wuqi_t, wuk_t, tabs, seq):
    rows = small.shape[0]
    tq = PREP_TQ
    nblk = tq // QUERY_BLOCK
    tpb = seq // tq
    nqb = rows // QUERY_BLOCK
    qscale = HEAD_DIM ** -0.5 * LOG2E
    kern = functools.partial(_prep_kernel, qscale=qscale)
    const = lambda t: (0, 0)
    tcol = lambda t: (0, t % tpb)
    trow = lambda t: (t % tpb, 0)
    return pl.pallas_call(
        kern,
        grid=(rows // tq,),
        in_specs=[pl.BlockSpec((tq, SMALL_W), lambda t: (t, 0)),
                  pl.BlockSpec((1, Q_RANK), const),
                  pl.BlockSpec((1, KV_RANK), const),
                  pl.BlockSpec((WIDTH_A, Q_RANK), const),
                  pl.BlockSpec((IDX_HEADS * IDX_DIM, Q_RANK), const),
                  pl.BlockSpec((N_HEADS_A, KV_RANK, HEAD_DIM), lambda t: (0, 0, 0)),
                  pl.BlockSpec((ROPE_DIM // 2, tq), tcol),
                  pl.BlockSpec((ROPE_DIM // 2, tq), tcol),
                  pl.BlockSpec((IDX_ROPE // 2, tq), tcol),
                  pl.BlockSpec((IDX_ROPE // 2, tq), tcol),
                  pl.BlockSpec((tq, LANE), trow),
                  pl.BlockSpec((tq, LANE), trow),
                  pl.BlockSpec((tq, LANE), trow),
                  pl.BlockSpec((tq, LANE), trow)],
        out_specs=[pl.BlockSpec((nblk, SCORE_UNITS, KCAT, UNIT_COLS), lambda t: (t, 0, 0, 0)),
                   pl.BlockSpec((nblk, KIDX_PAD, IDX_HEADS * QUERY_BLOCK), lambda t: (t, 0, 0)),
                   pl.BlockSpec((nblk, IDX_HEADS, QUERY_BLOCK), lambda t: (t, 0, 0)),
                   pl.BlockSpec((tq, KCAT), lambda t: (t, 0)),
                   pl.BlockSpec((tq, KIDX_PAD), lambda t: (t, 0)),
                   pl.BlockSpec((1, KV_RANK, tq), lambda t: (t, 0, 0))],
        out_shape=[jax.ShapeDtypeStruct((nqb, SCORE_UNITS, KCAT, UNIT_COLS), BF16),
                   jax.ShapeDtypeStruct((nqb, KIDX_PAD, IDX_HEADS * QUERY_BLOCK), BF16),
                   jax.ShapeDtypeStruct((nqb, IDX_HEADS, QUERY_BLOCK), F32),
                   jax.ShapeDtypeStruct((rows, KCAT), BF16),
                   jax.ShapeDtypeStruct((rows, KIDX_PAD), BF16),
                   jax.ShapeDtypeStruct((rows // tq, KV_RANK, tq), BF16)],
        compiler_params=_cparams(("parallel",)),
        name="dsa_prep",
    )(small, g_q, g_kv, wuq_t, wuqi_t, wuk_t, *tabs)


def _float_of_rank(u):
    key = u ^ INT_MIN
    return lax.bitcast_convert_type(key ^ (lax.shift_right_arithmetic(key, 31) & 0x7FFFFFFF), F32)


_ABOVE_NEG = float(np.nextafter(np.float32(NEG), np.float32(0.0)))
TRIM_PASSES = 8
SELECT_BITS = 24
SCORE_UNITS = 2
UNIT_COLS = N_HEADS_A * QUERY_BLOCK // SCORE_UNITS


def _dsa_kernel(kidx_ref, kcat_ref, ckvt_ref, qcat_ref, qidx_ref, wt_ref, qidx_nx_ref, wt_nx_ref,
                gate_ref, wuv_ref, o_ref, isc_ref, acc_ref, s_ref, *, k_sel):
    i = pl.program_id(1)
    kc = KEY_CHUNK
    qb = QUERY_BLOCK
    per = kc // qb
    nch = (i + per) // per
    has_next = i + 1 < pl.num_programs(1)
    nch_next = (i + 1 + per) // per
    lane_q = lax.broadcasted_iota(I32, (kc, qb), 1)
    krow = lax.broadcasted_iota(I32, (kc, qb), 0)

    def chunk_rows(c):
        return pl.ds(c * kc if isinstance(c, int) else pl.multiple_of(c * kc, kc), kc)

    def fold(op, x):
        return op(x.reshape(kc // 8, 8, qb), axis=0)

    def chunk_pairs(step):
        def body(j, carry):
            step(2 * j)
            step(2 * j + 1)
            return carry

        lax.fori_loop(0, nch // 2, body, 0)

        @pl.when(nch % 2 == 1)
        def _():
            step(nch - 1)

    def index_chunk(c, q_ref, w_ref, block):
        kblk = kidx_ref[chunk_rows(c), :]
        acc = jnp.zeros((kc, qb), F32)
        group = 4
        for g in range(IDX_HEADS // group):
            lg = jnp.dot(kblk, q_ref[0, :, g * group * qb:(g + 1) * group * qb],
                         preferred_element_type=F32)
            for hh in range(group):
                h = g * group + hh
                acc = acc + jnp.maximum(lg[:, hh * qb:(hh + 1) * qb], 0.0) * w_ref[0, h:h + 1, :]
        causal = (c * kc + krow) <= (block * qb + lane_q)
        isc_ref[chunk_rows(c), :] = jnp.where(causal, acc, NEG)

    @pl.when(i == 0)
    def _():
        chunk_pairs(lambda c: index_chunk(c, qidx_ref, wt_ref, i))

    n_max = isc_ref.shape[0] // kc

    def count_chunks(n, t):
        parts = [fold(jnp.sum, jnp.where(isc_ref[chunk_rows(c), :] >= t, 1, 0).astype(I32))
                 for c in range(n)]
        while len(parts) > 1:
            parts = [a + b for a, b in zip(parts[::2], parts[1::2])] + parts[len(parts) & ~1:]
        return jnp.sum(parts[0], axis=0, keepdims=True)

    def bit_step(carry, bit, count):
        tu, cnt_t = carry
        cand_u = tu | lax.shift_left(jnp.int32(1), bit)
        cnt = count(_float_of_rank(cand_u))
        ok = cnt >= k_sel
        return jnp.where(ok, cand_u, tu), jnp.where(ok, cnt, cnt_t)

    def bis_body(s, carry):
        return bit_step(carry, 31 - s, lambda t: lax.switch(
            nch - 1, [functools.partial(count_chunks, n, t) for n in range(1, n_max + 1)]))

    start = (jnp.zeros((1, qb), I32), jnp.full((1, qb), k_sel, I32))
    units = s_ref.shape[0]
    bits_per_unit = SELECT_BITS // units

    def select_with_scores(n):
        def run():
            def body(u, carry):
                s_ref[u, 0:n * kc, :] = jnp.dot(kcat_ref[0:n * kc, :], qcat_ref[0, u],
                                                preferred_element_type=F32)
                for k in range(bits_per_unit):
                    carry = bit_step(carry, 31 - (u * bits_per_unit + k),
                                     functools.partial(count_chunks, n))
                return carry
            return lax.fori_loop(0, units, body, start)
        return run

    tu, cnt_t = lax.switch(nch - 1, [select_with_scores(n) for n in range(1, n_max + 1)])

    def trim(tu, cnt_t):
        thr = jnp.where(tu == 0, -jnp.inf, _float_of_rank(tu))
        few = thr < _ABOVE_NEG
        thr = jnp.where(few, _ABOVE_NEG, thr)
        cnt0 = jnp.where(few, k_sel, cnt_t)

        def trim_cond(state):
            _, cnt, n = state
            return (jnp.max(cnt) > k_sel) & (n < TRIM_PASSES)

        def trim_body(state):
            low, cnt, n = state

            def min_body(c, mn8):
                x = isc_ref[chunk_rows(c), :]
                sel = jnp.where(x >= thr, jnp.where(x > low, x, jnp.inf), jnp.inf)
                return jnp.minimum(mn8, fold(jnp.min, sel))

            mn = jnp.min(lax.fori_loop(0, nch, min_body, jnp.full((8, qb), jnp.inf, F32)),
                         axis=0, keepdims=True)

            def eq_body(c, cnt8):
                return cnt8 + fold(jnp.sum, jnp.where(isc_ref[chunk_rows(c), :] == mn, 1, 0).astype(I32))

            n_eq = jnp.sum(lax.fori_loop(0, nch, eq_body, jnp.zeros((8, qb), I32)),
                           axis=0, keepdims=True)
            can = (cnt > k_sel) & (cnt - n_eq >= k_sel)
            stalled = jnp.max(jnp.where(can, 1, 0)) == 0
            return (jnp.where(can, mn, low), jnp.where(can, cnt - n_eq, cnt),
                    jnp.where(stalled, TRIM_PASSES + 1, n + 1))

        low, cnt, n = lax.while_loop(trim_cond, trim_body,
                                     (jnp.full((1, qb), -jnp.inf, F32), cnt0, jnp.int32(0)))
        return thr, low, (n == TRIM_PASSES) & (jnp.max(cnt) > k_sel)

    thr, low, unresolved = trim(tu, cnt_t)
    thr, low = lax.cond(unresolved,
                        lambda: trim(*lax.fori_loop(SELECT_BITS, 32, bis_body, (tu, cnt_t)))[:2],
                        lambda: (thr, low))

    def fold_chunks(step, init):
        def body(j, carry):
            return step(2 * j + 1, step(2 * j, carry))

        carry = lax.fori_loop(0, nch // 2, body, init)
        return lax.cond(nch % 2 == 1, lambda cr: step(nch - 1, cr), lambda cr: cr, carry)

    def load_scores(rows):
        return jnp.concatenate([s_ref[u, rows, :] for u in range(units)], axis=1)

    def mask_and_max(c, m, index_next):
        rows = chunk_rows(c)
        x = isc_ref[rows, :]
        if index_next:
            index_chunk(c, qidx_nx_ref, wt_nx_ref, i + 1)
        bias = jnp.where(x >= thr, jnp.where(x > low, 0.0, NEG), NEG).astype(F32)
        s = load_scores(rows) + jnp.concatenate([bias] * N_HEADS_A, axis=1)
        for u in range(units):
            s_ref[u, rows, :] = s[:, u * UNIT_COLS:(u + 1) * UNIT_COLS]
        return jnp.maximum(m, jnp.max(s, axis=0, keepdims=True))

    def masked_max(index_next):
        def run():
            m = fold_chunks(functools.partial(mask_and_max, index_next=index_next),
                            jnp.full((1, N_HEADS_A * qb), NEG, F32))
            if index_next:
                @pl.when(nch_next > nch)
                def _():
                    index_chunk(nch, qidx_nx_ref, wt_nx_ref, i + 1)
            return m
        return run

    m = lax.cond(has_next, masked_max(True), masked_max(False))
    acc_ref[...] = jnp.zeros(acc_ref.shape, F32)

    def exp_sum_pv(c, l):
        p = jnp.exp2(load_scores(chunk_rows(c)) - m)
        acc_ref[...] += jnp.dot(ckvt_ref[c], p.astype(BF16), preferred_element_type=F32)
        return l + jnp.sum(p, axis=0, keepdims=True)

    l = fold_chunks(exp_sum_pv, jnp.zeros((1, N_HEADS_A * qb), F32))

    inv_l = 1.0 / l
    outs = []
    for h in range(N_HEADS_A):
        cols = slice(h * qb, (h + 1) * qb)
        o_t = acc_ref[:, cols] * inv_l[:, cols]
        outs.append(jnp.dot(o_t.T.astype(BF16), wuv_ref[h], preferred_element_type=F32))
    o = jnp.concatenate(outs, axis=1)
    g = gate_ref[...].astype(F32)
    o_ref[...] = (o * _silu(g)).astype(BF16)


def _dsa_call(kidx, kcat, ckvt, qcat, qidx, w_t, gates, wuv, batch, seq):
    rows = kidx.shape[0]
    nb = seq // QUERY_BLOCK
    nkc = seq // KEY_CHUNK
    k_sel = min(TOPK_MAX, seq // 4)
    kern = functools.partial(_dsa_kernel, k_sel=k_sel)
    nxt = lambda b, i: (b * nb + jnp.minimum(i + 1, nb - 1), 0, 0)
    return pl.pallas_call(
        kern,
        grid=(batch, nb),
        in_specs=[pl.BlockSpec((seq, KIDX_PAD), lambda b, i: (b, 0)),
                  pl.BlockSpec((seq, KCAT), lambda b, i: (b, 0)),
                  pl.BlockSpec((nkc, KV_RANK, KEY_CHUNK), lambda b, i: (b, 0, 0)),
                  pl.BlockSpec((1, SCORE_UNITS, KCAT, UNIT_COLS), lambda b, i: (b * nb + i, 0, 0, 0)),
                  pl.BlockSpec((1, KIDX_PAD, IDX_HEADS * QUERY_BLOCK), lambda b, i: (b * nb + i, 0, 0)),
                  pl.BlockSpec((1, IDX_HEADS, QUERY_BLOCK), lambda b, i: (b * nb + i, 0, 0)),
                  pl.BlockSpec((1, KIDX_PAD, IDX_HEADS * QUERY_BLOCK), nxt),
                  pl.BlockSpec((1, IDX_HEADS, QUERY_BLOCK), nxt),
                  pl.BlockSpec((QUERY_BLOCK, WIDTH_A), lambda b, i: (b * nb + i, 0)),
                  pl.BlockSpec((N_HEADS_A, KV_RANK, HEAD_DIM), lambda b, i: (0, 0, 0))],
        out_specs=pl.BlockSpec((QUERY_BLOCK, WIDTH_A), lambda b, i: (b * nb + i, 0)),
        out_shape=jax.ShapeDtypeStruct((rows, WIDTH_A), BF16),
        scratch_shapes=[pltpu.VMEM((seq, QUERY_BLOCK), F32),
                        pltpu.VMEM((KV_RANK, N_HEADS_A * QUERY_BLOCK), F32),
                        pltpu.VMEM((SCORE_UNITS, seq, UNIT_COLS), F32)],
        compiler_params=_cparams(("arbitrary", "arbitrary")),
        name="dsa_attn",
    )(kidx, kcat, ckvt, qcat, qidx, w_t, qidx, w_t, gates, wuv)


def _dil_kernel(q_ref, k_ref, v_ref, gate_ref, c_ref, s_ref, o_ref,
                nat_ref, a4_ref, qn, kn, vn, q4, k4, v4, q16, k16, v16,
                oc0, ms0, ls0, oc1, ms1, ls1, oc2, ms2, ls2, *, qscale, unroll):
    seq = q_ref.shape[0]
    blk = N_BACK
    rc = ROW_CHUNK
    m4 = seq // 4
    m16 = seq // 16

    def build(src_ref, dn, d4, d16, rope_scale):
        def nat_body(c, carry):
            rows = pl.ds(pl.multiple_of(c * rc, rc), rc)
            x = src_ref[rows, :]
            if rope_scale is not None:
                x = x * c_ref[rows, :] + pltpu.roll(x, LANE // 2, 1) * s_ref[rows, :]
                if rope_scale != 1.0:
                    x = x * rope_scale
                nat_ref[rows, :] = x
            dn[rows, :] = x.astype(BF16)
            return carry

        lax.fori_loop(0, seq // rc, nat_body, 0)
        nat = src_ref if rope_scale is None else nat_ref

        def a4_body(c, carry):
            r4 = c // (m4 // rc)
            j0 = (c % (m4 // rc)) * rc
            x = nat[pl.ds(r4 + 4 * j0, rc, stride=4), :]
            rows = pl.ds(pl.multiple_of(c * rc, rc), rc)
            a4_ref[rows, :] = x
            d4[rows, :] = x.astype(BF16)
            return carry

        lax.fori_loop(0, seq // rc, a4_body, 0)

        def a16_body(s, carry):
            for r4 in range(4):
                x = a4_ref[pl.ds(r4 * m4 + s, m16, stride=4), :]
                d16[pl.ds(pl.multiple_of((4 * s + r4) * m16, m16), m16), :] = x.astype(BF16)
            return carry

        lax.fori_loop(0, 4, a16_body, 0)

    build(q_ref, qn, q4, q16, qscale)
    build(k_ref, kn, k4, k16, 1.0)
    build(v_ref, vn, v4, v16, None)

    qi = lax.broadcasted_iota(I32, (blk, 2 * blk), 0) + blk
    kj = lax.broadcasted_iota(I32, (blk, 2 * blk), 1)
    rel = qi - kj
    bias_band = jnp.where((rel >= 0) & (rel <= N_BACK), 0.0, NEG).astype(F32)
    no_prev = jnp.where(kj < blk, NEG, 0.0).astype(F32)
    qi1 = lax.broadcasted_iota(I32, (blk, blk), 0)
    kj1 = lax.broadcasted_iota(I32, (blk, blk), 1)
    bias_first = jnp.where(kj1 <= qi1, 0.0, NEG).astype(F32)
    nt = (((1,), (1,)), ((), ()))

    def run_config(qs, ks, vs, nmb, store):
        def rows_of(u, first):
            cur = pl.ds(pl.multiple_of(u * blk, blk), blk)
            if first is True:
                return (cur,)
            return (pl.ds(pl.multiple_of(jnp.maximum(u * blk - blk, 0), blk), blk), cur)

        def gather(ref, rows):
            parts = [ref[r, :] for r in rows]
            return parts[0] if len(parts) == 1 else jnp.concatenate(parts, axis=0)

        def loop_body(g, carry):
            units = []
            for t in range(unroll):
                if nmb <= unroll:
                    first = (t % nmb == 0)
                else:
                    first = jnp.where(g == 0, 1.0, 0.0).astype(F32) if t == 0 else False
                units.append((g * unroll + t, first))
            scores = []
            for u, first in units:
                rows = rows_of(u, first)
                if first is True:
                    bias = bias_first
                elif first is False:
                    bias = bias_band
                else:
                    bias = bias_band + no_prev * first
                scores.append(lax.dot_general(qs[rows[-1], :], gather(ks, rows), nt,
                                              preferred_element_type=F32) + bias)
            probs = []
            for s in scores:
                m = jnp.max(s, axis=-1, keepdims=True)
                p = jnp.exp2(s - m)
                l = jnp.sum(p, axis=-1, keepdims=True)
                probs.append((p.astype(BF16), jnp.broadcast_to(m, (blk, LANE)),
                              jnp.broadcast_to(l, (blk, LANE))))
            for (u, first), (p, m_b, l_b) in zip(units, probs):
                store(u, jnp.dot(p, gather(vs, rows_of(u, first)), preferred_element_type=F32), m_b, l_b)
            return carry

        lax.fori_loop(0, seq // blk // unroll, loop_body, 0)

    def store_contig(oc, ms, ls):
        def store(u, o, m_b, l_b):
            rows = pl.ds(pl.multiple_of(u * blk, blk), blk)
            oc[rows, :] = o
            ms[rows, :] = m_b
            ls[rows, :] = l_b
        return store

    def store_a16_to_a4(u, o, m_b, l_b):
        r16 = u // (m16 // blk)
        mb = u % (m16 // blk)
        rows = pl.ds((r16 % 4) * m4 + 4 * blk * mb + r16 // 4, blk, stride=4)
        oc2[rows, :] = o
        ms2[rows, :] = m_b
        ls2[rows, :] = l_b

    run_config(qn, kn, vn, seq // blk, store_contig(oc0, ms0, ls0))
    run_config(q4, k4, v4, m4 // blk, store_contig(oc1, ms1, ls1))
    run_config(q16, k16, v16, m16 // blk, store_a16_to_a4)

    def merge_body(c, carry):
        r4 = c // (m4 // rc)
        j0 = (c % (m4 // rc)) * rc
        nat_rows = pl.ds(r4 + 4 * j0, rc, stride=4)
        rows = pl.ds(pl.multiple_of(c * rc, rc), rc)
        m0 = ms0[nat_rows, :]
        m1 = ms1[rows, :]
        m2 = ms2[rows, :]
        mx = jnp.maximum(jnp.maximum(m0, m1), m2)
        e0 = jnp.exp2(m0 - mx)
        e1 = jnp.exp2(m1 - mx)
        e2 = jnp.exp2(m2 - mx)
        den = e0 * ls0[nat_rows, :] + e1 * ls1[rows, :] + e2 * ls2[rows, :]
        nat_ref[nat_rows, :] = (e0 * oc0[nat_rows, :] + e1 * oc1[rows, :] + e2 * oc2[rows, :]) / den
        return carry

    lax.fori_loop(0, seq // rc, merge_body, 0)

    def gate_body(c, carry):
        rows = pl.ds(pl.multiple_of(c * rc, rc), rc)
        g = gate_ref[rows, :].astype(F32)
        o_ref[rows, :] = (nat_ref[rows, :] * _silu(g)).astype(BF16)
        return carry

    lax.fori_loop(0, seq // rc, gate_body, 0)


def _dil_call(qkv, gates, ctab, stab, batch, seq):
    rows = qkv.shape[0]
    nh = N_HEADS_B
    qscale = HEAD_DIM ** -0.5 * LOG2E
    kern = functools.partial(_dil_kernel, qscale=qscale, unroll=DIL_UNROLL)
    blk = (seq, HEAD_DIM)
    return pl.pallas_call(
        kern,
        grid=(batch, nh),
        in_specs=[pl.BlockSpec(blk, lambda b, h: (b, h)),
                  pl.BlockSpec(blk, lambda b, h: (b, nh + h)),
                  pl.BlockSpec(blk, lambda b, h: (b, 2 * nh + h)),
                  pl.BlockSpec(blk, lambda b, h: (b, nh + h)),
                  pl.BlockSpec(blk, lambda b, h: (0, 0), pipeline_mode=pl.Buffered(1)),
                  pl.BlockSpec(blk, lambda b, h: (0, 0), pipeline_mode=pl.Buffered(1))],
        out_specs=pl.BlockSpec(blk, lambda b, h: (b, h)),
        out_shape=jax.ShapeDtypeStruct((rows, WIDTH_B), BF16),
        scratch_shapes=([pltpu.VMEM(blk, F32)] * 2 + [pltpu.VMEM(blk, BF16)] * 9
                        + [pltpu.VMEM(blk, F32)] * 9),
        compiler_params=_cparams(("parallel", "arbitrary")),
        name="dilated",
    )(qkv, qkv, qkv, gates, ctab, stab)


def _out_kernel(oa_ref, ob_ref, w_ref, x_ref, mod_ref, g_ref, o_ref):
    y = jnp.dot(oa_ref[...], w_ref[0:WIDTH_A, :], preferred_element_type=F32)
    y = y + jnp.dot(ob_ref[...], w_ref[WIDTH_A:WIDTH_A + WIDTH_B, :], preferred_element_type=F32)
    yn = y * lax.rsqrt(jnp.mean(y * y, axis=-1, keepdims=True) + EPS) * g_ref[...]
    o_ref[...] = x_ref[...] + mod_ref[0, 2:3, :] * yn


def _out_call(o_a, o_b, w_out, x2, mod3, g_post, seq):
    rows, d = x2.shape
    tm = OUT_TM
    tpb = seq // tm
    return pl.pallas_call(
        _out_kernel,
        grid=(rows // tm,),
        in_specs=[pl.BlockSpec((tm, WIDTH_A), lambda i: (i, 0)),
                  pl.BlockSpec((tm, WIDTH_B), lambda i: (i, 0)),
                  pl.BlockSpec((WIDTH_A + WIDTH_B, d), lambda i: (0, 0)),
                  pl.BlockSpec((tm, d), lambda i: (i, 0)),
                  pl.BlockSpec((1, 3, d), lambda i: (i // tpb, 0, 0)),
                  pl.BlockSpec((1, d), lambda i: (0, 0))],
        out_specs=pl.BlockSpec((tm, d), lambda i: (i, 0)),
        out_shape=jax.ShapeDtypeStruct((rows, d), F32),
        compiler_params=_cparams(("parallel",)),
        name="out_proj",
    )(o_a, o_b, w_out, x2, mod3, g_post)


def _rope_tables(seq):
    pos = jnp.arange(seq, dtype=F32)

    def cs(n_rot):
        inv = ROPE_THETA ** (-jnp.arange(0, n_rot, 2, dtype=F32) / n_rot)
        ang = pos[:, None] * inv[None, :]
        return jnp.cos(ang), jnp.sin(ang)

    c32, s32 = cs(ROPE_DIM)
    c16, s16 = cs(IDX_ROPE)
    ck = jnp.concatenate([c32, c32, jnp.ones((seq, LANE - ROPE_DIM), F32)], axis=1)
    sk = jnp.concatenate([-s32, s32, jnp.zeros((seq, LANE - ROPE_DIM), F32)], axis=1)
    ci = jnp.concatenate([c16, c16, jnp.ones((seq, LANE - IDX_ROPE), F32)], axis=1)
    si = jnp.concatenate([-s16, s16, jnp.zeros((seq, LANE - IDX_ROPE), F32)], axis=1)
    gap = jnp.ones((seq, LANE // 2 - ROPE_DIM // 2), F32)
    cd = jnp.concatenate([c32, gap, c32, gap], axis=1)
    sd = jnp.concatenate([-s32, 0.0 * gap, s32, 0.0 * gap], axis=1)
    return (c32.T, s32.T, c16.T, s16.T, ck, sk, ci, si, cd, sd)


def _pair_heads(wt):
    half = ROPE_DIM // 2
    mid = ROPE_DIM + (LANE // 2 - half)
    w3 = wt.reshape(-1, HEAD_DIM, wt.shape[-1])
    w3 = jnp.concatenate([w3[:, :half], w3[:, ROPE_DIM:mid], w3[:, half:ROPE_DIM], w3[:, mid:]], axis=1)
    return w3.reshape(wt.shape)


def _layer(x, c, w_ada, b_ada, g_pre, g_post, w_in, g_q, g_kv, w_uq, w_uq_idx, w_uk, w_uv, w_out):
    batch, seq, d = x.shape
    assert seq % PROJ_TM == 0 or seq < PROJ_TM
    assert seq % KEY_CHUNK == 0 and seq // 16 >= N_BACK
    assert all(w // dl == N_BACK for w, dl in DILATED_CONFIGS)
    rows = batch * seq
    x2 = x.reshape(rows, d)

    o_ki = Q_RANK + KV_RANK + ROPE_DIM
    o_ga = o_ki + IDX_DIM + IDX_HEADS
    o_q = o_ga + WIDTH_A
    o_k, o_v, o_gb = o_q + WIDTH_B, o_q + 2 * WIDTH_B, o_q + 3 * WIDTH_B
    assert o_gb + WIDTH_B == w_in.shape[1]
    wt = w_in.T.astype(BF16)
    zeros = lambda n: jnp.zeros((n, d), BF16)
    w_all = jnp.concatenate([
        wt[:o_ki], zeros(SLAB_KIDX - o_ki), wt[o_ki:o_ga], zeros(SMALL_W - SLAB_KIDX - (o_ga - o_ki)),
        _pair_heads(wt[o_q:o_k]), _pair_heads(wt[o_k:o_v]), wt[o_v:o_gb],
        wt[o_ga:o_q], wt[o_gb:o_gb + WIDTH_B]], axis=0)
    assert w_all.shape[0] == IN_PAD

    c_pad = jnp.zeros((PACK16, d), F32).at[:batch].set(c)
    mod = _mod_call(c_pad, w_ada, b_ada.reshape(1, -1))[:batch]
    mod3 = mod.reshape(batch, 3, d)

    small, qkv, gates = _proj_call(x2, mod3, g_pre.reshape(1, d), w_all, seq)

    tabs = _rope_tables(seq)
    wuk_t = jnp.transpose(jnp.pad(w_uk, ((0, 0), (ROPE_DIM, 0), (0, 0))), (0, 2, 1)).astype(BF16)
    qcat, qidx, w_t, kcat, kidx, ckvt = _prep_call(
        small, g_q.reshape(1, -1), g_kv.reshape(1, -1), w_uq.T.astype(BF16),
        w_uq_idx.T.astype(BF16), wuk_t, tabs[:8], seq)

    o_a = _dsa_call(kidx, kcat, ckvt, qcat, qidx, w_t, gates, w_uv.astype(BF16), batch, seq)
    o_b = _dil_call(qkv, gates, tabs[8], tabs[9], batch, seq)
    out = _out_call(o_a, o_b, w_out.astype(BF16), x2, mod3, g_post.reshape(1, d), seq)
    return out.reshape(batch, seq, d)


def kernel(x, c, w_ada, b_ada, g_pre, g_post, w_in, g_q, g_kv, w_uq, w_uq_idx, w_uk, w_uv, w_out):
    for layer in range(w_ada.shape[0]):
        x = _layer(x, c, w_ada[layer], b_ada[layer], g_pre[layer], g_post[layer], w_in[layer],
                   g_q[layer], g_kv[layer], w_uq[layer], w_uq_idx[layer], w_uk[layer],
                   w_uv[layer], w_out[layer])
    return x
```

```python
import functools

import numpy as np
import jax
import jax.numpy as jnp
from jax import lax
from jax.experimental import pallas as pl
from jax.experimental.pallas import tpu as pltpu

F32 = jnp.float32
BF16 = jnp.bfloat16
I32 = jnp.int32

HEAD_DIM = 128
ROPE_DIM = HEAD_DIM // 4
ROPE_THETA = 500000.0
EPS = 1e-6
NEG = -1e30
N_HEADS_A = 8
WIDTH_A = N_HEADS_A * HEAD_DIM
Q_RANK = 512
KV_RANK = 256
IDX_HEADS = 16
IDX_DIM = 64
IDX_ROPE = IDX_DIM // 4
TOPK_MAX = 256
QUERY_BLOCK = 128
N_HEADS_B = 8
WIDTH_B = N_HEADS_B * HEAD_DIM
DILATED_CONFIGS = ((128, 1), (512, 4), (2048, 16))
N_BACK = 128
SMALL_W = 1024
IN_PAD = SMALL_W + 5 * 1024
KCAT = 288
KIDX_PAD = 64
INT_MIN = -(2 ** 31)
LOG2E = 1.4426950408889634

LANE = 128
PACK16 = 16
SLAB_KROPE = Q_RANK + KV_RANK
SLAB_KIDX = SLAB_KROPE + LANE
assert SLAB_KIDX + LANE == SMALL_W
DIL_UNROLL = 16
VMEM_LIMIT = 56 * 1024 * 1024
PROJ_TM = 1024
PROJ_TN = 1024
PREP_TQ = 1024
KEY_CHUNK = 512
OUT_TM = 512
ROW_CHUNK = 512
NORM_ROWS = 128


def _cparams(sem):
    return pltpu.CompilerParams(dimension_semantics=sem, vmem_limit_bytes=VMEM_LIMIT)


def _silu(g):
    return g * jax.nn.sigmoid(g)


def _split_bf16(x):
    hi = x.astype(BF16)
    return hi, (x - hi.astype(F32)).astype(BF16)


def _mod_kernel(c_ref, w_ref, b_ref, o_ref):
    a_hi, a_lo = _split_bf16(_silu(c_ref[...]))
    w_hi, w_lo = _split_bf16(w_ref[...])
    dot = functools.partial(jnp.dot, preferred_element_type=F32)
    o_ref[...] = dot(a_hi, w_hi) + (dot(a_hi, w_lo) + dot(a_lo, w_hi)) + b_ref[...]


def _mod_call(c_pad, w_ada, b_ada):
    rows, d = c_pad.shape
    n = w_ada.shape[1]
    tn = 1024
    return pl.pallas_call(
        _mod_kernel,
        grid=(n // tn,),
        in_specs=[pl.BlockSpec((rows, d), lambda j: (0, 0)),
                  pl.BlockSpec((d, tn), lambda j: (0, j)),
                  pl.BlockSpec((1, tn), lambda j: (0, j))],
        out_specs=pl.BlockSpec((rows, tn), lambda j: (0, j)),
        out_shape=jax.ShapeDtypeStruct((rows, n), F32),
        compiler_params=_cparams(("arbitrary",)),
        name="mod",
    )(c_pad, w_ada, b_ada)


def _proj_kernel(x_ref, mod_ref, g_ref, w_ref, small_ref, qkv_ref, gates_ref, h_ref, *,
                 n_small, n_qkv):
    j = pl.program_id(1)
    tm = x_ref.shape[0]
    half = tm // 2
    assert n_small == 1 and half % NORM_ROWS == 0

    def project(rows=slice(None)):
        return lax.dot_general(h_ref[rows, :], w_ref[...], (((1,), (1,)), ((), ())),
                               preferred_element_type=F32)

    @pl.when(j == 0)
    def _():
        shift = mod_ref[0, 0:1, :]
        scale1 = 1.0 + mod_ref[0, 1:2, :]
        g = g_ref[...]

        def norm_rows(rows):
            x = x_ref[rows, :]
            ms = jnp.mean(x * x, axis=-1, keepdims=True)
            y = x * lax.rsqrt(ms + EPS) * g
            h_ref[rows, :] = (y * scale1 + shift).astype(BF16)

        def body(r, carry):
            norm_rows(pl.ds(pl.multiple_of(r * NORM_ROWS, NORM_ROWS), NORM_ROWS))
            return carry

        lax.fori_loop(0, half // NORM_ROWS, body, 0)
        small_ref[0:half, :] = project(slice(0, half))
        for r in range(half // NORM_ROWS):
            norm_rows(slice(half + r * NORM_ROWS, half + (r + 1) * NORM_ROWS))
        small_ref[half:tm, :] = project(slice(half, tm))

    @pl.when((j >= n_small) & (j < n_small + n_qkv))
    def _():
        qkv_ref[...] = project()

    @pl.when(j >= n_small + n_qkv)
    def _():
        gates_ref[...] = project().astype(BF16)


def _proj_call(x2, mod3, g_pre, w_all, seq):
    rows, d = x2.shape
    tm = min(PROJ_TM, seq)
    tn = PROJ_TN
    n_small = SMALL_W // tn
    n_qkv = 3 * WIDTH_B // tn
    n_gate = (WIDTH_A + WIDTH_B) // tn
    tiles_per_batch = seq // tm
    kern = functools.partial(_proj_kernel, n_small=n_small, n_qkv=n_qkv)
    return pl.pallas_call(
        kern,
        grid=(rows // tm, n_small + n_qkv + n_gate),
        in_specs=[pl.BlockSpec((tm, d), lambda i, j: (i, 0)),
                  pl.BlockSpec((1, 3, d), lambda i, j: (i // tiles_per_batch, 0, 0)),
                  pl.BlockSpec((1, d), lambda i, j: (0, 0)),
                  pl.BlockSpec((tn, d), lambda i, j: (j, 0))],
        out_specs=[pl.BlockSpec((tm, tn), lambda i, j: (i, jnp.minimum(j, n_small - 1))),
                   pl.BlockSpec((tm, tn), lambda i, j: (i, jnp.clip(j - n_small, 0, n_qkv - 1))),
                   pl.BlockSpec((tm, tn),
                                lambda i, j: (i, jnp.clip(j - n_small - n_qkv, 0, n_gate - 1)))],
        out_shape=[jax.ShapeDtypeStruct((rows, SMALL_W), F32),
                   jax.ShapeDtypeStruct((rows, 3 * WIDTH_B), F32),
                   jax.ShapeDtypeStruct((rows, WIDTH_A + WIDTH_B), BF16)],
        scratch_shapes=[pltpu.VMEM((tm, d), BF16)],
        compiler_params=_cparams(("parallel", "arbitrary")),
        name="in_proj",
    )(x2, mod3, g_pre, w_all)


def _prep_kernel(small_ref, gq_ref, gkv_ref, wuq_ref, wuqi_ref, wuk_ref,
                 cosq_ref, sinq_ref, cosi_ref, sini_ref, ck_ref, sk_ref, ci_ref, si_ref,
                 qcat_ref, qidx_ref, wt_ref, kcat_ref, kidx_ref, ckvt_ref, *, qscale):
    tq = small_ref.shape[0]
    nblk = tq // QUERY_BLOCK
    cq = small_ref[:, 0:Q_RANK]
    cqn = (cq * lax.rsqrt(jnp.mean(cq * cq, axis=-1, keepdims=True) + EPS)
           * gq_ref[...]).astype(BF16)
    nt = (((1,), (1,)), ((), ()))

    q_t = lax.dot_general(wuq_ref[...], cqn, nt, preferred_element_type=F32)
    cosq = cosq_ref[...]
    sinq = sinq_ref[...]
    half = ROPE_DIM // 2
    zpad = [jnp.zeros((KCAT - KV_RANK - ROPE_DIM, tq), F32)] if KCAT > KV_RANK + ROPE_DIM else []
    for h in range(N_HEADS_A):
        base = h * HEAD_DIM
        x1 = q_t[base:base + half]
        x2 = q_t[base + half:base + ROPE_DIM]
        q_rope = jnp.concatenate([x1 * cosq - x2 * sinq, x2 * cosq + x1 * sinq] + zpad, axis=0)
        q_lat = jnp.dot(wuk_ref[h], q_t[base:base + HEAD_DIM].astype(BF16),
                        preferred_element_type=F32)
        q_lat = (q_lat * qscale).astype(BF16)
        q_rope = (q_rope * qscale).astype(BF16)
        for blk in range(nblk):
            cols = slice(blk * QUERY_BLOCK, (blk + 1) * QUERY_BLOCK)
            unit, hu = divmod(h, N_HEADS_A // SCORE_UNITS)
            lanes = slice(hu * QUERY_BLOCK, (hu + 1) * QUERY_BLOCK)
            qcat_ref[blk, unit, 0:KV_RANK, lanes] = q_lat[:, cols]
            qcat_ref[blk, unit, KV_RANK:KCAT, lanes] = q_rope[:, cols]

    qi_t = lax.dot_general(wuqi_ref[...], cqn, nt, preferred_element_type=F32)
    cosi = cosi_ref[...]
    sini = sini_ref[...]
    ihalf = IDX_ROPE // 2
    ipad = [jnp.zeros((KIDX_PAD - IDX_DIM, tq), F32)] if KIDX_PAD > IDX_DIM else []
    for h in range(IDX_HEADS):
        base = h * IDX_DIM
        x1 = qi_t[base:base + ihalf]
        x2 = qi_t[base + ihalf:base + IDX_ROPE]
        qi = jnp.concatenate([x1 * cosi - x2 * sini, x2 * cosi + x1 * sini,
                              qi_t[base + IDX_ROPE:base + IDX_DIM]] + ipad, axis=0).astype(BF16)
        for blk in range(nblk):
            cols = slice(blk * QUERY_BLOCK, (blk + 1) * QUERY_BLOCK)
            qidx_ref[blk, :, h * QUERY_BLOCK:(h + 1) * QUERY_BLOCK] = qi[:, cols]

    lane = lax.broadcasted_iota(I32, (tq, LANE), 1)

    slab_b = small_ref[:, SLAB_KIDX:SLAB_KIDX + LANE]
    swap_b = jnp.where(lane < ihalf, pltpu.roll(slab_b, LANE - ihalf, 1), pltpu.roll(slab_b, ihalf, 1))
    kidx_ref[...] = (slab_b * ci_ref[...] + swap_b * si_ref[...])[:, :KIDX_PAD].astype(BF16)
    w_t = slab_b.T[IDX_DIM:IDX_DIM + IDX_HEADS] * (IDX_HEADS ** -0.5 * IDX_DIM ** -0.5)
    for blk in range(nblk):
        wt_ref[blk] = w_t[:, blk * QUERY_BLOCK:(blk + 1) * QUERY_BLOCK]

    ckv = small_ref[:, Q_RANK:Q_RANK + KV_RANK]
    cn = ckv * lax.rsqrt(jnp.mean(ckv * ckv, axis=-1, keepdims=True) + EPS) * gkv_ref[...]
    kcat_ref[:, 0:KV_RANK] = cn.astype(BF16)
    for j in range(tq // KEY_CHUNK):
        ckvt_ref[j] = cn[j * KEY_CHUNK:(j + 1) * KEY_CHUNK].T.astype(BF16)
    slab_a = small_ref[:, SLAB_KROPE:SLAB_KROPE + LANE]
    swap_a = jnp.where(lane < half, pltpu.roll(slab_a, LANE - half, 1), pltpu.roll(slab_a, half, 1))
    kcat_ref[:, KV_RANK:KCAT] = (slab_a * ck_ref[...] + swap_a * sk_ref[...])[:, :KCAT - KV_RANK].astype(BF16)


def _prep_call(small, g_q, g_kv, wuq_t, wuqi_t, wuk_t, tabs, seq):
    rows = small.shape[0]
    tq = PREP_TQ
    nblk = tq // QUERY_BLOCK
    tpb = seq // tq
    nqb = rows // QUERY_BLOCK
    qscale = HEAD_DIM ** -0.5 * LOG2E
    kern = functools.partial(_prep_kernel, qscale=qscale)
    const = lambda t: (0, 0)
    tcol = lambda t: (0, t % tpb)
    trow = lambda t: (t % tpb, 0)
    return pl.pallas_call(
        kern,
        grid=(rows // tq,),
        in_specs=[pl.BlockSpec((tq, SMALL_W), lambda t: (t, 0)),
                  pl.BlockSpec((1, Q_RANK), const),
                  pl.BlockSpec((1, KV_RANK), const),
                  pl.BlockSpec((WIDTH_A, Q_RANK), const),
                  pl.BlockSpec((IDX_HEADS * IDX_DIM, Q_RANK), const),
                  pl.BlockSpec((N_HEADS_A, KV_RANK, HEAD_DIM), lambda t: (0, 0, 0)),
                  pl.BlockSpec((ROPE_DIM // 2, tq), tcol),
                  pl.BlockSpec((ROPE_DIM // 2, tq), tcol),
                  pl.BlockSpec((IDX_ROPE // 2, tq), tcol),
                  pl.BlockSpec((IDX_ROPE // 2, tq), tcol),
                  pl.BlockSpec((tq, LANE), trow),
                  pl.BlockSpec((tq, LANE), trow),
                  pl.BlockSpec((tq, LANE), trow),
                  pl.BlockSpec((tq, LANE), trow)],
        out_specs=[pl.BlockSpec((nblk, SCORE_UNITS, KCAT, UNIT_COLS), lambda t: (t, 0, 0, 0)),
                   pl.BlockSpec((nblk, KIDX_PAD, IDX_HEADS * QUERY_BLOCK), lambda t: (t, 0, 0)),
                   pl.BlockSpec((nblk, IDX_HEADS, QUERY_BLOCK), lambda t: (t, 0, 0)),
                   pl.BlockSpec((tq, KCAT), lambda t: (t, 0)),
                   pl.BlockSpec((tq, KIDX_PAD), lambda t: (t, 0)),
                   pl.BlockSpec((tq // KEY_CHUNK, KV_RANK, KEY_CHUNK), lambda t: (t, 0, 0))],
        out_shape=[jax.ShapeDtypeStruct((nqb, SCORE_UNITS, KCAT, UNIT_COLS), BF16),
                   jax.ShapeDtypeStruct((nqb, KIDX_PAD, IDX_HEADS * QUERY_BLOCK), BF16),
                   jax.ShapeDtypeStruct((nqb, IDX_HEADS, QUERY_BLOCK), F32),
                   jax.ShapeDtypeStruct((rows, KCAT), BF16),
                   jax.ShapeDtypeStruct((rows, KIDX_PAD), BF16),
                   jax.ShapeDtypeStruct((rows // KEY_CHUNK, KV_RANK, KEY_CHUNK), BF16)],
        compiler_params=_cparams(("parallel",)),
        name="dsa_prep",
    )(small, g_q, g_kv, wuq_t, wuqi_t, wuk_t, *tabs)


def _float_of_rank(u):
    key = u ^ INT_MIN
    return lax.bitcast_convert_type(key ^ (lax.shift_right_arithmetic(key, 31) & 0x7FFFFFFF), F32)


_ABOVE_NEG = float(np.nextafter(np.float32(NEG), np.float32(0.0)))
TRIM_PASSES = 8
SELECT_BITS = 24
SCORE_UNITS = 2
UNIT_COLS = N_HEADS_A * QUERY_BLOCK // SCORE_UNITS


def _dsa_kernel(kidx_ref, kcat_ref, ckvt_ref, qcat_ref, qidx_ref, wt_ref, gate_ref, wuv_ref,
                o_ref, isc_ref, acc_ref, s_ref, *, k_sel):
    i = pl.program_id(1)
    kc = KEY_CHUNK
    qb = QUERY_BLOCK
    per = kc // qb
    nch = (i + per) // per
    qpos = i * qb + lax.broadcasted_iota(I32, (kc, qb), 1)
    krow = lax.broadcasted_iota(I32, (kc, qb), 0)

    def chunk_rows(c):
        return pl.ds(c * kc if isinstance(c, int) else pl.multiple_of(c * kc, kc), kc)

    def fold(op, x):
        return op(x.reshape(kc // 8, 8, qb), axis=0)

    def chunk_pairs(step):
        def body(j, carry):
            step(2 * j)
            step(2 * j + 1)
            return carry

        lax.fori_loop(0, nch // 2, body, 0)

        @pl.when(nch % 2 == 1)
        def _():
            step(nch - 1)

    def index_chunk(c):
        kblk = kidx_ref[chunk_rows(c), :]
        acc = jnp.zeros((kc, qb), F32)
        group = 4
        for g in range(IDX_HEADS // group):
            lg = jnp.dot(kblk, qidx_ref[0, :, g * group * qb:(g + 1) * group * qb],
                         preferred_element_type=F32)
            for hh in range(group):
                h = g * group + hh
                acc = acc + jnp.maximum(lg[:, hh * qb:(hh + 1) * qb], 0.0) * wt_ref[0, h:h + 1, :]
        causal = (c * kc + krow) <= qpos
        isc_ref[chunk_rows(c), :] = jnp.where(causal, acc, NEG)

    chunk_pairs(index_chunk)

    n_max = isc_ref.shape[0] // kc

    def count_chunks(n, t):
        parts = [fold(jnp.sum, jnp.where(isc_ref[chunk_rows(c), :] >= t, 1, 0).astype(I32))
                 for c in range(n)]
        while len(parts) > 1:
            parts = [a + b for a, b in zip(parts[::2], parts[1::2])] + parts[len(parts) & ~1:]
        return jnp.sum(parts[0], axis=0, keepdims=True)

    def bit_step(carry, bit, count):
        tu, cnt_t = carry
        cand_u = tu | lax.shift_left(jnp.int32(1), bit)
        cnt = count(_float_of_rank(cand_u))
        ok = cnt >= k_sel
        return jnp.where(ok, cand_u, tu), jnp.where(ok, cnt, cnt_t)

    def bis_body(s, carry):
        return bit_step(carry, 31 - s, lambda t: lax.switch(
            nch - 1, [functools.partial(count_chunks, n, t) for n in range(1, n_max + 1)]))

    start = (jnp.zeros((1, qb), I32), jnp.full((1, qb), k_sel, I32))
    units = s_ref.shape[0]
    bits_per_unit = SELECT_BITS // units

    def select_with_scores(n):
        def run():
            def body(u, carry):
                s_ref[u, 0:n * kc, :] = jnp.dot(kcat_ref[0:n * kc, :], qcat_ref[0, u],
                                                preferred_element_type=F32)
                for k in range(bits_per_unit):
                    carry = bit_step(carry, 31 - (u * bits_per_unit + k),
                                     functools.partial(count_chunks, n))
                return carry
            return lax.fori_loop(0, units, body, start)
        return run

    tu, cnt_t = lax.switch(nch - 1, [select_with_scores(n) for n in range(1, n_max + 1)])

    def trim(tu, cnt_t):
        thr = jnp.where(tu == 0, -jnp.inf, _float_of_rank(tu))
        few = thr < _ABOVE_NEG
        thr = jnp.where(few, _ABOVE_NEG, thr)
        cnt0 = jnp.where(few, k_sel, cnt_t)

        def trim_cond(state):
            _, cnt, n = state
            return (jnp.max(cnt) > k_sel) & (n < TRIM_PASSES)

        def trim_body(state):
            low, cnt, n = state

            def min_body(c, mn8):
                x = isc_ref[chunk_rows(c), :]
                sel = jnp.where(x >= thr, jnp.where(x > low, x, jnp.inf), jnp.inf)
                return jnp.minimum(mn8, fold(jnp.min, sel))

            mn = jnp.min(lax.fori_loop(0, nch, min_body, jnp.full((8, qb), jnp.inf, F32)),
                         axis=0, keepdims=True)

            def eq_body(c, cnt8):
                return cnt8 + fold(jnp.sum, jnp.where(isc_ref[chunk_rows(c), :] == mn, 1, 0).astype(I32))

            n_eq = jnp.sum(lax.fori_loop(0, nch, eq_body, jnp.zeros((8, qb), I32)),
                           axis=0, keepdims=True)
            can = (cnt > k_sel) & (cnt - n_eq >= k_sel)
            stalled = jnp.max(jnp.where(can, 1, 0)) == 0
            return (jnp.where(can, mn, low), jnp.where(can, cnt - n_eq, cnt),
                    jnp.where(stalled, TRIM_PASSES + 1, n + 1))

        low, cnt, n = lax.while_loop(trim_cond, trim_body,
                                     (jnp.full((1, qb), -jnp.inf, F32), cnt0, jnp.int32(0)))
        return thr, low, (n == TRIM_PASSES) & (jnp.max(cnt) > k_sel)

    thr, low, unresolved = trim(tu, cnt_t)
    thr, low = lax.cond(unresolved,
                        lambda: trim(*lax.fori_loop(SELECT_BITS, 32, bis_body, (tu, cnt_t)))[:2],
                        lambda: (thr, low))

    def fold_chunks(step, init):
        def body(j, carry):
            return step(2 * j + 1, step(2 * j, carry))

        carry = lax.fori_loop(0, nch // 2, body, init)
        return lax.cond(nch % 2 == 1, lambda cr: step(nch - 1, cr), lambda cr: cr, carry)

    def load_scores(rows):
        return jnp.concatenate([s_ref[u, rows, :] for u in range(units)], axis=1)

    def mask_and_max(c, m):
        rows = chunk_rows(c)
        x = isc_ref[rows, :]
        bias = jnp.where(x >= thr, jnp.where(x > low, 0.0, NEG), NEG).astype(F32)
        s = load_scores(rows) + jnp.concatenate([bias] * N_HEADS_A, axis=1)
        for u in range(units):
            s_ref[u, rows, :] = s[:, u * UNIT_COLS:(u + 1) * UNIT_COLS]
        return jnp.maximum(m, jnp.max(s, axis=0, keepdims=True))

    m = fold_chunks(mask_and_max, jnp.full((1, N_HEADS_A * qb), NEG, F32))
    acc_ref[...] = jnp.zeros(acc_ref.shape, F32)

    def exp_sum_pv(c, l):
        p = jnp.exp2(load_scores(chunk_rows(c)) - m)
        acc_ref[...] += jnp.dot(ckvt_ref[c], p.astype(BF16), preferred_element_type=F32)
        return l + jnp.sum(p, axis=0, keepdims=True)

    l = fold_chunks(exp_sum_pv, jnp.zeros((1, N_HEADS_A * qb), F32))

    inv_l = 1.0 / l
    outs = []
    for h in range(N_HEADS_A):
        cols = slice(h * qb, (h + 1) * qb)
        o_t = acc_ref[:, cols] * inv_l[:, cols]
        outs.append(jnp.dot(o_t.T.astype(BF16), wuv_ref[h], preferred_element_type=F32))
    o = jnp.concatenate(outs, axis=1)
    g = gate_ref[...].astype(F32)
    o_ref[...] = (o * _silu(g)).astype(BF16)


def _dsa_call(kidx, kcat, ckvt, qcat, qidx, w_t, gates, wuv, batch, seq):
    rows = kidx.shape[0]
    nb = seq // QUERY_BLOCK
    nkc = seq // KEY_CHUNK
    k_sel = min(TOPK_MAX, seq // 4)
    kern = functools.partial(_dsa_kernel, k_sel=k_sel)
    return pl.pallas_call(
        kern,
        grid=(batch, nb),
        in_specs=[pl.BlockSpec((seq, KIDX_PAD), lambda b, i: (b, 0)),
                  pl.BlockSpec((seq, KCAT), lambda b, i: (b, 0)),
                  pl.BlockSpec((nkc, KV_RANK, KEY_CHUNK), lambda b, i: (b, 0, 0)),
                  pl.BlockSpec((1, SCORE_UNITS, KCAT, UNIT_COLS), lambda b, i: (b * nb + i, 0, 0, 0)),
                  pl.BlockSpec((1, KIDX_PAD, IDX_HEADS * QUERY_BLOCK), lambda b, i: (b * nb + i, 0, 0)),
                  pl.BlockSpec((1, IDX_HEADS, QUERY_BLOCK), lambda b, i: (b * nb + i, 0, 0)),
                  pl.BlockSpec((QUERY_BLOCK, WIDTH_A), lambda b, i: (b * nb + i, 0)),
                  pl.BlockSpec((N_HEADS_A, KV_RANK, HEAD_DIM), lambda b, i: (0, 0, 0))],
        out_specs=pl.BlockSpec((QUERY_BLOCK, WIDTH_A), lambda b, i: (b * nb + i, 0)),
        out_shape=jax.ShapeDtypeStruct((rows, WIDTH_A), BF16),
        scratch_shapes=[pltpu.VMEM((seq, QUERY_BLOCK), F32),
                        pltpu.VMEM((KV_RANK, N_HEADS_A * QUERY_BLOCK), F32),
                        pltpu.VMEM((SCORE_UNITS, seq, UNIT_COLS), F32)],
        compiler_params=_cparams(("parallel", "arbitrary")),
        name="dsa_attn",
    )(kidx, kcat, ckvt, qcat, qidx, w_t, gates, wuv)


def _dil_kernel(q_ref, k_ref, v_ref, gate_ref, c_ref, s_ref, o_ref,
                nat_ref, a4_ref, qn, kn, vn, q4, k4, v4, q16, k16, v16,
                oc0, ms0, ls0, oc1, ms1, ls1, oc2, ms2, ls2, *, qscale, unroll):
    seq = q_ref.shape[0]
    blk = N_BACK
    rc = ROW_CHUNK
    m4 = seq // 4
    m16 = seq // 16

    def build(src_ref, dn, d4, d16, rope_scale):
        def nat_body(c, carry):
            rows = pl.ds(pl.multiple_of(c * rc, rc), rc)
            x = src_ref[rows, :]
            if rope_scale is not None:
                x = x * c_ref[rows, :] + pltpu.roll(x, LANE // 2, 1) * s_ref[rows, :]
                if rope_scale != 1.0:
                    x = x * rope_scale
                nat_ref[rows, :] = x
            dn[rows, :] = x.astype(BF16)
            return carry

        lax.fori_loop(0, seq // rc, nat_body, 0)
        nat = src_ref if rope_scale is None else nat_ref

        def a4_body(c, carry):
            r4 = c // (m4 // rc)
            j0 = (c % (m4 // rc)) * rc
            x = nat[pl.ds(r4 + 4 * j0, rc, stride=4), :]
            rows = pl.ds(pl.multiple_of(c * rc, rc), rc)
            a4_ref[rows, :] = x
            d4[rows, :] = x.astype(BF16)
            return carry

        lax.fori_loop(0, seq // rc, a4_body, 0)

        def a16_body(s, carry):
            for r4 in range(4):
                x = a4_ref[pl.ds(r4 * m4 + s, m16, stride=4), :]
                d16[pl.ds(pl.multiple_of((4 * s + r4) * m16, m16), m16), :] = x.astype(BF16)
            return carry

        lax.fori_loop(0, 4, a16_body, 0)

    build(q_ref, qn, q4, q16, qscale)
    build(k_ref, kn, k4, k16, 1.0)
    build(v_ref, vn, v4, v16, None)

    qi = lax.broadcasted_iota(I32, (blk, 2 * blk), 0) + blk
    kj = lax.broadcasted_iota(I32, (blk, 2 * blk), 1)
    rel = qi - kj
    bias_band = jnp.where((rel >= 0) & (rel <= N_BACK), 0.0, NEG).astype(F32)
    no_prev = jnp.where(kj < blk, NEG, 0.0).astype(F32)
    qi1 = lax.broadcasted_iota(I32, (blk, blk), 0)
    kj1 = lax.broadcasted_iota(I32, (blk, blk), 1)
    bias_first = jnp.where(kj1 <= qi1, 0.0, NEG).astype(F32)
    nt = (((1,), (1,)), ((), ()))

    def run_config(qs, ks, vs, nmb, store):
        def rows_of(u, first):
            cur = pl.ds(pl.multiple_of(u * blk, blk), blk)
            if first is True:
                return (cur,)
            return (pl.ds(pl.multiple_of(jnp.maximum(u * blk - blk, 0), blk), blk), cur)

        def gather(ref, rows):
            parts = [ref[r, :] for r in rows]
            return parts[0] if len(parts) == 1 else jnp.concatenate(parts, axis=0)

        def loop_body(g, carry):
            units = []
            for t in range(unroll):
                if nmb <= unroll:
                    first = (t % nmb == 0)
                else:
                    first = jnp.where(g == 0, 1.0, 0.0).astype(F32) if t == 0 else False
                units.append((g * unroll + t, first))
            scores = []
            for u, first in units:
                rows = rows_of(u, first)
                if first is True:
                    bias = bias_first
                elif first is False:
                    bias = bias_band
                else:
                    bias = bias_band + no_prev * first
                scores.append(lax.dot_general(qs[rows[-1], :], gather(ks, rows), nt,
                                              preferred_element_type=F32) + bias)
            probs = []
            for s in scores:
                m = jnp.max(s, axis=-1, keepdims=True)
                p = jnp.exp2(s - m)
                l = jnp.sum(p, axis=-1, keepdims=True)
                probs.append((p.astype(BF16), jnp.broadcast_to(m, (blk, LANE)),
                              jnp.broadcast_to(l, (blk, LANE))))
            for (u, first), (p, m_b, l_b) in zip(units, probs):
                store(u, jnp.dot(p, gather(vs, rows_of(u, first)), preferred_element_type=F32), m_b, l_b)
            return carry

        lax.fori_loop(0, seq // blk // unroll, loop_body, 0)

    def store_contig(oc, ms, ls):
        def store(u, o, m_b, l_b):
            rows = pl.ds(pl.multiple_of(u * blk, blk), blk)
            oc[rows, :] = o
            ms[rows, :] = m_b
            ls[rows, :] = l_b
        return store

    def store_a16_to_a4(u, o, m_b, l_b):
        r16 = u // (m16 // blk)
        mb = u % (m16 // blk)
        rows = pl.ds((r16 % 4) * m4 + 4 * blk * mb + r16 // 4, blk, stride=4)
        oc2[rows, :] = o
        ms2[rows, :] = m_b
        ls2[rows, :] = l_b

    run_config(qn, kn, vn, seq // blk, store_contig(oc0, ms0, ls0))
    run_config(q4, k4, v4, m4 // blk, store_contig(oc1, ms1, ls1))
    run_config(q16, k16, v16, m16 // blk, store_a16_to_a4)

    def merge_body(c, carry):
        r4 = c // (m4 // rc)
        j0 = (c % (m4 // rc)) * rc
        nat_rows = pl.ds(r4 + 4 * j0, rc, stride=4)
        rows = pl.ds(pl.multiple_of(c * rc, rc), rc)
        m0 = ms0[nat_rows, :]
        m1 = ms1[rows, :]
        m2 = ms2[rows, :]
        mx = jnp.maximum(jnp.maximum(m0, m1), m2)
        e0 = jnp.exp2(m0 - mx)
        e1 = jnp.exp2(m1 - mx)
        e2 = jnp.exp2(m2 - mx)
        den = e0 * ls0[nat_rows, :] + e1 * ls1[rows, :] + e2 * ls2[rows, :]
        nat_ref[nat_rows, :] = (e0 * oc0[nat_rows, :] + e1 * oc1[rows, :] + e2 * oc2[rows, :]) / den
        return carry

    lax.fori_loop(0, seq // rc, merge_body, 0)

    def gate_body(c, carry):
        rows = pl.ds(pl.multiple_of(c * rc, rc), rc)
        g = gate_ref[rows, :].astype(F32)
        o_ref[rows, :] = (nat_ref[rows, :] * _silu(g)).astype(BF16)
        return carry

    lax.fori_loop(0, seq // rc, gate_body, 0)


def _dil_call(qkv, gates, ctab, stab, batch, seq):
    rows = qkv.shape[0]
    nh = N_HEADS_B
    qscale = HEAD_DIM ** -0.5 * LOG2E
    kern = functools.partial(_dil_kernel, qscale=qscale, unroll=DIL_UNROLL)
    blk = (seq, HEAD_DIM)
    return pl.pallas_call(
        kern,
        grid=(batch, nh),
        in_specs=[pl.BlockSpec(blk, lambda b, h: (b, h)),
                  pl.BlockSpec(blk, lambda b, h: (b, nh + h)),
                  pl.BlockSpec(blk, lambda b, h: (b, 2 * nh + h)),
                  pl.BlockSpec(blk, lambda b, h: (b, nh + h)),
                  pl.BlockSpec(blk, lambda b, h: (0, 0), pipeline_mode=pl.Buffered(1)),
                  pl.BlockSpec(blk, lambda b, h: (0, 0), pipeline_mode=pl.Buffered(1))],
        out_specs=pl.BlockSpec(blk, lambda b, h: (b, h)),
        out_shape=jax.ShapeDtypeStruct((rows, WIDTH_B), BF16),
        scratch_shapes=([pltpu.VMEM(blk, F32)] * 2 + [pltpu.VMEM(blk, BF16)] * 9
                        + [pltpu.VMEM(blk, F32)] * 9),
        compiler_params=_cparams(("parallel", "arbitrary")),
        name="dilated",
    )(qkv, qkv, qkv, gates, ctab, stab)


def _out_kernel(oa_ref, ob_ref, w_ref, x_ref, mod_ref, g_ref, o_ref):
    y = jnp.dot(oa_ref[...], w_ref[0:WIDTH_A, :], preferred_element_type=F32)
    y = y + jnp.dot(ob_ref[...], w_ref[WIDTH_A:WIDTH_A + WIDTH_B, :], preferred_element_type=F32)
    yn = y * lax.rsqrt(jnp.mean(y * y, axis=-1, keepdims=True) + EPS) * g_ref[...]
    o_ref[...] = x_ref[...] + mod_ref[0, 2:3, :] * yn


def _out_call(o_a, o_b, w_out, x2, mod3, g_post, seq):
    rows, d = x2.shape
    tm = OUT_TM
    tpb = seq // tm
    return pl.pallas_call(
        _out_kernel,
        grid=(rows // tm,),
        in_specs=[pl.BlockSpec((tm, WIDTH_A), lambda i: (i, 0)),
                  pl.BlockSpec((tm, WIDTH_B), lambda i: (i, 0)),
                  pl.BlockSpec((WIDTH_A + WIDTH_B, d), lambda i: (0, 0)),
                  pl.BlockSpec((tm, d), lambda i: (i, 0)),
                  pl.BlockSpec((1, 3, d), lambda i: (i // tpb, 0, 0)),
                  pl.BlockSpec((1, d), lambda i: (0, 0))],
        out_specs=pl.BlockSpec((tm, d), lambda i: (i, 0)),
        out_shape=jax.ShapeDtypeStruct((rows, d), F32),
        compiler_params=_cparams(("parallel",)),
        name="out_proj",
    )(o_a, o_b, w_out, x2, mod3, g_post)


def _rope_tables(seq):
    pos = jnp.arange(seq, dtype=F32)

    def cs(n_rot):
        inv = ROPE_THETA ** (-jnp.arange(0, n_rot, 2, dtype=F32) / n_rot)
        ang = pos[:, None] * inv[None, :]
        return jnp.cos(ang), jnp.sin(ang)

    c32, s32 = cs(ROPE_DIM)
    c16, s16 = cs(IDX_ROPE)
    ck = jnp.concatenate([c32, c32, jnp.ones((seq, LANE - ROPE_DIM), F32)], axis=1)
    sk = jnp.concatenate([-s32, s32, jnp.zeros((seq, LANE - ROPE_DIM), F32)], axis=1)
    ci = jnp.concatenate([c16, c16, jnp.ones((seq, LANE - IDX_ROPE), F32)], axis=1)
    si = jnp.concatenate([-s16, s16, jnp.zeros((seq, LANE - IDX_ROPE), F32)], axis=1)
    gap = jnp.ones((seq, LANE // 2 - ROPE_DIM // 2), F32)
    cd = jnp.concatenate([c32, gap, c32, gap], axis=1)
    sd = jnp.concatenate([-s32, 0.0 * gap, s32, 0.0 * gap], axis=1)
    return (c32.T, s32.T, c16.T, s16.T, ck, sk, ci, si, cd, sd)


def _pair_heads(wt):
    half = ROPE_DIM // 2
    mid = ROPE_DIM + (LANE // 2 - half)
    w3 = wt.reshape(-1, HEAD_DIM, wt.shape[-1])
    w3 = jnp.concatenate([w3[:, :half], w3[:, ROPE_DIM:mid], w3[:, half:ROPE_DIM], w3[:, mid:]], axis=1)
    return w3.reshape(wt.shape)


def _layer(x, c, w_ada, b_ada, g_pre, g_post, w_in, g_q, g_kv, w_uq, w_uq_idx, w_uk, w_uv, w_out):
    batch, seq, d = x.shape
    assert seq % PROJ_TM == 0 or seq < PROJ_TM
    assert seq % KEY_CHUNK == 0 and seq // 16 >= N_BACK
    assert all(w // dl == N_BACK for w, dl in DILATED_CONFIGS)
    rows = batch * seq
    x2 = x.reshape(rows, d)

    o_ki = Q_RANK + KV_RANK + ROPE_DIM
    o_ga = o_ki + IDX_DIM + IDX_HEADS
    o_q = o_ga + WIDTH_A
    o_k, o_v, o_gb = o_q + WIDTH_B, o_q + 2 * WIDTH_B, o_q + 3 * WIDTH_B
    assert o_gb + WIDTH_B == w_in.shape[1]
    wt = w_in.T.astype(BF16)
    zeros = lambda n: jnp.zeros((n, d), BF16)
    w_all = jnp.concatenate([
        wt[:o_ki], zeros(SLAB_KIDX - o_ki), wt[o_ki:o_ga], zeros(SMALL_W - SLAB_KIDX - (o_ga - o_ki)),
        _pair_heads(wt[o_q:o_k]), _pair_heads(wt[o_k:o_v]), wt[o_v:o_gb],
        wt[o_ga:o_q], wt[o_gb:o_gb + WIDTH_B]], axis=0)
    assert w_all.shape[0] == IN_PAD

    c_pad = jnp.zeros((PACK16, d), F32).at[:batch].set(c)
    mod = _mod_call(c_pad, w_ada, b_ada.reshape(1, -1))[:batch]
    mod3 = mod.reshape(batch, 3, d)

    small, qkv, gates = _proj_call(x2, mod3, g_pre.reshape(1, d), w_all, seq)

    tabs = _rope_tables(seq)
    wuk_t = jnp.transpose(jnp.pad(w_uk, ((0, 0), (ROPE_DIM, 0), (0, 0))), (0, 2, 1)).astype(BF16)
    qcat, qidx, w_t, kcat, kidx, ckvt = _prep_call(
        small, g_q.reshape(1, -1), g_kv.reshape(1, -1), w_uq.T.astype(BF16),
        w_uq_idx.T.astype(BF16), wuk_t, tabs[:8], seq)

    o_a = _dsa_call(kidx, kcat, ckvt, qcat, qidx, w_t, gates, w_uv.astype(BF16), batch, seq)
    o_b = _dil_call(qkv, gates, tabs[8], tabs[9], batch, seq)
    out = _out_call(o_a, o_b, w_out.astype(BF16), x2, mod3, g_post.reshape(1, d), seq)
    return out.reshape(batch, seq, d)


def kernel(x, c, w_ada, b_ada, g_pre, g_post, w_in, g_q, g_kv, w_uq, w_uq_idx, w_uk, w_uv, w_out):
    for layer in range(w_ada.shape[0]):
        x = _layer(x, c, w_ada[layer], b_ada[layer], g_pre[layer], g_post[layer], w_in[layer],
                   g_q[layer], g_kv[layer], w_uq[layer], w_uq_idx[layer], w_uk[layer],
                   w_uv[layer], w_out[layer])
    return x
```
